```python
import jax, jax.numpy as jnp
from jax import lax
import numpy as np

D_MODEL = 2048
BATCH = 8
SEQ = 4096
DEPTH = 4

GRID_W = 64
CTX_LEN = 256
EPS = 1e-6

D_MIX = D_MODEL
GLA_HEADS = 4
GLA_DK = 128
GLA_DV = 128
GLA_W = GLA_HEADS * GLA_DV
GLA_GATE_RANK = 16
GLA_GATE_NORM = 16.0
GLA_CHUNK = 64
CONF_W = 512
CONF_KERNEL = 31
SC_W = 512
SC_KERNEL = 3
MLA_HEADS = 4
MLA_NOPE = 128
MLA_ROPE = 64
MLA_V = 128
MLA_W = MLA_HEADS * MLA_V
MLA_Q_RANK = 384
MLA_KV_RANK = 128
Q_BLOCK = 128
AXIS_DIM = MLA_ROPE // 2
ROPE_FREQS = AXIS_DIM // 2
ROPE_BASE = 10000.0
D_FF = -(-8 * D_MODEL // (3 * 256)) * 256

IN_SIZES = (GLA_HEADS * GLA_DK, GLA_HEADS * GLA_DK, GLA_W, GLA_W, GLA_GATE_RANK, GLA_GATE_RANK,
            CONF_W, CONF_W,
            SC_W, SC_W, SC_W,
            MLA_Q_RANK, MLA_KV_RANK, MLA_ROPE)
IN_W = sum(IN_SIZES)
IN_SPLITS = tuple(int(s) for s in np.cumsum(IN_SIZES)[:-1])

kernel_name = "hybrid_parallel_heads_dit_trunk"


def rms_norm(x, g):
    xf = x.astype(jnp.float32)
    y = xf * lax.rsqrt(jnp.mean(xf * xf, axis=-1, keepdims=True) + EPS)
    return (y * g.astype(jnp.float32)).astype(x.dtype)


def layer_norm(x, g, b):
    xf = x.astype(jnp.float32)
    mu = jnp.mean(xf, axis=-1, keepdims=True)
    xc = xf - mu
    y = xc * lax.rsqrt(jnp.mean(xc * xc, axis=-1, keepdims=True) + EPS)
    return (y * g.astype(jnp.float32) + b.astype(jnp.float32)).astype(x.dtype)


def depthwise_conv(u, w):
    return lax.conv_general_dilated(u, w[:, None, :].astype(u.dtype), window_strides=(1,), padding='SAME',
                                    dimension_numbers=('NWC', 'WIO', 'NWC'),
                                    feature_group_count=u.shape[-1])


def to_heads(t, d):
    return jnp.swapaxes(t.reshape(t.shape[0], t.shape[1], -1, d), 1, 2)


def rope_tables(L):
    rows = L // GRID_W
    row = jnp.repeat(jnp.arange(rows, dtype=jnp.float32), GRID_W)
    col = jnp.tile(jnp.arange(GRID_W, dtype=jnp.float32), rows)
    inv = ROPE_BASE ** (-jnp.arange(ROPE_FREQS, dtype=jnp.float32) * 2.0 / AXIS_DIM)
    ang = jnp.stack([row[:, None] * inv, col[:, None] * inv], axis=1)
    return jnp.cos(ang), jnp.sin(ang)


def apply_rope(t, cos, sin):
    tr = t.reshape(t.shape[:-1] + (2, 2, ROPE_FREQS)).astype(jnp.float32)
    t1, t2 = tr[..., 0, :], tr[..., 1, :]
    out = jnp.stack([t1 * cos - t2 * sin, t2 * cos + t1 * sin], axis=-2)
    return out.reshape(t.shape).astype(t.dtype)


def gla_chunked(q, k, v, logd, s0):
    q, k, v, logd = (t.astype(jnp.float32) for t in (q, k, v, logd))
    b_, h_, L, _ = q.shape
    n = L // GLA_CHUNK

    def chunks(t):
        return jnp.moveaxis(t.reshape(b_, h_, n, GLA_CHUNK, t.shape[-1]), 2, 0)

    idx = jnp.arange(GLA_CHUNK)
    lower = (idx[:, None] >= idx[None, :])[..., None]

    def step(s, inp):
        qi, ki, vi, gi = inp
        bcum = jnp.cumsum(gi, axis=2)
        o_inter = jnp.einsum('bhcd,bhde->bhce', qi * jnp.exp(bcum), s)
        diff = bcum[:, :, :, None, :] - bcum[:, :, None, :, :]
        decay = jnp.where(lower, jnp.exp(jnp.minimum(diff, 0.0)), 0.0)
        scores = jnp.einsum('bhid,bhjd,bhijd->bhij', qi, ki, decay)
        o_intra = jnp.einsum('bhij,bhje->bhie', scores, vi)
        blast = bcum[:, :, -1:, :]
        s_new = jnp.exp(blast[:, :, 0, :])[..., None] * s + jnp.einsum(
            'bhcd,bhce->bhde', ki * jnp.exp(blast - bcum), vi)
        return s_new, o_inter + o_intra

    s_fin, oc = lax.scan(step, s0.astype(jnp.float32), (chunks(q), chunks(k), chunks(v), chunks(logd)))
    return jnp.moveaxis(oc, 0, 2).reshape(b_, h_, L, -1), s_fin


def gla_mixer(zc, zl, fg_up, fg_b, onorm_g, need_ctx):
    def prep(z):
        q, k, v, g, lr_f, lr_b = z
        q = to_heads(q, GLA_DK) * (GLA_DK ** -0.5)
        lds = [to_heads(jax.nn.log_sigmoid((lr @ fg_up[d] + fg_b[d]).astype(jnp.float32)) / GLA_GATE_NORM,
                        GLA_DK) for d, lr in enumerate((lr_f, lr_b))]
        return q, to_heads(k, GLA_DK), to_heads(v, GLA_DV), g, lds

    qc, kc, vc, gc, ldc = prep(zc)
    ql, kl, vl, gl, ldl = prep(zl)
    flip = lambda t: jnp.flip(t, axis=2)
    zero = jnp.zeros(qc.shape[:2] + (GLA_DK, GLA_DV), jnp.float32)
    oc_f, s_cf = gla_chunked(qc, kc, vc, ldc[0], zero)
    ol_f, _ = gla_chunked(ql, kl, vl, ldl[0], s_cf)
    oc_b, s_cb = gla_chunked(flip(qc), flip(kc), flip(vc), flip(ldc[1]), zero)
    ol_b, _ = gla_chunked(flip(ql), flip(kl), flip(vl), flip(ldl[1]), s_cb)

    def finish(o, g):
        o = rms_norm(o, onorm_g)
        o = jnp.swapaxes(o, 1, 2).reshape(g.shape[0], g.shape[1], GLA_W)
        return (o * jax.nn.silu(g.astype(jnp.float32))).astype(g.dtype)

    out_l = finish(ol_f + flip(ol_b), gl)
    out_c = finish(oc_f + flip(oc_b), gc) if need_ctx else None
    return out_c, out_l


def conformer_conv(a, gate, dw, dw_b, ln_g, ln_b):
    u = a * jax.nn.sigmoid(gate)
    u = depthwise_conv(u, dw) + dw_b
    return jax.nn.silu(layer_norm(u, ln_g, ln_b))


def short_conv(bg, cg, h, dw):
    return bg * depthwise_conv(cg * h, dw)


def block_attention(qn, qr, kn, kr, v):
    b_, lq, h_, _ = qn.shape
    nb = lq // Q_BLOCK
    scale = (MLA_NOPE + MLA_ROPE) ** -0.5

    def blk(args):
        qn_b, qr_b = args
        s = jnp.einsum('bqhd,bkhd->bhqk', qn_b, kn) + jnp.einsum('bqhr,bkr->bhqk', qr_b, kr)
        p = jax.nn.softmax(s.astype(jnp.float32) * scale, axis=-1).astype(v.dtype)
        return jnp.einsum('bhqk,bkhd->bqhd', p, v)

    split = lambda t: jnp.moveaxis(t.reshape(b_, nb, Q_BLOCK, h_, t.shape[-1]), 1, 0)
    out = lax.map(blk, (split(qn), split(qr)))
    return jnp.moveaxis(out, 0, 1).reshape(b_, lq, h_ * MLA_V)


def mla_mixer(zc, zl, q_norm_g, kv_norm_g, w_uq, w_ukv, cos, sin, need_ctx):
    def project(z):
        cq, ckv, kr = z
        b_, n = cq.shape[:2]
        q = (rms_norm(cq, q_norm_g) @ w_uq).reshape(b_, n, MLA_HEADS, MLA_NOPE + MLA_ROPE)
        kv = (rms_norm(ckv, kv_norm_g) @ w_ukv).reshape(b_, n, MLA_HEADS, MLA_NOPE + MLA_V)
        return q[..., :MLA_NOPE], q[..., MLA_NOPE:], kv[..., :MLA_NOPE], kr, kv[..., MLA_NOPE:]

    qn_c, qr_c, kn_c, kr_c, v_c = project(zc)
    qn_l, qr_l, kn_l, kr_l, v_l = project(zl)
    qr_l = apply_rope(qr_l, cos[:, None], sin[:, None])
    kr_l = apply_rope(kr_l, cos, sin)
    out_l = block_attention(qn_l, qr_l, jnp.concatenate([kn_c, kn_l], axis=1),
                            jnp.concatenate([kr_c, kr_l], axis=1), jnp.concatenate([v_c, v_l], axis=1))
    out_c = block_attention(qn_c, qr_c, kn_c, kr_c, v_c) if need_ctx else None
    return out_c, out_l


def swiglu(h, w1, w3, w2):
    return (jax.nn.silu(h @ w1) * (h @ w3)) @ w2


def trunk_layer(x, ctx, mod_l, mod_c, p, cos, sin, need_ctx):
    sh1, sc1, g1, sh2, sc2, g2 = jnp.split(mod_l[:, None, :], 6, axis=-1)
    csh1, csc1, cg1, csh2, csc2, cg2 = jnp.split(mod_c, 6, axis=-1)
    n_ctx = ctx.shape[1]
    hl = rms_norm(x, p['norm1_g']) * (1 + sc1) + sh1
    hc = rms_norm(ctx, p['norm1_g']) * (1 + csc1) + csh1
    z = jnp.concatenate([hc, hl], axis=1) @ p['w_in']
    zc = jnp.split(z[:, :n_ctx], IN_SPLITS, axis=-1)
    zl = jnp.split(z[:, n_ctx:], IN_SPLITS, axis=-1)

    gla_c, gla_l = gla_mixer(zc[0:6], zl[0:6], p['gla_fg_up'], p['gla_fg_b'], p['gla_onorm_g'], need_ctx)
    conf = lambda zz: conformer_conv(zz[6], zz[7], p['conf_dw'], p['conf_dw_b'], p['conf_ln_g'], p['conf_ln_b'])
    sconv = lambda zz: short_conv(zz[8], zz[9], zz[10], p['sc_dw'])
    mla_c, mla_l = mla_mixer(zc[11:14], zl[11:14], p['mla_q_norm_g'], p['mla_kv_norm_g'],
                             p['mla_w_uq'], p['mla_w_ukv'], cos, sin, need_ctx)

    o_l = jnp.concatenate([gla_l, conf(zl), sconv(zl), mla_l], axis=-1) @ p['w_out']
    x = x + g1 * o_l
    x = x + g2 * swiglu(rms_norm(x, p['norm2_g']) * (1 + sc2) + sh2, p['ffn_w1'], p['ffn_w3'], p['ffn_w2'])
    if need_ctx:
        o_c = jnp.concatenate([gla_c, conf(zc), sconv(zc), mla_c], axis=-1) @ p['w_out']
        ctx = ctx + cg1 * o_c
        ctx = ctx + cg2 * swiglu(rms_norm(ctx, p['norm2_g']) * (1 + csc2) + csh2,
                                 p['ffn_w1'], p['ffn_w3'], p['ffn_w2'])
    return x, ctx


def _fwd_setup_inputs(seed: int = 0) -> dict:
    key = jax.random.key(seed)
    ks = iter(jax.random.split(key, 32))
    D = D_MODEL
    nrm = lambda shape, scale: jax.random.normal(next(ks), shape, jnp.float32) * scale
    gain = lambda shape: 1.0 + nrm(shape, 0.05)
    return {
        "x": nrm((BATCH, SEQ, D), 1.0),
        "c": nrm((BATCH, D), 1.0),
        "ctx": nrm((BATCH, CTX_LEN, D), 1.0),
        "c_ctx": nrm((D,), 1.0),
        "norm1_g": gain((DEPTH, D)),
        "w_mod": nrm((DEPTH, D, 6 * D), 0.5 * D ** -0.5),
        "b_mod": nrm((DEPTH, 6 * D), 0.02),
        "w_in": nrm((DEPTH, D, IN_W), D ** -0.5),
        "gla_fg_up": nrm((DEPTH, 2, GLA_GATE_RANK, GLA_HEADS * GLA_DK), GLA_GATE_RANK ** -0.5),
        "gla_fg_b": nrm((DEPTH, 2, GLA_HEADS * GLA_DK), 0.1),
        "gla_onorm_g": gain((DEPTH, GLA_DV)),
        "conf_dw": nrm((DEPTH, CONF_KERNEL, CONF_W), CONF_KERNEL ** -0.5),
        "conf_dw_b": nrm((DEPTH, CONF_W), 0.02),
        "conf_ln_g": gain((DEPTH, CONF_W)),
        "conf_ln_b": nrm((DEPTH, CONF_W), 0.02),
        "sc_dw": nrm((DEPTH, SC_KERNEL, SC_W), SC_KERNEL ** -0.5),
        "mla_q_norm_g": gain((DEPTH, MLA_Q_RANK)),
        "mla_kv_norm_g": gain((DEPTH, MLA_KV_RANK)),
        "mla_w_uq": nrm((DEPTH, MLA_Q_RANK, MLA_HEADS * (MLA_NOPE + MLA_ROPE)), MLA_Q_RANK ** -0.5),
        "mla_w_ukv": nrm((DEPTH, MLA_KV_RANK, MLA_HEADS * (MLA_NOPE + MLA_V)), MLA_KV_RANK ** -0.5),
        "w_out": nrm((DEPTH, D_MIX, D), D_MIX ** -0.5),
        "norm2_g": gain((DEPTH, D)),
        "ffn_w1": nrm((DEPTH, D, D_FF), D ** -0.5),
        "ffn_w3": nrm((DEPTH, D, D_FF), D ** -0.5),
        "ffn_w2": nrm((DEPTH, D_FF, D), D_FF ** -0.5),
        "final_norm_g": gain((D,)),
    }


def _fwd_reference(x, c, ctx, c_ctx, norm1_g, w_mod, b_mod, w_in, gla_fg_up, gla_fg_b, gla_onorm_g,
              conf_dw, conf_dw_b, conf_ln_g, conf_ln_b, sc_dw, mla_q_norm_g, mla_kv_norm_g,
              mla_w_uq, mla_w_ukv, w_out, norm2_g, ffn_w1, ffn_w3, ffn_w2, final_norm_g):
    cos, sin = rope_tables(x.shape[1])
    c_act = jax.nn.silu(c)
    cc_act = jax.nn.silu(c_ctx)
    for i in range(DEPTH):
        mod_l = c_act @ w_mod[i] + b_mod[i]
        mod_c = cc_act @ w_mod[i] + b_mod[i]
        p = dict(norm1_g=norm1_g[i], w_in=w_in[i], gla_fg_up=gla_fg_up[i], gla_fg_b=gla_fg_b[i],
                 gla_onorm_g=gla_onorm_g[i], conf_dw=conf_dw[i], conf_dw_b=conf_dw_b[i],
                 conf_ln_g=conf_ln_g[i], conf_ln_b=conf_ln_b[i], sc_dw=sc_dw[i],
                 mla_q_norm_g=mla_q_norm_g[i], mla_kv_norm_g=mla_kv_norm_g[i], mla_w_uq=mla_w_uq[i],
                 mla_w_ukv=mla_w_ukv[i], w_out=w_out[i], norm2_g=norm2_g[i], ffn_w1=ffn_w1[i],
                 ffn_w3=ffn_w3[i], ffn_w2=ffn_w2[i])
        x, ctx = trunk_layer(x, ctx, mod_l, mod_c, p, cos, sin, need_ctx=(i < DEPTH - 1))
    return rms_norm(x, final_norm_g)


import jax as _jax
import jax.numpy as _jnp

TWIN_FORMAT = 'train_step'
FWD_PARAMS = ['x', 'c', 'ctx', 'c_ctx', 'norm1_g', 'w_mod', 'b_mod', 'w_in', 'gla_fg_up', 'gla_fg_b', 'gla_onorm_g', 'conf_dw', 'conf_dw_b', 'conf_ln_g', 'conf_ln_b', 'sc_dw', 'mla_q_norm_g', 'mla_kv_norm_g', 'mla_w_uq', 'mla_w_ukv', 'w_out', 'norm2_g', 'ffn_w1', 'ffn_w3', 'ffn_w2', 'final_norm_g']
TWIN_WEIGHTS = ['c_ctx', 'norm1_g', 'w_mod', 'b_mod', 'w_in', 'gla_fg_up', 'gla_fg_b', 'gla_onorm_g', 'conf_dw', 'conf_dw_b', 'conf_ln_g', 'conf_ln_b', 'sc_dw', 'mla_q_norm_g', 'mla_kv_norm_g', 'mla_w_uq', 'mla_w_ukv', 'w_out', 'norm2_g', 'ffn_w1', 'ffn_w3', 'ffn_w2', 'final_norm_g']
TWIN_DIFF_INPUT = 'x'
TWIN_INPUTS = ['x', 'c', 'ctx', 'c_ctx', 'norm1_g', 'w_mod', 'b_mod', 'w_in', 'gla_fg_up', 'gla_fg_b', 'gla_onorm_g', 'conf_dw', 'conf_dw_b', 'conf_ln_g', 'conf_ln_b', 'sc_dw', 'mla_q_norm_g', 'mla_kv_norm_g', 'mla_w_uq', 'mla_w_ukv', 'w_out', 'norm2_g', 'ffn_w1', 'ffn_w3', 'ffn_w2', 'final_norm_g', 'loss_target', 'm_c_ctx', 'm_norm1_g', 'm_w_mod', 'm_b_mod', 'm_w_in', 'm_gla_fg_up', 'm_gla_fg_b', 'm_gla_onorm_g', 'm_conf_dw', 'm_conf_dw_b', 'm_conf_ln_g', 'm_conf_ln_b', 'm_sc_dw', 'm_mla_q_norm_g', 'm_mla_kv_norm_g', 'm_mla_w_uq', 'm_mla_w_ukv', 'm_w_out', 'm_norm2_g', 'm_ffn_w1', 'm_ffn_w3', 'm_ffn_w2', 'm_final_norm_g', 'v_c_ctx', 'v_norm1_g', 'v_w_mod', 'v_b_mod', 'v_w_in', 'v_gla_fg_up', 'v_gla_fg_b', 'v_gla_onorm_g', 'v_conf_dw', 'v_conf_dw_b', 'v_conf_ln_g', 'v_conf_ln_b', 'v_sc_dw', 'v_mla_q_norm_g', 'v_mla_kv_norm_g', 'v_mla_w_uq', 'v_mla_w_ukv', 'v_w_out', 'v_norm2_g', 'v_ffn_w1', 'v_ffn_w3', 'v_ffn_w2', 'v_final_norm_g']
TWIN_OUTPUTS = ['loss', 'grad_x', 'grad_c_ctx', 'grad_norm1_g', 'grad_w_mod', 'grad_b_mod', 'grad_w_in', 'grad_gla_fg_up', 'grad_gla_fg_b', 'grad_gla_onorm_g', 'grad_conf_dw', 'grad_conf_dw_b', 'grad_conf_ln_g', 'grad_conf_ln_b', 'grad_sc_dw', 'grad_mla_q_norm_g', 'grad_mla_kv_norm_g', 'grad_mla_w_uq', 'grad_mla_w_ukv', 'grad_w_out', 'grad_norm2_g', 'grad_ffn_w1', 'grad_ffn_w3', 'grad_ffn_w2', 'grad_final_norm_g', 'delta_c_ctx', 'delta_norm1_g', 'delta_w_mod', 'delta_b_mod', 'delta_w_in', 'delta_gla_fg_up', 'delta_gla_fg_b', 'delta_gla_onorm_g', 'delta_conf_dw', 'delta_conf_dw_b', 'delta_conf_ln_g', 'delta_conf_ln_b', 'delta_sc_dw', 'delta_mla_q_norm_g', 'delta_mla_kv_norm_g', 'delta_mla_w_uq', 'delta_mla_w_ukv', 'delta_w_out', 'delta_norm2_g', 'delta_ffn_w1', 'delta_ffn_w3', 'delta_ffn_w2', 'delta_final_norm_g', 'new_m_c_ctx', 'new_m_norm1_g', 'new_m_w_mod', 'new_m_b_mod', 'new_m_w_in', 'new_m_gla_fg_up', 'new_m_gla_fg_b', 'new_m_gla_onorm_g', 'new_m_conf_dw', 'new_m_conf_dw_b', 'new_m_conf_ln_g', 'new_m_conf_ln_b', 'new_m_sc_dw', 'new_m_mla_q_norm_g', 'new_m_mla_kv_norm_g', 'new_m_mla_w_uq', 'new_m_mla_w_ukv', 'new_m_w_out', 'new_m_norm2_g', 'new_m_ffn_w1', 'new_m_ffn_w3', 'new_m_ffn_w2', 'new_m_final_norm_g', 'new_v_c_ctx', 'new_v_norm1_g', 'new_v_w_mod', 'new_v_b_mod', 'new_v_w_in', 'new_v_gla_fg_up', 'new_v_gla_fg_b', 'new_v_gla_onorm_g', 'new_v_conf_dw', 'new_v_conf_dw_b', 'new_v_conf_ln_g', 'new_v_conf_ln_b', 'new_v_sc_dw', 'new_v_mla_q_norm_g', 'new_v_mla_kv_norm_g', 'new_v_mla_w_uq', 'new_v_mla_w_ukv', 'new_v_w_out', 'new_v_norm2_g', 'new_v_ffn_w1', 'new_v_ffn_w3', 'new_v_ffn_w2', 'new_v_final_norm_g']
TWIN_LEAF_KINDS = {'loss': 'loss', 'grad_x': 'grad_x', 'grad_c_ctx': 'grad_w', 'grad_norm1_g': 'grad_w', 'grad_w_mod': 'grad_w', 'grad_b_mod': 'grad_w', 'grad_w_in': 'grad_w', 'grad_gla_fg_up': 'grad_w', 'grad_gla_fg_b': 'grad_w', 'grad_gla_onorm_g': 'grad_w', 'grad_conf_dw': 'grad_w', 'grad_conf_dw_b': 'grad_w', 'grad_conf_ln_g': 'grad_w', 'grad_conf_ln_b': 'grad_w', 'grad_sc_dw': 'grad_w', 'grad_mla_q_norm_g': 'grad_w', 'grad_mla_kv_norm_g': 'grad_w', 'grad_mla_w_uq': 'grad_w', 'grad_mla_w_ukv': 'grad_w', 'grad_w_out': 'grad_w', 'grad_norm2_g': 'grad_w', 'grad_ffn_w1': 'grad_w', 'grad_ffn_w3': 'grad_w', 'grad_ffn_w2': 'grad_w', 'grad_final_norm_g': 'grad_w', 'delta_c_ctx': 'delta_w', 'delta_norm1_g': 'delta_w', 'delta_w_mod': 'delta_w', 'delta_b_mod': 'delta_w', 'delta_w_in': 'delta_w', 'delta_gla_fg_up': 'delta_w', 'delta_gla_fg_b': 'delta_w', 'delta_gla_onorm_g': 'delta_w', 'delta_conf_dw': 'delta_w', 'delta_conf_dw_b': 'delta_w', 'delta_conf_ln_g': 'delta_w', 'delta_conf_ln_b': 'delta_w', 'delta_sc_dw': 'delta_w', 'delta_mla_q_norm_g': 'delta_w', 'delta_mla_kv_norm_g': 'delta_w', 'delta_mla_w_uq': 'delta_w', 'delta_mla_w_ukv': 'delta_w', 'delta_w_out': 'delta_w', 'delta_norm2_g': 'delta_w', 'delta_ffn_w1': 'delta_w', 'delta_ffn_w3': 'delta_w', 'delta_ffn_w2': 'delta_w', 'delta_final_norm_g': 'delta_w', 'new_m_c_ctx': 'new_m', 'new_m_norm1_g': 'new_m', 'new_m_w_mod': 'new_m', 'new_m_b_mod': 'new_m', 'new_m_w_in': 'new_m', 'new_m_gla_fg_up': 'new_m', 'new_m_gla_fg_b': 'new_m', 'new_m_gla_onorm_g': 'new_m', 'new_m_conf_dw': 'new_m', 'new_m_conf_dw_b': 'new_m', 'new_m_conf_ln_g': 'new_m', 'new_m_conf_ln_b': 'new_m', 'new_m_sc_dw': 'new_m', 'new_m_mla_q_norm_g': 'new_m', 'new_m_mla_kv_norm_g': 'new_m', 'new_m_mla_w_uq': 'new_m', 'new_m_mla_w_ukv': 'new_m', 'new_m_w_out': 'new_m', 'new_m_norm2_g': 'new_m', 'new_m_ffn_w1': 'new_m', 'new_m_ffn_w3': 'new_m', 'new_m_ffn_w2': 'new_m', 'new_m_final_norm_g': 'new_m', 'new_v_c_ctx': 'new_v', 'new_v_norm1_g': 'new_v', 'new_v_w_mod': 'new_v', 'new_v_b_mod': 'new_v', 'new_v_w_in': 'new_v', 'new_v_gla_fg_up': 'new_v', 'new_v_gla_fg_b': 'new_v', 'new_v_gla_onorm_g': 'new_v', 'new_v_conf_dw': 'new_v', 'new_v_conf_dw_b': 'new_v', 'new_v_conf_ln_g': 'new_v', 'new_v_conf_ln_b': 'new_v', 'new_v_sc_dw': 'new_v', 'new_v_mla_q_norm_g': 'new_v', 'new_v_mla_kv_norm_g': 'new_v', 'new_v_mla_w_uq': 'new_v', 'new_v_mla_w_ukv': 'new_v', 'new_v_w_out': 'new_v', 'new_v_norm2_g': 'new_v', 'new_v_ffn_w1': 'new_v', 'new_v_ffn_w3': 'new_v', 'new_v_ffn_w2': 'new_v', 'new_v_final_norm_g': 'new_v'}


def _forward(args):
    return _fwd_reference(*[args[k] for k in FWD_PARAMS])


def _output_shape():
    def fwd():
        inp = _fwd_setup_inputs(0)
        return _fwd_reference(*[inp[k] for k in FWD_PARAMS])
    out = _jax.eval_shape(fwd)
    return out.shape, out.dtype

N_MICROBATCH = 1
ADAM_LR = 0.001
ADAM_B1 = 0.9
ADAM_B2 = 0.999
ADAM_EPS = 1e-08
ADAM_WD = 0.01
ADAM_STEP = 10
PER_EXAMPLE_BATCH_AXIS = {'x': 0, 'c': 0, 'ctx': 0, 'loss_target': 0}
SHARED_INPUTS = []
_WEIGHT_DTYPES = {'c_ctx': _jnp.float32, 'norm1_g': _jnp.float32, 'w_mod': _jnp.float32, 'b_mod': _jnp.float32, 'w_in': _jnp.float32, 'gla_fg_up': _jnp.float32, 'gla_fg_b': _jnp.float32, 'gla_onorm_g': _jnp.float32, 'conf_dw': _jnp.float32, 'conf_dw_b': _jnp.float32, 'conf_ln_g': _jnp.float32, 'conf_ln_b': _jnp.float32, 'sc_dw': _jnp.float32, 'mla_q_norm_g': _jnp.float32, 'mla_kv_norm_g': _jnp.float32, 'mla_w_uq': _jnp.float32, 'mla_w_ukv': _jnp.float32, 'w_out': _jnp.float32, 'norm2_g': _jnp.float32, 'ffn_w1': _jnp.float32, 'ffn_w3': _jnp.float32, 'ffn_w2': _jnp.float32, 'final_norm_g': _jnp.float32}
MOMENT_SCALE = {'c_ctx': 5.577174e-03, 'norm1_g': 3.494252e-02, 'w_mod': 3.307586e-02, 'b_mod': 6.026595e-02, 'w_in': 2.324352e-02, 'gla_fg_up': 2.315695e-03, 'gla_fg_b': 5.816973e-03, 'gla_onorm_g': 3.648596e-02, 'conf_dw': 1.690413e-02, 'conf_dw_b': 3.104368e-02, 'conf_ln_g': 1.949558e-02, 'conf_ln_b': 1.846738e-02, 'sc_dw': 3.535161e-02, 'mla_q_norm_g': 3.316894e-03, 'mla_kv_norm_g': 1.568632e-02, 'mla_w_uq': 2.340224e-03, 'mla_w_ukv': 5.924561e-03, 'w_out': 2.187041e-02, 'norm2_g': 2.619803e-02, 'ffn_w1': 1.152930e-02, 'ffn_w3': 1.117568e-02, 'ffn_w2': 1.853295e-02, 'final_norm_g': 1.602105e+01}


def _to_microbatches(a, axis):
    t = _jnp.moveaxis(a, axis, 0)
    t = t.reshape((N_MICROBATCH, t.shape[0] // N_MICROBATCH) + t.shape[1:])
    return _jnp.moveaxis(t, 1, axis + 1)


def setup_inputs(seed: int = 0) -> dict:
    inp = _fwd_setup_inputs(seed)
    key = _jax.random.fold_in(_jax.random.key(seed), 7919)
    shape, _ = _output_shape()
    out = dict(inp)
    out["loss_target"] = _jax.random.normal(_jax.random.fold_in(key, 0), shape, _jnp.float32)
    for i, name in enumerate(TWIN_WEIGHTS):
        w = inp[name].astype(_jnp.float32)
        if MOMENT_SCALE is None:
            s = _jnp.sqrt(_jnp.mean(_jnp.square(w)) + 1e-30)
        else:
            s = MOMENT_SCALE[name]
        km, kv = _jax.random.split(_jax.random.fold_in(key, i + 1))
        out[name] = w
        out["m_" + name] = s * _jax.random.normal(km, w.shape, _jnp.float32)
        out["v_" + name] = (s * s) * _jax.random.uniform(kv, w.shape, _jnp.float32, 0.5, 1.5)
    if N_MICROBATCH > 1:
        for name, axis in PER_EXAMPLE_BATCH_AXIS.items():
            out[name] = _to_microbatches(out[name], axis)
    return {'x': out['x'], 'c': out['c'], 'ctx': out['ctx'], 'c_ctx': out['c_ctx'], 'norm1_g': out['norm1_g'], 'w_mod': out['w_mod'], 'b_mod': out['b_mod'], 'w_in': out['w_in'], 'gla_fg_up': out['gla_fg_up'], 'gla_fg_b': out['gla_fg_b'], 'gla_onorm_g': out['gla_onorm_g'], 'conf_dw': out['conf_dw'], 'conf_dw_b': out['conf_dw_b'], 'conf_ln_g': out['conf_ln_g'], 'conf_ln_b': out['conf_ln_b'], 'sc_dw': out['sc_dw'], 'mla_q_norm_g': out['mla_q_norm_g'], 'mla_kv_norm_g': out['mla_kv_norm_g'], 'mla_w_uq': out['mla_w_uq'], 'mla_w_ukv': out['mla_w_ukv'], 'w_out': out['w_out'], 'norm2_g': out['norm2_g'], 'ffn_w1': out['ffn_w1'], 'ffn_w3': out['ffn_w3'], 'ffn_w2': out['ffn_w2'], 'final_norm_g': out['final_norm_g'], 'loss_target': out['loss_target'], 'm_c_ctx': out['m_c_ctx'], 'm_norm1_g': out['m_norm1_g'], 'm_w_mod': out['m_w_mod'], 'm_b_mod': out['m_b_mod'], 'm_w_in': out['m_w_in'], 'm_gla_fg_up': out['m_gla_fg_up'], 'm_gla_fg_b': out['m_gla_fg_b'], 'm_gla_onorm_g': out['m_gla_onorm_g'], 'm_conf_dw': out['m_conf_dw'], 'm_conf_dw_b': out['m_conf_dw_b'], 'm_conf_ln_g': out['m_conf_ln_g'], 'm_conf_ln_b': out['m_conf_ln_b'], 'm_sc_dw': out['m_sc_dw'], 'm_mla_q_norm_g': out['m_mla_q_norm_g'], 'm_mla_kv_norm_g': out['m_mla_kv_norm_g'], 'm_mla_w_uq': out['m_mla_w_uq'], 'm_mla_w_ukv': out['m_mla_w_ukv'], 'm_w_out': out['m_w_out'], 'm_norm2_g': out['m_norm2_g'], 'm_ffn_w1': out['m_ffn_w1'], 'm_ffn_w3': out['m_ffn_w3'], 'm_ffn_w2': out['m_ffn_w2'], 'm_final_norm_g': out['m_final_norm_g'], 'v_c_ctx': out['v_c_ctx'], 'v_norm1_g': out['v_norm1_g'], 'v_w_mod': out['v_w_mod'], 'v_b_mod': out['v_b_mod'], 'v_w_in': out['v_w_in'], 'v_gla_fg_up': out['v_gla_fg_up'], 'v_gla_fg_b': out['v_gla_fg_b'], 'v_gla_onorm_g': out['v_gla_onorm_g'], 'v_conf_dw': out['v_conf_dw'], 'v_conf_dw_b': out['v_conf_dw_b'], 'v_conf_ln_g': out['v_conf_ln_g'], 'v_conf_ln_b': out['v_conf_ln_b'], 'v_sc_dw': out['v_sc_dw'], 'v_mla_q_norm_g': out['v_mla_q_norm_g'], 'v_mla_kv_norm_g': out['v_mla_kv_norm_g'], 'v_mla_w_uq': out['v_mla_w_uq'], 'v_mla_w_ukv': out['v_mla_w_ukv'], 'v_w_out': out['v_w_out'], 'v_norm2_g': out['v_norm2_g'], 'v_ffn_w1': out['v_ffn_w1'], 'v_ffn_w3': out['v_ffn_w3'], 'v_ffn_w2': out['v_ffn_w2'], 'v_final_norm_g': out['v_final_norm_g']}


def _loss(weights, diff, rest, loss_target):
    with _jax.named_scope("forward"):
        args = {**rest, TWIN_DIFF_INPUT: diff, **{k: w.astype(_WEIGHT_DTYPES[k]) for k, w in weights.items()}}
        y = _forward(args)
    with _jax.named_scope("loss_head"):
        err = _jnp.square(y.astype(_jnp.float32) - loss_target)
        return 0.5 * _jnp.sum(_jnp.mean(err, axis=-1)) if err.ndim else 0.5 * err


def _adamw(w, g, m, v):
    m = ADAM_B1 * m + (1.0 - ADAM_B1) * g
    v = ADAM_B2 * v + (1.0 - ADAM_B2) * _jnp.square(g)
    m_hat = m / (1.0 - ADAM_B1 ** ADAM_STEP)
    v_hat = v / (1.0 - ADAM_B2 ** ADAM_STEP)
    delta = -ADAM_LR * (m_hat / (_jnp.sqrt(v_hat) + ADAM_EPS) + ADAM_WD * w)
    return delta, m, v


def reference(x, c, ctx, c_ctx, norm1_g, w_mod, b_mod, w_in, gla_fg_up, gla_fg_b, gla_onorm_g, conf_dw, conf_dw_b, conf_ln_g, conf_ln_b, sc_dw, mla_q_norm_g, mla_kv_norm_g, mla_w_uq, mla_w_ukv, w_out, norm2_g, ffn_w1, ffn_w3, ffn_w2, final_norm_g, loss_target, m_c_ctx, m_norm1_g, m_w_mod, m_b_mod, m_w_in, m_gla_fg_up, m_gla_fg_b, m_gla_onorm_g, m_conf_dw, m_conf_dw_b, m_conf_ln_g, m_conf_ln_b, m_sc_dw, m_mla_q_norm_g, m_mla_kv_norm_g, m_mla_w_uq, m_mla_w_ukv, m_w_out, m_norm2_g, m_ffn_w1, m_ffn_w3, m_ffn_w2, m_final_norm_g, v_c_ctx, v_norm1_g, v_w_mod, v_b_mod, v_w_in, v_gla_fg_up, v_gla_fg_b, v_gla_onorm_g, v_conf_dw, v_conf_dw_b, v_conf_ln_g, v_conf_ln_b, v_sc_dw, v_mla_q_norm_g, v_mla_kv_norm_g, v_mla_w_uq, v_mla_w_ukv, v_w_out, v_norm2_g, v_ffn_w1, v_ffn_w3, v_ffn_w2, v_final_norm_g):
    given = dict(x=x, c=c, ctx=ctx, c_ctx=c_ctx, norm1_g=norm1_g, w_mod=w_mod, b_mod=b_mod, w_in=w_in, gla_fg_up=gla_fg_up, gla_fg_b=gla_fg_b, gla_onorm_g=gla_onorm_g, conf_dw=conf_dw, conf_dw_b=conf_dw_b, conf_ln_g=conf_ln_g, conf_ln_b=conf_ln_b, sc_dw=sc_dw, mla_q_norm_g=mla_q_norm_g, mla_kv_norm_g=mla_kv_norm_g, mla_w_uq=mla_w_uq, mla_w_ukv=mla_w_ukv, w_out=w_out, norm2_g=norm2_g, ffn_w1=ffn_w1, ffn_w3=ffn_w3, ffn_w2=ffn_w2, final_norm_g=final_norm_g, loss_target=loss_target, m_c_ctx=m_c_ctx, m_norm1_g=m_norm1_g, m_w_mod=m_w_mod, m_b_mod=m_b_mod, m_w_in=m_w_in, m_gla_fg_up=m_gla_fg_up, m_gla_fg_b=m_gla_fg_b, m_gla_onorm_g=m_gla_onorm_g, m_conf_dw=m_conf_dw, m_conf_dw_b=m_conf_dw_b, m_conf_ln_g=m_conf_ln_g, m_conf_ln_b=m_conf_ln_b, m_sc_dw=m_sc_dw, m_mla_q_norm_g=m_mla_q_norm_g, m_mla_kv_norm_g=m_mla_kv_norm_g, m_mla_w_uq=m_mla_w_uq, m_mla_w_ukv=m_mla_w_ukv, m_w_out=m_w_out, m_norm2_g=m_norm2_g, m_ffn_w1=m_ffn_w1, m_ffn_w3=m_ffn_w3, m_ffn_w2=m_ffn_w2, m_final_norm_g=m_final_norm_g, v_c_ctx=v_c_ctx, v_norm1_g=v_norm1_g, v_w_mod=v_w_mod, v_b_mod=v_b_mod, v_w_in=v_w_in, v_gla_fg_up=v_gla_fg_up, v_gla_fg_b=v_gla_fg_b, v_gla_onorm_g=v_gla_onorm_g, v_conf_dw=v_conf_dw, v_conf_dw_b=v_conf_dw_b, v_conf_ln_g=v_conf_ln_g, v_conf_ln_b=v_conf_ln_b, v_sc_dw=v_sc_dw, v_mla_q_norm_g=v_mla_q_norm_g, v_mla_kv_norm_g=v_mla_kv_norm_g, v_mla_w_uq=v_mla_w_uq, v_mla_w_ukv=v_mla_w_ukv, v_w_out=v_w_out, v_norm2_g=v_norm2_g, v_ffn_w1=v_ffn_w1, v_ffn_w3=v_ffn_w3, v_ffn_w2=v_ffn_w2, v_final_norm_g=v_final_norm_g)
    weights = {n: given[n] for n in TWIN_WEIGHTS}
    shared = {n: given[n] for n in SHARED_INPUTS}
    per_example = {n: given[n] for n in ['x', 'c', 'ctx']}
    grad_fn = _jax.value_and_grad(_loss, argnums=(0, 1))

    def one_microbatch(ex, loss_target):
        ex = dict(ex)
        diff = ex.pop(TWIN_DIFF_INPUT)
        return grad_fn(weights, diff, {**shared, **ex}, loss_target)

    if N_MICROBATCH == 1:
        loss, (grad_w, grad_x) = one_microbatch(per_example, given["loss_target"])
    else:
        def body(carry, xs):
            loss_sum, grad_sum = carry
            l_k, (gw_k, gx_k) = one_microbatch(xs[0], xs[1])
            with _jax.named_scope("update"):
                return (loss_sum + l_k, _jax.tree.map(_jnp.add, grad_sum, gw_k)), gx_k

        init = (_jnp.zeros((), _jnp.float32), _jax.tree.map(_jnp.zeros_like, weights))
        (loss, grad_w), grad_x = _jax.lax.scan(body, init, (per_example, given["loss_target"]))
    with _jax.named_scope("update"):
        delta_w, new_m, new_v = {}, {}, {}
        for n in TWIN_WEIGHTS:
            delta_w[n], new_m[n], new_v[n] = _adamw(weights[n], grad_w[n], given["m_" + n], given["v_" + n])
    return (loss, grad_x, *[grad_w[n] for n in TWIN_WEIGHTS], *[delta_w[n] for n in TWIN_WEIGHTS],
            *[new_m[n] for n in TWIN_WEIGHTS], *[new_v[n] for n in TWIN_WEIGHTS])
```

```python
import functools
import math

import jax
import jax.numpy as jnp
from jax import lax
from jax.experimental import pallas as pl
from jax.experimental.pallas import tpu as pltpu

F32 = jnp.float32
BF16 = jnp.bfloat16
MESH = pl.DeviceIdType.MESH
N_DEV = 8

EPS = 1e-6
GRID_W = 64
HEADS = 4
HEAD_W = 128
MIX_W = HEADS * HEAD_W
GATE_RANK = 16
GATE_NORM = 16.0
GLA_CHUNK = 128
CONF_K = 31
SC_K = 3
Q_RANK = 384
KV_RANK = 128
ROPE = 64
ROPE_FREQS = 16
ROPE_BASE = 10000.0
ATT_SCALE = (HEAD_W + ROPE) ** -0.5
CONV_HALO = 16

ADAM_LR = 0.001
ADAM_B1 = 0.9
ADAM_B2 = 0.999
ADAM_EPS = 1e-08
ADAM_WD = 0.01
ADAM_STEP = 10

LANES = 128
VMEM_LIMIT = 56 * 2 ** 20
ROW_BLOCK_BYTES = 10 * 2 ** 20

Z_Q, Z_K, Z_V, Z_G, Z_LR, Z_A, Z_GATE, Z_BG, Z_CG, Z_H, Z_CQ, Z_CKV, Z_KR, Z_END = (
    0, 512, 1024, 1536, 2048, 2176, 2688, 3200, 3712, 4224, 4736, 5120, 5248, 5376)
IN_W = 5216


def _pcall(body, **kw):
    return pl.pallas_call(body, **kw)


def _params(sem=None):
    return pltpu.CompilerParams(dimension_semantics=sem, vmem_limit_bytes=VMEM_LIMIT)


def _pick(dim, cap, mult):
    d = (min(cap, dim) // mult) * mult
    while d >= mult:
        if dim % d == 0:
            return d
        d -= mult
    return dim


def _mm_call(name, a, b, out_shape, grid, a_spec, b_spec, o_spec, dims, k_axis=None, once_axis=None):
    a_blk = tuple(d for d in a_spec.block_shape if d is not None)
    o_blk = tuple(d for d in o_spec.block_shape if d is not None)
    scratch = ([pltpu.VMEM(o_blk, F32)] if k_axis is not None else []) + (
        [pltpu.VMEM(a_blk, BF16)] if once_axis is not None else [])
    nk = grid[k_axis] if k_axis is not None else 1

    def body(a_ref, b_ref, o_ref, *scr):
        if once_axis is not None:
            a_bf = scr[-1]

            @pl.when(pl.program_id(once_axis) == 0)
            def _():
                a_bf[...] = a_ref[...].astype(BF16)

            av = a_bf[...]
        else:
            av = a_ref[...].astype(BF16)
        prod = lax.dot_general(av, b_ref[...].astype(BF16), dims, preferred_element_type=F32)
        if k_axis is None:
            o_ref[...] = prod
        else:
            acc, k = scr[0], pl.program_id(k_axis)

            @pl.when(k == 0)
            def _():
                acc[...] = prod

            @pl.when(k != 0)
            def _():
                acc[...] += prod

            @pl.when(k == nk - 1)
            def _():
                o_ref[...] = acc[...]

    return _pcall(
        body, name=name, grid=grid, in_specs=[a_spec, b_spec], out_specs=o_spec,
        out_shape=jax.ShapeDtypeStruct(out_shape, F32), scratch_shapes=scratch,
        compiler_params=_params(("arbitrary",) * len(grid)),
    )(a, b)


NN = (((1,), (0,)), ((), ()))
NT = (((1,), (1,)), ((), ()))
TN = (((0,), (0,)), ((), ()))


def _matmul(a, b, mode, name):
    if mode == "nn":
        (M, K), (_, N) = a.shape, b.shape
    elif mode == "nt":
        (M, K), (N, _) = a.shape, b.shape
    else:
        (K, M), (_, N) = a.shape, b.shape
    tn = _pick(N, 768, LANES)
    if mode == "tn":
        tm, tk = _pick(M, 512, LANES), _pick(K, 1088, 8)
    else:
        tm, tk = _pick(M, 1088, 8), _pick(K, 1024, LANES)
    if mode == "nn":
        a_spec = pl.BlockSpec((tm, tk), lambda i, j, k: (i, k))
        b_spec = pl.BlockSpec((tk, tn), lambda i, j, k: (k, j))
    elif mode == "nt":
        a_spec = pl.BlockSpec((tm, tk), lambda i, j, k: (i, k))
        b_spec = pl.BlockSpec((tn, tk), lambda i, j, k: (j, k))
    else:
        a_spec = pl.BlockSpec((tk, tm), lambda i, j, k: (k, i))
        b_spec = pl.BlockSpec((tk, tn), lambda i, j, k: (k, j))
    return _mm_call(name, a, b, (M, N), (M // tm, N // tn, K // tk), a_spec, b_spec,
                    pl.BlockSpec((tm, tn), lambda i, j, k: (i, j)), {"nn": NN, "nt": NT, "tn": TN}[mode], k_axis=2)


@jax.custom_vjp
def mm(a, w, carrier):
    return _matmul(a, w, "nn", "mm_fwd")


def _mm_fwd(a, w, carrier):
    return _matmul(a, w, "nn", "mm_fwd"), (a, w)


def _mm_bwd(res, dc):
    a, w = res
    return _matmul(dc, w, "nt", "mm_da"), jnp.zeros_like(w), _matmul(a, dc, "tn", "mm_dw")


mm.defvjp(_mm_fwd, _mm_bwd)


def _make_mm_cols(layer):
    def forward(a, G):
        (M, K), n = a.shape, G.shape[3]
        tm = _pick(M, 1088, 8)
        return _mm_call("mmc_fwd", a, G, (N_DEV, M, n), (M // tm, N_DEV),
                        pl.BlockSpec((tm, K), lambda i, d: (i, 0)),
                        pl.BlockSpec((None, None, K, n), lambda i, d: (d, layer, 0, 0)),
                        pl.BlockSpec((None, tm, n), lambda i, d: (d, i, 0)), NN, once_axis=1)

    def grad_a(do, G):
        (_, M, n), K = do.shape, G.shape[2]
        tm, tn = _pick(M, 1088, 8), _pick(K, 1024, LANES)
        return _mm_call("mmc_da", do, G, (M, K), (M // tm, K // tn, N_DEV),
                        pl.BlockSpec((None, tm, n), lambda i, j, d: (d, i, 0)),
                        pl.BlockSpec((None, None, tn, n), lambda i, j, d: (d, layer, j, 0)),
                        pl.BlockSpec((tm, tn), lambda i, j, d: (i, j)), NT, k_axis=2)

    def grad_w(a, do):
        (M, K), n = a.shape, do.shape[2]
        tm, tk = _pick(K, 1024, LANES), _pick(M, 1088, 8)
        return _mm_call("mmc_dw", a, do, (N_DEV, K, n), (N_DEV, K // tm, M // tk),
                        pl.BlockSpec((tk, tm), lambda d, i, k: (k, i)),
                        pl.BlockSpec((None, tk, n), lambda d, i, k: (d, k, 0)),
                        pl.BlockSpec((None, tm, n), lambda d, i, k: (d, i, 0)), TN, k_axis=2)

    @jax.custom_vjp
    def f(a, G, carrier):
        return forward(a, G)

    def f_fwd(a, G, carrier):
        return forward(a, G), (a, G)

    def f_bwd(res, do):
        a, G = res
        return grad_a(do, G), jnp.zeros_like(G), grad_w(a, do)

    f.defvjp(f_fwd, f_bwd)
    return f


def _make_mm_rows(layer):
    def a_spec(a, tm, r, where):
        if a.ndim == 3:
            return pl.BlockSpec((None, tm, r), lambda *g: (where(*g)[1], where(*g)[0], 0))
        return pl.BlockSpec((tm, r), lambda *g: where(*g))

    def forward(a, G):
        M, (r, N) = a.shape[-2], G.shape[2:]
        tm, tn = _pick(M, 1088, 8), _pick(N, 1024, LANES)
        return _mm_call("mmr_fwd", a, G, (M, N), (M // tm, N // tn, N_DEV),
                        a_spec(a, tm, r, lambda i, j, d: (i, d)),
                        pl.BlockSpec((None, None, r, tn), lambda i, j, d: (d, layer, 0, j)),
                        pl.BlockSpec((tm, tn), lambda i, j, d: (i, j)), NN, k_axis=2)

    def grad_a(dc, G, like):
        (M, N), r = dc.shape, G.shape[2]
        tm = _pick(M, 1088, 8)
        return _mm_call("mmr_da", dc, G, like.shape, (M // tm, N_DEV),
                        pl.BlockSpec((tm, N), lambda i, d: (i, 0)),
                        pl.BlockSpec((None, None, r, N), lambda i, d: (d, layer, 0, 0)),
                        a_spec(like, tm, r, lambda i, d: (i, d)), NT, once_axis=1)

    def grad_w(a, dc):
        (M, N), r = dc.shape, (a.shape[2] if a.ndim == 3 else a.shape[1] // N_DEV)
        tn, tk = _pick(N, 1024, LANES), _pick(M, 1088, 8)
        return _mm_call("mmr_dw", a, dc, (N_DEV, r, N), (N_DEV, N // tn, M // tk),
                        a_spec(a, tk, r, lambda d, j, k: (k, d)),
                        pl.BlockSpec((tk, tn), lambda d, j, k: (k, j)),
                        pl.BlockSpec((None, r, tn), lambda d, j, k: (d, 0, j)), TN, k_axis=2)

    @jax.custom_vjp
    def f(a, G, carrier):
        return forward(a, G)

    def f_fwd(a, G, carrier):
        return forward(a, G), (a, G)

    def f_bwd(res, dc):
        a, G = res
        return grad_a(dc, G, a), jnp.zeros_like(G), grad_w(a, dc)

    f.defvjp(f_fwd, f_bwd)
    return f


@functools.partial(jax.custom_vjp, nondiff_argnums=(1,))
def _split_cols(z, bounds):
    return tuple(z[:, a:b] for a, b in zip(bounds[:-1], bounds[1:]))


def _split_cols_fwd(z, bounds):
    return _split_cols(z, bounds), None


def _split_cols_bwd(bounds, _, cts):
    return (jnp.concatenate(cts, axis=1),)


_split_cols.defvjp(_split_cols_fwd, _split_cols_bwd)


def _rowop(fn, name, n_ctx=0, tile=256, lane_block=None):
    def geometry(rows):
        L, w0 = rows[0].shape
        nj = w0 // lane_block if lane_block else 1
        width = 3 * sum(lane_block or r.shape[1] for r in rows)
        cap = max(8, ROW_BLOCK_BYTES // (4 * width) // 8 * 8)
        tl = _pick(math.gcd(L, n_ctx) if n_ctx else L, min(tile, cap), 8)
        return L, tl, n_ctx // tl, nj

    def block_w(x):
        return lane_block or x.shape[1]

    def row_spec(tl, x):
        if lane_block and x.shape[1] != lane_block:
            return pl.BlockSpec((tl, lane_block), lambda i, j: (i, j))
        return pl.BlockSpec((tl, block_w(x)), lambda i, j: (i, 0))

    def param_spec(p, nct):
        s, r, w = p.shape
        if s == 1:
            return pl.BlockSpec((1, r, w), lambda i, j: (0, 0, 0))
        return pl.BlockSpec((1, r, w), lambda i, j: (jnp.where(i < nct, 0, 1), 0, 0))

    def forward(rows, consts, params):
        L, tl, nct, nj = geometry(rows)
        nr, nc, npar = len(rows), len(consts), len(params)
        outs = jax.eval_shape(
            lambda: fn(*[jnp.zeros((tl, block_w(r)), F32) for r in rows + consts],
                       *[jnp.zeros(p.shape[1:], F32) for p in params]))

        def body(*refs):
            ins = [r[...] for r in refs[:nr + nc]] + [r[0] for r in refs[nr + nc:nr + nc + npar]]
            for o_ref, o in zip(refs[nr + nc + npar:], fn(*ins)):
                o_ref[...] = o

        return _pcall(
            body, name=name + "_fwd", grid=(L // tl, nj),
            in_specs=[row_spec(tl, r) for r in rows + consts] + [param_spec(p, nct) for p in params],
            out_specs=[pl.BlockSpec((tl, o.shape[1]), lambda i, j: (i, j)) for o in outs],
            out_shape=[jax.ShapeDtypeStruct((L, o.shape[1] * nj), F32) for o in outs],
            compiler_params=_params(("parallel", "parallel")),
        )(*rows, *consts, *params)

    def backward(rows, consts, params, cts):
        L, tl, nct, nj = geometry(rows)
        nr, nc, npar, no = len(rows), len(consts), len(params), len(cts)

        def body(*refs):
            i, j = pl.program_id(0), pl.program_id(1)
            rv = [r[...] for r in refs[:nr]]
            cv = [r[...] for r in refs[nr:nr + nc]]
            pv = [r[0] for r in refs[nr + nc:nr + nc + npar]]
            ct = tuple(r[...] for r in refs[nr + nc + npar:nr + nc + npar + no])
            out_refs = refs[nr + nc + npar + no:]
            _, vjp = jax.vjp(lambda *d: tuple(fn(*d[:nr], *cv, *d[nr:])), *rv, *pv)
            grads = vjp(ct)
            for ref, g in zip(out_refs[:nr], grads[:nr]):
                ref[...] = g
            for ref, g, p in zip(out_refs[nr:], grads[nr:], params):
                first_row = (i == 0) if (p.shape[0] == 1 or nct == 0) else ((i == 0) | (i == nct))
                first = jnp.logical_and(first_row, j == 0)

                @pl.when(first)
                def _():
                    ref[0] = g

                @pl.when(jnp.logical_not(first))
                def _():
                    ref[0] += g

        outs = _pcall(
            body, name=name + "_bwd", grid=(L // tl, nj),
            in_specs=[row_spec(tl, r) for r in rows + consts] + [param_spec(p, nct) for p in params]
            + [pl.BlockSpec((tl, c.shape[1] // nj), lambda i, j: (i, j)) for c in cts],
            out_specs=[row_spec(tl, r) for r in rows] + [param_spec(p, nct) for p in params],
            out_shape=[jax.ShapeDtypeStruct(r.shape, F32) for r in rows]
            + [jax.ShapeDtypeStruct(p.shape, F32) for p in params],
            compiler_params=_params(("arbitrary", "arbitrary")),
        )(*rows, *consts, *params, *cts)
        return tuple(outs[:nr]), tuple(outs[nr:])

    @jax.custom_vjp
    def op(rows, consts, params):
        return tuple(forward(rows, consts, params))

    def op_fwd(rows, consts, params):
        return tuple(forward(rows, consts, params)), (rows, consts, params)

    def op_bwd(res, cts):
        rows, consts, params = res
        d_rows, d_params = backward(rows, consts, params, tuple(cts))
        return d_rows, tuple(jnp.zeros_like(c) for c in consts), d_params

    op.defvjp(op_fwd, op_bwd)
    return op


def _sigmoid(x):
    return 1.0 / (1.0 + jnp.exp(-x))


def _silu(x):
    return x * _sigmoid(x)


def _log_sigmoid(x):
    return jnp.minimum(x, 0.0) - jnp.log(1.0 + jnp.exp(-jnp.abs(x)))


def _rms(x, g):
    return x * lax.rsqrt(jnp.mean(x * x, axis=-1, keepdims=True) + EPS) * g


def _f_norm_mod(x, g, sc, sh):
    return (_rms(x, g) * (1.0 + sc) + sh,)


def _f_resid_norm_mod(x, o, gate, g, sc, sh):
    x1 = x + gate * o
    return x1, _rms(x1, g) * (1.0 + sc) + sh


def _f_resid(x, o, gate):
    return (x + gate * o,)


def _f_swiglu(a1, a3):
    return (_silu(a1) * a3,)


def _f_gla_gate(lr, up_f, up_b, b_f, b_b):
    dot = functools.partial(jnp.dot, preferred_element_type=F32)
    return (_log_sigmoid(dot(lr, up_f) + b_f) / GATE_NORM, _log_sigmoid(dot(lr, up_b) + b_b) / GATE_NORM)


def _f_gla_finish(o_f, o_b, gate, g):
    return (_rms(o_f + o_b, g) * _silu(gate),)


def _f_glu(a, gate):
    return (a * _sigmoid(gate),)


def _f_ln_silu(u, dw_b, g, b):
    u = u + dw_b
    xc = u - jnp.mean(u, axis=-1, keepdims=True)
    y = xc * lax.rsqrt(jnp.mean(xc * xc, axis=-1, keepdims=True) + EPS)
    return (_silu(y * g + b),)


def _f_mul(a, b):
    return (a * b,)


def _f_rms(x, g):
    return (_rms(x, g),)


def _f_rope(t, cos, sin):
    w = t.shape[1]
    r = lax.broadcasted_iota(jnp.int32, (w, w), 0)
    c = lax.broadcasted_iota(jnp.int32, (w, w), 1)
    perm = (jnp.bitwise_xor(r, ROPE_FREQS) == c).astype(F32)
    partner = jnp.dot(t, perm, precision=lax.Precision.HIGHEST, preferred_element_type=F32)
    return (t * cos + partner * sin,)


def _f_silu(x):
    return (_silu(x),)


def _f_add_bias(x, b):
    return (x + b,)


def _f_mul_silu_grad(d, x):
    _, vjp = jax.vjp(_silu, x)
    return (vjp(d)[0],)


def _conv_geometry(u, n_ctx):
    L, C = u.shape
    tl = _pick(math.gcd(L, n_ctx), 256, 8)
    return L, C, tl, n_ctx // tl, L // tl


def _conv_specs(tl, nt):
    prev = pl.BlockSpec((tl, LANES), lambda c, i: (jnp.maximum(i - 1, 0), c))
    cur = pl.BlockSpec((tl, LANES), lambda c, i: (i, c))
    nxt = pl.BlockSpec((tl, LANES), lambda c, i: (jnp.minimum(i + 1, nt - 1), c))
    return prev, cur, nxt


def _conv_window(prev_ref, cur_ref, next_ref, tl, nct, nt):
    i = pl.program_id(1)
    has_prev = jnp.logical_and(i != 0, i != nct)
    has_next = jnp.logical_and(i != nct - 1, i != nt - 1)
    prev = jnp.where(has_prev, prev_ref[tl - CONV_HALO:tl, :], 0.0)
    nxt = jnp.where(has_next, next_ref[0:CONV_HALO, :], 0.0)
    return jnp.concatenate([prev, cur_ref[...], nxt], axis=0)


def _shifted(window, off, tl):
    n = window.shape[0]
    if off == 0:
        return window[0:tl]
    return pltpu.roll(window, n - off, 0)[0:tl]


def _conv_apply(u, w, n_ctx, flip, name):
    L, C, tl, nct, nt = _conv_geometry(u, n_ctx)
    K = w.shape[0]
    pad = (K - 1) // 2
    prev, cur, nxt = _conv_specs(tl, nt)

    def body(p_ref, c_ref, n_ref, w_ref, o_ref):
        win = _conv_window(p_ref, c_ref, n_ref, tl, nct, nt)
        acc = jnp.zeros((tl, LANES), F32)
        for k in range(K):
            kk = K - 1 - k if flip else k
            acc = acc + _shifted(win, CONV_HALO - pad + k, tl) * w_ref[kk:kk + 1, :]
        o_ref[...] = acc

    return _pcall(
        body, name=name, grid=(C // LANES, nt),
        in_specs=[prev, cur, nxt, pl.BlockSpec((K, LANES), lambda c, i: (0, c))],
        out_specs=cur, out_shape=jax.ShapeDtypeStruct((L, C), F32),
        compiler_params=_params(("parallel", "parallel")),
    )(u, u, u, w)


def _conv_dw(u, dy, K, n_ctx, name):
    L, C, tl, nct, nt = _conv_geometry(u, n_ctx)
    pad = (K - 1) // 2
    prev, cur, nxt = _conv_specs(tl, nt)

    def body(p_ref, c_ref, n_ref, dy_ref, dw_ref):
        i = pl.program_id(1)

        @pl.when(i == 0)
        def _():
            dw_ref[...] = jnp.zeros_like(dw_ref)

        win = _conv_window(p_ref, c_ref, n_ref, tl, nct, nt)
        dy_t = dy_ref[...]
        for k in range(K):
            dw_ref[k:k + 1, :] += jnp.sum(_shifted(win, CONV_HALO - pad + k, tl) * dy_t, axis=0, keepdims=True)

    return _pcall(
        body, name=name, grid=(C // LANES, nt),
        in_specs=[prev, cur, nxt, cur],
        out_specs=pl.BlockSpec((K, LANES), lambda c, i: (0, c)),
        out_shape=jax.ShapeDtypeStruct((K, C), F32),
        compiler_params=_params(("parallel", "arbitrary")),
    )(u, u, u, dy)


def _make_conv(n_ctx):
    @jax.custom_vjp
    def conv(u, w):
        return _conv_apply(u, w, n_ctx, False, "conv_fwd")

    def conv_fwd(u, w):
        return _conv_apply(u, w, n_ctx, False, "conv_fwd"), (u, w)

    def conv_bwd(res, dy):
        u, w = res
        return _conv_apply(dy, w, n_ctx, True, "conv_du"), _conv_dw(u, dy, w.shape[0], n_ctx, "conv_dw")

    conv.defvjp(conv_fwd, conv_bwd)
    return conv


def _gla_chunk(q, k, v, g, st, reverse):
    C = q.shape[0]
    r = lax.broadcasted_iota(jnp.int32, (C, C), 0)
    c = lax.broadcasted_iota(jnp.int32, (C, C), 1)
    seen = (r <= c) if reverse else (r >= c)
    dot = functools.partial(lax.dot_general, preferred_element_type=F32)
    bcum = dot(seen.astype(F32), g, (((1,), (0,)), ((), ())), precision=lax.Precision.HIGHEST)
    total = jnp.sum(g, axis=0, keepdims=True)
    a = q * (HEAD_W ** -0.5) * jnp.exp(bcum)
    scores = jnp.where(seen, dot(a, k * jnp.exp(-bcum), (((1,), (1,)), ((), ()))), 0.0)
    o = dot(a, st, (((1,), (1,)), ((), ()))) + dot(scores, v, (((1,), (0,)), ((), ())))
    st_new = st * jnp.exp(total) + dot(v, k * jnp.exp(total - bcum), (((0,), (0,)), ((), ())))
    return o, st_new


def _gla_order(t, nc, ncc, reverse):
    if not reverse:
        return t
    return jnp.where(t < ncc, ncc - 1 - t, ncc + nc - 1 - t)


def _gla_fwd_call(q, k, v, g, n_ctx, reverse):
    L = q.shape[0]
    C = GLA_CHUNK
    nc, ncc = L // C, n_ctx // C
    spec = pl.BlockSpec((C, HEAD_W), lambda h, t: (_gla_order(t, nc, ncc, reverse), h))

    def body(q_ref, k_ref, v_ref, g_ref, o_ref, s_ref, st):
        @pl.when(pl.program_id(1) == 0)
        def _():
            st[...] = jnp.zeros_like(st)

        s_ref[0, 0] = st[...]
        o, st_new = _gla_chunk(q_ref[...], k_ref[...], v_ref[...], g_ref[...], st[...], reverse)
        o_ref[...] = o
        st[...] = st_new

    return _pcall(
        body, name="gla_fwd", grid=(HEADS, nc), in_specs=[spec] * 4,
        out_specs=[spec, pl.BlockSpec((1, 1, HEAD_W, HEAD_W), lambda h, t: (h, t, 0, 0))],
        out_shape=[jax.ShapeDtypeStruct((L, MIX_W), F32), jax.ShapeDtypeStruct((HEADS, nc, HEAD_W, HEAD_W), F32)],
        scratch_shapes=[pltpu.VMEM((HEAD_W, HEAD_W), F32)],
        compiler_params=_params(("parallel", "arbitrary")),
    )(q, k, v, g)


def _gla_bwd_call(q, k, v, g, states, do, n_ctx, reverse):
    L = q.shape[0]
    C = GLA_CHUNK
    nc, ncc = L // C, n_ctx // C
    spec = pl.BlockSpec((C, HEAD_W), lambda h, t: (_gla_order(nc - 1 - t, nc, ncc, reverse), h))

    def body(q_ref, k_ref, v_ref, g_ref, s_ref, do_ref, dq_ref, dk_ref, dv_ref, dg_ref, dst):
        @pl.when(pl.program_id(1) == 0)
        def _():
            dst[...] = jnp.zeros_like(dst)

        _, vjp = jax.vjp(functools.partial(_gla_chunk, reverse=reverse),
                         q_ref[...], k_ref[...], v_ref[...], g_ref[...], s_ref[0, 0])
        dq, dk, dv, dg, dst_prev = vjp((do_ref[...], dst[...]))
        dq_ref[...] = dq
        dk_ref[...] = dk
        dv_ref[...] = dv
        dg_ref[...] = dg
        dst[...] = dst_prev

    return _pcall(
        body, name="gla_bwd", grid=(HEADS, nc),
        in_specs=[spec] * 4 + [pl.BlockSpec((1, 1, HEAD_W, HEAD_W), lambda h, t: (h, nc - 1 - t, 0, 0)), spec],
        out_specs=[spec] * 4, out_shape=[jax.ShapeDtypeStruct((L, MIX_W), F32)] * 4,
        scratch_shapes=[pltpu.VMEM((HEAD_W, HEAD_W), F32)],
        compiler_params=_params(("parallel", "arbitrary")),
    )(q, k, v, g, states, do)


def _make_gla(n_ctx, reverse):
    @jax.custom_vjp
    def gla(q, k, v, g):
        return _gla_fwd_call(q, k, v, g, n_ctx, reverse)[0]

    def gla_fwd(q, k, v, g):
        o, states = _gla_fwd_call(q, k, v, g, n_ctx, reverse)
        return o, (q, k, v, g, states)

    def gla_bwd(res, do):
        q, k, v, g, states = res
        return tuple(_gla_bwd_call(q, k, v, g, states, do, n_ctx, reverse))

    gla.defvjp(gla_fwd, gla_bwd)
    return gla


def _att_probs(qn, qr, kn, kr, i, nct, n_ctx):
    nt_dims = (((1,), (1,)), ((), ()))
    s = lax.dot_general(qn, kn, nt_dims, preferred_element_type=F32)
    s = (s + lax.dot_general(qr, kr, nt_dims, preferred_element_type=F32)) * ATT_SCALE
    col = lax.broadcasted_iota(jnp.int32, s.shape, 1)
    s = jnp.where(col < jnp.where(i < nct, n_ctx, s.shape[1]), s, -1e30)
    p = jnp.exp(s - jnp.max(s, axis=-1, keepdims=True))
    return p / jnp.sum(p, axis=-1, keepdims=True)


def _att_geometry(qn, n_ctx):
    L = qn.shape[0]
    tq = _pick(math.gcd(L, n_ctx), 256, 8)
    q_spec = pl.BlockSpec((tq, HEAD_W), lambda h, i: (i, h))
    k_spec = pl.BlockSpec((L, HEAD_W), lambda h, i: (0, h))
    kr_spec = pl.BlockSpec((L, HEAD_W), lambda h, i: (0, 0))
    return L, tq, n_ctx // tq, q_spec, k_spec, kr_spec


def _att_fwd_call(qn, qr, kn, kr, v, n_ctx):
    L, tq, nct, q_spec, k_spec, kr_spec = _att_geometry(qn, n_ctx)

    def body(qn_ref, qr_ref, kn_ref, kr_ref, v_ref, o_ref):
        p = _att_probs(qn_ref[...].astype(BF16), qr_ref[...].astype(BF16), kn_ref[...].astype(BF16),
                       kr_ref[...].astype(BF16), pl.program_id(1), nct, n_ctx)
        o_ref[...] = jnp.dot(p.astype(BF16), v_ref[...].astype(BF16), preferred_element_type=F32)

    return _pcall(
        body, name="att_fwd", grid=(HEADS, L // tq),
        in_specs=[q_spec, q_spec, k_spec, kr_spec, k_spec], out_specs=q_spec,
        out_shape=jax.ShapeDtypeStruct((L, MIX_W), F32),
        compiler_params=_params(("parallel", "parallel")),
    )(qn, qr, kn, kr, v)


def _att_bwd_call(qn, qr, kn, kr, v, do, n_ctx):
    L, tq, nct, q_spec, k_spec, kr_spec = _att_geometry(qn, n_ctx)
    tn_dims = (((0,), (0,)), ((), ()))

    def body(qn_ref, qr_ref, kn_ref, kr_ref, v_ref, do_ref, dqn_ref, dqr_ref, dkn_ref, dkr_ref, dv_ref):
        h, i = pl.program_id(0), pl.program_id(1)
        qn, qr = qn_ref[...].astype(BF16), qr_ref[...].astype(BF16)
        kn, kr, vv = kn_ref[...].astype(BF16), kr_ref[...].astype(BF16), v_ref[...].astype(BF16)
        do = do_ref[...].astype(BF16)
        p = _att_probs(qn, qr, kn, kr, i, nct, n_ctx)
        dp = lax.dot_general(do, vv, (((1,), (1,)), ((), ())), preferred_element_type=F32)
        ds = (p * (dp - jnp.sum(p * dp, axis=-1, keepdims=True)) * ATT_SCALE).astype(BF16)
        dqn_ref[...] = jnp.dot(ds, kn, preferred_element_type=F32)
        dqr_ref[...] = jnp.dot(ds, kr, preferred_element_type=F32)

        @pl.when(i == 0)
        def _():
            dkn_ref[...] = jnp.zeros_like(dkn_ref)
            dv_ref[...] = jnp.zeros_like(dv_ref)

        @pl.when(jnp.logical_and(i == 0, h == 0))
        def _():
            dkr_ref[...] = jnp.zeros_like(dkr_ref)

        dkn_ref[...] += lax.dot_general(ds, qn, tn_dims, preferred_element_type=F32)
        dkr_ref[...] += lax.dot_general(ds, qr, tn_dims, preferred_element_type=F32)
        dv_ref[...] += lax.dot_general(p.astype(BF16), do, tn_dims, preferred_element_type=F32)

    return _pcall(
        body, name="att_bwd", grid=(HEADS, L // tq),
        in_specs=[q_spec, q_spec, k_spec, kr_spec, k_spec, q_spec],
        out_specs=[q_spec, q_spec, k_spec, kr_spec, k_spec],
        out_shape=[jax.ShapeDtypeStruct((L, MIX_W), F32)] * 3 + [jax.ShapeDtypeStruct((L, HEAD_W), F32),
                                                                 jax.ShapeDtypeStruct((L, MIX_W), F32)],
        compiler_params=_params(("arbitrary", "arbitrary")),
    )(qn, qr, kn, kr, v, do)


def _make_attention(n_ctx):
    @jax.custom_vjp
    def att(qn, qr, kn, kr, v):
        return _att_fwd_call(qn, qr, kn, kr, v, n_ctx)

    def att_fwd(qn, qr, kn, kr, v):
        return _att_fwd_call(qn, qr, kn, kr, v, n_ctx), (qn, qr, kn, kr, v)

    def att_bwd(res, do):
        return tuple(_att_bwd_call(*res, do, n_ctx))

    att.defvjp(att_fwd, att_bwd)
    return att


def _loss_call(x, g, target):
    L, D = x.shape
    tl = _pick(L, 256, 8)

    def f(xv, gv, tv):
        err = _rms(xv, gv) - tv
        return 0.5 * jnp.sum(err * err, axis=0, keepdims=True) / D

    def body(x_ref, g_ref, t_ref, loss_ref, dx_ref, dg_ref):
        i = pl.program_id(0)
        loss, vjp = jax.vjp(lambda xv, gv: f(xv, gv, t_ref[...]), x_ref[...], g_ref[...])
        dx, dg = vjp(jnp.ones_like(loss))
        dx_ref[...] = dx

        @pl.when(i == 0)
        def _():
            loss_ref[...] = loss
            dg_ref[...] = dg

        @pl.when(i != 0)
        def _():
            loss_ref[...] += loss
            dg_ref[...] += dg

    row = pl.BlockSpec((tl, D), lambda i: (i, 0))
    one = pl.BlockSpec((1, D), lambda i: (0, 0))
    return _pcall(
        body, name="loss", grid=(L // tl,), in_specs=[row, one, row], out_specs=[one, row, one],
        out_shape=[jax.ShapeDtypeStruct((1, D), F32), jax.ShapeDtypeStruct((L, D), F32),
                   jax.ShapeDtypeStruct((1, D), F32)],
        compiler_params=_params(("arbitrary",)),
    )(x, g, target)


def _sum_leading(x, name):
    n, R, W = x.shape
    tr = _pick(R, 512, 8)

    def body(x_ref, o_ref):
        acc = x_ref[0]
        for d in range(1, n):
            acc = acc + x_ref[d]
        o_ref[...] = acc

    return _pcall(
        body, name=name, grid=(R // tr,), in_specs=[pl.BlockSpec((n, tr, W), lambda i: (0, i, 0))],
        out_specs=pl.BlockSpec((tr, W), lambda i: (i, 0)), out_shape=jax.ShapeDtypeStruct((R, W), F32),
        compiler_params=_params(("parallel",)),
    )(x)


def _adamw(w, g, m, v):
    shape = w.shape
    W = shape[-1]
    as2d = lambda t: t.reshape(-1, W)
    R = as2d(w).shape[0]
    tr = _pick(R, max(8, (2 ** 17 // W) // 8 * 8), 8)

    def body(w_ref, g_ref, m_ref, v_ref, d_ref, nm_ref, nv_ref):
        gv = g_ref[...]
        m_new = ADAM_B1 * m_ref[...] + (1.0 - ADAM_B1) * gv
        v_new = ADAM_B2 * v_ref[...] + (1.0 - ADAM_B2) * (gv * gv)
        m_hat = m_new / (1.0 - ADAM_B1 ** ADAM_STEP)
        v_hat = v_new / (1.0 - ADAM_B2 ** ADAM_STEP)
        d_ref[...] = -ADAM_LR * (m_hat / (jnp.sqrt(v_hat) + ADAM_EPS) + ADAM_WD * w_ref[...])
        nm_ref[...] = m_new
        nv_ref[...] = v_new

    spec = pl.BlockSpec((tr, W), lambda i: (i, 0))
    outs = _pcall(
        body, name="adamw", grid=(R // tr,), in_specs=[spec] * 4, out_specs=[spec] * 3,
        out_shape=[jax.ShapeDtypeStruct((R, W), F32)] * 3, compiler_params=_params(("parallel",)),
    )(as2d(w), as2d(g), as2d(m), as2d(v))
    return tuple(o.reshape(shape) for o in outs)


HBM = pl.BlockSpec(memory_space=pltpu.HBM)


def _place():
    x, y, c = lax.axis_index("x"), lax.axis_index("y"), lax.axis_index("c")
    return x, y, c, [(1 - x, y), (x, 1 - y), (1 - x, 1 - y)]


def _all_gather(blocks, name):
    n = len(blocks)

    def body(*refs):
        x_refs, out_refs = refs[:n], refs[n:2 * n]
        send_sems, recv_sems, local_sems = refs[2 * n:]
        x, y, c, chips = _place()
        me, sibling = (x, y, c), (x, y, 1 - c)

        def slot(a, px, py, pc):
            return out_refs[a].at[4 * px + 2 * py + pc]

        def copy(a, k, blk, to, src=None):
            return pltpu.make_async_remote_copy(
                src_ref=slot(a, *blk) if src is None else src, dst_ref=slot(a, *blk),
                send_sem=send_sems.at[7 * a + k], recv_sem=recv_sems.at[7 * a + k], device_id=to, device_id_type=MESH)

        mine = [pltpu.make_async_copy(x_refs[a], slot(a, *me), local_sems.at[a]) for a in range(n)]
        for cp in mine:
            cp.start()
        first = []
        for a in range(n):
            first.append(copy(a, 0, me, sibling, src=x_refs[a]))
            first += [copy(a, 1 + j, me, (*chip, c), src=x_refs[a]) for j, chip in enumerate(chips)]
        for cp in first:
            cp.start()
        passed = []
        for j, chip in enumerate(chips):
            for a in range(n):
                copy(a, 1 + j, (*chip, c), me).wait_recv()
                passed.append(copy(a, 4 + j, (*chip, c), sibling))
                passed[-1].start()
        for a in range(n):
            copy(a, 0, sibling, me).wait_recv()
        for j, chip in enumerate(chips):
            for a in range(n):
                copy(a, 4 + j, (*chip, 1 - c), me).wait_recv()
        for cp in first + passed:
            cp.wait_send()
        for cp in mine:
            cp.wait()

    return _pcall(
        body, name=name, out_shape=[jax.ShapeDtypeStruct((N_DEV,) + b.shape, b.dtype) for b in blocks],
        in_specs=[HBM] * n, out_specs=[HBM] * n,
        scratch_shapes=[pltpu.SemaphoreType.DMA((7 * n,)), pltpu.SemaphoreType.DMA((7 * n,)),
                        pltpu.SemaphoreType.DMA((n,))],
    )(*blocks)


def _send_to_sibling(gs, name):
    n = len(gs)

    def body(*refs):
        g_refs, out_refs, (send_sems, recv_sems) = refs[:n], refs[n:2 * n], refs[2 * n:]
        x, y, c, _ = _place()
        copies = [pltpu.make_async_remote_copy(
            src_ref=g_refs[a].at[2 * q + 1 - c], dst_ref=out_refs[a].at[q], send_sem=send_sems.at[4 * a + q],
            recv_sem=recv_sems.at[4 * a + q], device_id=(x, y, 1 - c), device_id_type=MESH)
            for a in range(n) for q in range(4)]
        for cp in copies:
            cp.start()
        for cp in copies:
            cp.wait()

    return _pcall(
        body, name=name, out_shape=[jax.ShapeDtypeStruct((4,) + g.shape[1:], g.dtype) for g in gs],
        in_specs=[HBM] * n, out_specs=[HBM] * n,
        scratch_shapes=[pltpu.SemaphoreType.DMA((4 * n,)), pltpu.SemaphoreType.DMA((4 * n,))],
    )(*gs)


def _send_to_chips(ps, name):
    n = len(ps)

    def body(*refs):
        p_refs, out_refs, (send_sems, recv_sems) = refs[:n], refs[n:2 * n], refs[2 * n:]
        x, y, c, chips = _place()
        copies = [pltpu.make_async_remote_copy(
            src_ref=p_refs[a].at[2 * cx + cy], dst_ref=out_refs[a].at[j], send_sem=send_sems.at[3 * a + j],
            recv_sem=recv_sems.at[3 * a + j], device_id=(cx, cy, c), device_id_type=MESH)
            for a in range(n) for j, (cx, cy) in enumerate(chips)]
        for cp in copies:
            cp.start()
        for cp in copies:
            cp.wait()

    return _pcall(
        body, name=name, out_shape=[jax.ShapeDtypeStruct((3,) + p.shape[1:], p.dtype) for p in ps],
        in_specs=[HBM] * n, out_specs=[HBM] * n,
        scratch_shapes=[pltpu.SemaphoreType.DMA((3 * n,)), pltpu.SemaphoreType.DMA((3 * n,))],
    )(*ps)


def _add_rows(R, W):
    return _pick(R, max(8, 2 ** 19 // W // 8 * 8), 8)


def _add_sibling(g, recv, core):
    _, R, W = g.shape
    tr = _add_rows(R, W)

    def body(core_ref, g_ref, r_ref, o_ref):
        o_ref[...] = g_ref[...] + r_ref[...]

    return _pcall(
        body, name="rs_add_sibling",
        grid_spec=pltpu.PrefetchScalarGridSpec(
            num_scalar_prefetch=1, grid=(4, R // tr),
            in_specs=[pl.BlockSpec((None, tr, W), lambda q, i, core_ref: (2 * q + core_ref[0], i, 0)),
                      pl.BlockSpec((None, tr, W), lambda q, i, core_ref: (q, i, 0))],
            out_specs=pl.BlockSpec((None, tr, W), lambda q, i, core_ref: (q, i, 0))),
        out_shape=jax.ShapeDtypeStruct((4, R, W), F32), compiler_params=_params(("parallel", "parallel")),
    )(core, g, recv)


def _add_chips(p, recv, chip):
    _, R, W = p.shape
    tr = _add_rows(R, W)

    def body(chip_ref, p_ref, r_ref, o_ref):
        o_ref[...] = ((p_ref[...] + r_ref[0]) + r_ref[1]) + r_ref[2]

    return _pcall(
        body, name="rs_add_chips",
        grid_spec=pltpu.PrefetchScalarGridSpec(
            num_scalar_prefetch=1, grid=(R // tr,),
            in_specs=[pl.BlockSpec((None, tr, W), lambda i, chip_ref: (chip_ref[0], i, 0)),
                      pl.BlockSpec((3, tr, W), lambda i, chip_ref: (0, i, 0))],
            out_specs=pl.BlockSpec((tr, W), lambda i, chip_ref: (i, 0))),
        out_shape=jax.ShapeDtypeStruct((R, W), F32), compiler_params=_params(("parallel",)),
    )(chip, p, recv)


def _reduce_scatter(gs):
    x, y, c = lax.axis_index("x"), lax.axis_index("y"), lax.axis_index("c")
    core = jnp.reshape(c, (1,)).astype(jnp.int32)
    chip = jnp.reshape(2 * x + y, (1,)).astype(jnp.int32)
    from_sibling = _send_to_sibling(gs, "rs_sibling")
    chip_sums = [_add_sibling(g, r, core) for g, r in zip(gs, from_sibling)]
    from_chips = _send_to_chips(chip_sums, "rs_chips")
    return [_add_chips(p, r, chip) for p, r in zip(chip_sums, from_chips)]


def _rope_tables(seq, n_ctx):
    rows = seq // GRID_W
    row = jnp.repeat(jnp.arange(rows, dtype=F32), GRID_W)
    col = jnp.tile(jnp.arange(GRID_W, dtype=F32), rows)
    inv = ROPE_BASE ** (-jnp.arange(ROPE_FREQS, dtype=F32) * 2.0 / (ROPE // 2))
    ang_r, ang_c = row[:, None] * inv, col[:, None] * inv
    one, zero = jnp.ones((seq, ROPE), F32), jnp.zeros((seq, ROPE), F32)
    cos = jnp.concatenate([jnp.cos(ang_r), jnp.cos(ang_r), jnp.cos(ang_c), jnp.cos(ang_c), one], axis=1)
    sin = jnp.concatenate([-jnp.sin(ang_r), jnp.sin(ang_r), -jnp.sin(ang_c), jnp.sin(ang_c), zero], axis=1)
    cos = jnp.concatenate([jnp.ones((n_ctx, LANES), F32), cos], axis=0)
    sin = jnp.concatenate([jnp.zeros((n_ctx, LANES), F32), sin], axis=0)
    return cos, sin


def _shared(v):
    return v.reshape((1, 1, -1)) if v.ndim == 1 else v.reshape((1,) + v.shape)


Z_BOUNDS = (Z_Q, Z_K, Z_V, Z_G, Z_LR, Z_A, Z_GATE, Z_BG, Z_CG, Z_H, Z_CQ, Z_CKV, Z_KR, Z_END)
PLAIN = ("w_in", "w_qn", "w_qr", "w_kn", "w_v")
COLS = ("ffn_w1", "ffn_w3")
ROWS = ("w_out", "ffn_w2")


def _layer(t, p, car, plain, G, layer, mod_l, mod_c, n_ctx, tables):
    L, D = t.shape
    seg = lambda i: jnp.stack([mod_c[i * D:(i + 1) * D], mod_l[i * D:(i + 1) * D]]).reshape(2, 1, D)
    sh1, sc1, g1, sh2, sc2, g2 = (seg(i) for i in range(6))
    dense = lambda a, n: mm(a, plain[n], car[n])

    (h,) = _rowop(_f_norm_mod, "norm_mod", n_ctx)((t,), (), (_shared(p["norm1_g"]), sc1, sh1))
    q, k, v, gate, lr, conf_a, conf_gate, sc_b, sc_c, sc_h, cq, ckv, kr = _split_cols(dense(h, "w_in"), Z_BOUNDS)

    up = p["gla_fg_up"]
    up_f = jnp.pad(up[0], ((0, LANES - GATE_RANK), (0, 0)))
    up_b = jnp.pad(up[1], ((GATE_RANK, LANES - 2 * GATE_RANK), (0, 0)))
    logd_f, logd_b = _rowop(_f_gla_gate, "gla_gate")(
        (lr,), (), (_shared(up_f), _shared(up_b), _shared(p["gla_fg_b"][0]), _shared(p["gla_fg_b"][1])))
    o_f = _make_gla(n_ctx, False)(q, k, v, logd_f)
    o_b = _make_gla(n_ctx, True)(q, k, v, logd_b)
    (gla,) = _rowop(_f_gla_finish, "gla_finish", lane_block=HEAD_W)((o_f, o_b, gate), (), (_shared(p["gla_onorm_g"]),))

    conv = _make_conv(n_ctx)
    (u,) = _rowop(_f_glu, "glu")((conf_a, conf_gate), (), ())
    (conf,) = _rowop(_f_ln_silu, "ln_silu")(
        (conv(u, p["conf_dw"]),), (), (_shared(p["conf_dw_b"]), _shared(p["conf_ln_g"]), _shared(p["conf_ln_b"])))

    (ch,) = _rowop(_f_mul, "mul")((sc_c, sc_h), (), ())
    (sconv,) = _rowop(_f_mul, "mul")((sc_b, conv(ch, p["sc_dw"])), (), ())

    (cq,) = _rowop(_f_rms, "rms")((cq,), (), (_shared(p["mla_q_norm_g"]),))
    (ckv,) = _rowop(_f_rms, "rms")((ckv,), (), (_shared(p["mla_kv_norm_g"]),))
    rope = _rowop(_f_rope, "rope", lane_block=LANES)
    (qr,) = rope((dense(cq, "w_qr"),), tables, ())
    (kr,) = rope((kr,), tables, ())
    mla = _make_attention(n_ctx)(dense(cq, "w_qn"), qr, dense(ckv, "w_kn"), kr, dense(ckv, "w_v"))

    o = _make_mm_rows(layer)(jnp.concatenate([gla, conf, sconv, mla], axis=1), G["w_out"], car["w_out"])
    t1, h2 = _rowop(_f_resid_norm_mod, "resid_norm_mod", n_ctx)((t, o), (), (g1, _shared(p["norm2_g"]), sc2, sh2))
    a1 = _make_mm_cols(layer)(h2, G["ffn_w1"], car["ffn_w1"])
    a3 = _make_mm_cols(layer)(h2, G["ffn_w3"], car["ffn_w3"])
    n_ff = a1.shape[2]
    (act,) = _rowop(_f_swiglu, "swiglu")((a1.reshape(N_DEV * L, n_ff), a3.reshape(N_DEV * L, n_ff)), (), ())
    f = _make_mm_rows(layer)(act.reshape(N_DEV, L, n_ff), G["ffn_w2"], car["ffn_w2"])
    (t2,) = _rowop(_f_resid, "resid", n_ctx)((t1, f), (), (g2,))
    return t2


def _trunk(t, smalls, cars, plains, G, mods_l, mods_c, n_ctx, seq):
    tables = _rope_tables(seq, n_ctx)
    for layer, (p, car, plain, mod_l, mod_c) in enumerate(zip(smalls, cars, plains, mods_l, mods_c)):
        t = _layer(t, p, car, plain, G, layer, mod_l, mod_c, n_ctx, tables)
    return t


def _plain_weights(G, layer, D):
    full = lambda n: jnp.concatenate([G[n][d, layer] for d in range(N_DEV)], axis=1)
    w_in = full("w_in")
    w_in = jnp.concatenate([w_in[:, :Z_LR + 2 * GATE_RANK], jnp.zeros((D, Z_A - Z_LR - 2 * GATE_RANK), BF16),
                            w_in[:, Z_LR + 2 * GATE_RANK:], jnp.zeros((D, Z_END - Z_KR - ROPE), BF16)], axis=1)
    w_uq = full("mla_w_uq").reshape(Q_RANK, HEADS, HEAD_W + ROPE)
    w_ukv = full("mla_w_ukv").reshape(KV_RANK, HEADS, 2 * HEAD_W)
    return {"w_in": w_in,
            "w_qn": w_uq[:, :, :HEAD_W].reshape(Q_RANK, MIX_W),
            "w_qr": jnp.pad(w_uq[:, :, HEAD_W:], ((0, 0), (0, 0), (0, LANES - ROPE))).reshape(Q_RANK, HEADS * LANES),
            "w_kn": w_ukv[:, :, :HEAD_W].reshape(KV_RANK, MIX_W),
            "w_v": w_ukv[:, :, HEAD_W:].reshape(KV_RANK, MIX_W)}


def _col_slabs(full):
    n = full.shape[1] // N_DEV
    return jnp.stack([full[:, d * n:(d + 1) * n] for d in range(N_DEV)])


def _shard_grads(d_car):
    d_in = d_car["w_in"]
    d_in = jnp.concatenate([d_in[:, :Z_LR + 2 * GATE_RANK], d_in[:, Z_A:Z_KR + ROPE]], axis=1)
    by_head = lambda g: g.reshape(g.shape[0], HEADS, -1)
    d_uq = jnp.concatenate([by_head(d_car["w_qn"]), by_head(d_car["w_qr"])[:, :, :ROPE]], axis=2)
    d_ukv = jnp.concatenate([by_head(d_car["w_kn"]), by_head(d_car["w_v"])], axis=2)
    out = {"w_in": _col_slabs(d_in), "mla_w_uq": _col_slabs(d_uq.reshape(Q_RANK, -1)),
           "mla_w_ukv": _col_slabs(d_ukv.reshape(KV_RANK, -1))}
    out.update({n: d_car[n] for n in COLS + ROWS})
    return out


BIG = ("w_in", "w_out", "ffn_w1", "ffn_w3", "ffn_w2", "mla_w_uq", "mla_w_ukv")
SMALL_SHARED = ("norm1_g", "gla_onorm_g", "conf_dw_b", "conf_ln_g", "conf_ln_b", "mla_q_norm_g", "mla_kv_norm_g",
                "norm2_g")
SMALL_SHARDED = ("gla_fg_up", "gla_fg_b", "conf_dw", "sc_dw")
WEIGHTS = ("c_ctx", "norm1_g", "w_mod", "b_mod", "w_in", "gla_fg_up", "gla_fg_b", "gla_onorm_g", "conf_dw",
           "conf_dw_b", "conf_ln_g", "conf_ln_b", "sc_dw", "mla_q_norm_g", "mla_kv_norm_g", "mla_w_uq", "mla_w_ukv",
           "w_out", "norm2_g", "ffn_w1", "ffn_w3", "ffn_w2", "final_norm_g")


def _gather_last(pieces8):
    moved = jnp.moveaxis(pieces8, 0, -2)
    return moved.reshape(moved.shape[:-2] + (-1,))


def _sum_devices(x8, name):
    shape = x8.shape[1:]
    return _sum_leading(x8.reshape(N_DEV, -1, shape[-1]), name).reshape(shape)


def _step(w, m, v, x, c, ctx, loss_target):
    depth = w["norm1_g"].shape[0]
    seq, D = x.shape[1], x.shape[2]
    n_ctx = ctx.shape[1]
    me = 4 * lax.axis_index("x") + 2 * lax.axis_index("y") + lax.axis_index("c")

    G = dict(zip(BIG, _all_gather([w[n].astype(BF16) for n in BIG], "gather_weights")))
    small8 = _all_gather([c] + [w[n] for n in SMALL_SHARDED], "gather_small")
    c_all = small8[0].reshape(N_DEV, D)
    small_full = {n: _gather_last(g) for n, g in zip(SMALL_SHARDED, small8[1:])}

    rows = jnp.concatenate([c_all, w["c_ctx"][None], jnp.zeros((16 - N_DEV - 1, D), F32)])
    (act,) = _rowop(_f_silu, "silu")((rows,), (), ())
    n_mod = w["w_mod"].shape[2]
    b_mine = lax.dynamic_slice_in_dim(w["b_mod"], me * n_mod, n_mod, axis=1)
    mod_cols = [_rowop(_f_add_bias, "add_bias")((_matmul(act, w["w_mod"][i], "nn", "mod_fwd"),), (),
                                                (_shared(b_mine[i]),))[0] for i in range(depth)]
    (mods8,) = _all_gather([jnp.stack(mod_cols)], "gather_mod")
    mods = jnp.moveaxis(mods8, 0, 2).reshape(depth, 16, N_DEV * n_mod)
    mods_l = [lax.dynamic_index_in_dim(mods[i], me, 0, keepdims=False) for i in range(depth)]
    mods_c = [mods[i, N_DEV] for i in range(depth)]

    smalls, plains, cars = [], [], []
    for i in range(depth):
        p = {n: w[n][i] for n in SMALL_SHARED}
        p.update({n: small_full[n][i] for n in SMALL_SHARDED})
        smalls.append(p)
        plains.append(_plain_weights(G, i, D))
        car = {n: lax.empty(a.shape, F32) for n, a in plains[i].items()}
        car.update({n: lax.empty((N_DEV,) + G[n].shape[2:], F32) for n in COLS + ROWS})
        cars.append(car)
    t0 = jnp.concatenate([ctx[0], x[0]], axis=0)
    t_out, vjp = jax.vjp(lambda t, ps, cs, ml, mc: _trunk(t, ps, cs, plains, G, ml, mc, n_ctx, seq),
                         t0, smalls, cars, mods_l, mods_c)
    loss_lanes, d_out, d_final_g = _loss_call(t_out[n_ctx:], w["final_norm_g"][None], loss_target[0])
    d_t0, d_smalls, d_cars, d_mods_l, d_mods_c = vjp(jnp.concatenate([jnp.zeros((n_ctx, D), F32), d_out], axis=0))
    grad_x = d_t0[n_ctx:][None]

    grads = {n: [] for n in BIG}
    for i in range(depth):
        sharded = _shard_grads(d_cars[i])
        for n, g in zip(BIG, _reduce_scatter([sharded[n] for n in BIG])):
            grads[n].append(g)
    grads = {n: jnp.stack(g) for n, g in grads.items()}

    names = SMALL_SHARED + SMALL_SHARDED
    d_mod = jnp.stack([jnp.stack([d_mods_l[i], d_mods_c[i]]) for i in range(depth)])
    parts = [loss_lanes, d_final_g, d_mod] + [jnp.stack([d_smalls[i][n] for i in range(depth)]) for n in names]
    parts8 = _all_gather(parts, "gather_partials")
    summed = [_sum_devices(p8, "sum_partials") for p8 in parts8]
    (loss_row,) = _rowop(lambda a: (jnp.sum(a, axis=-1, keepdims=True) + jnp.zeros_like(a),), "loss_sum")(
        (summed[0],), (), ())
    loss = loss_row[0, 0]
    grads["final_norm_g"] = summed[1].reshape(D)
    for n, g in zip(names, summed[3:]):
        if n in SMALL_SHARDED:
            g = lax.dynamic_slice_in_dim(g, me * w[n].shape[-1], w[n].shape[-1], axis=g.ndim - 1)
        grads[n] = g

    d_mod8, d_mod_c = parts8[2], summed[2][:, 1]
    grads["b_mod"] = _rowop(lambda a, b: (a + b,), "add")((summed[2][:, 0], d_mod_c), (), ())[0]
    d_rows = jnp.concatenate([jnp.moveaxis(d_mod8[:, :, 0], 0, 1), d_mod_c[:, None],
                              jnp.zeros((depth, 16 - N_DEV - 1, 6 * D), F32)], axis=1)
    d_rows = lax.dynamic_slice_in_dim(d_rows, me * n_mod, n_mod, axis=2)
    grads["w_mod"] = jnp.stack([_matmul(act, d_rows[i], "tn", "mod_dw") for i in range(depth)])
    d_act = _matmul(d_rows.transpose(1, 0, 2).reshape(16, depth * n_mod),
                    w["w_mod"].transpose(1, 0, 2).reshape(D, depth * n_mod), "nt", "mod_dact")
    (d_act8,) = _all_gather([d_act], "gather_dact")
    (d_rows_in,) = _rowop(_f_mul_silu_grad, "silu_grad")((_sum_devices(d_act8, "sum_dact"), rows), (), ())
    grads["c_ctx"] = d_rows_in[N_DEV]

    outs = {n: _adamw(w[n], grads[n], m[n], v[n]) for n in WEIGHTS}
    return (loss, grad_x, *[grads[n] for n in WEIGHTS], *[outs[n][0] for n in WEIGHTS],
            *[outs[n][1] for n in WEIGHTS], *[outs[n][2] for n in WEIGHTS])


def kernel(x, c, ctx, c_ctx, norm1_g, w_mod, b_mod, w_in, gla_fg_up, gla_fg_b, gla_onorm_g, conf_dw, conf_dw_b, conf_ln_g, conf_ln_b, sc_dw, mla_q_norm_g, mla_kv_norm_g, mla_w_uq, mla_w_ukv, w_out, norm2_g, ffn_w1, ffn_w3, ffn_w2, final_norm_g, loss_target, m_c_ctx, m_norm1_g, m_w_mod, m_b_mod, m_w_in, m_gla_fg_up, m_gla_fg_b, m_gla_onorm_g, m_conf_dw, m_conf_dw_b, m_conf_ln_g, m_conf_ln_b, m_sc_dw, m_mla_q_norm_g, m_mla_kv_norm_g, m_mla_w_uq, m_mla_w_ukv, m_w_out, m_norm2_g, m_ffn_w1, m_ffn_w3, m_ffn_w2, m_final_norm_g, v_c_ctx, v_norm1_g, v_w_mod, v_b_mod, v_w_in, v_gla_fg_up, v_gla_fg_b, v_gla_onorm_g, v_conf_dw, v_conf_dw_b, v_conf_ln_g, v_conf_ln_b, v_sc_dw, v_mla_q_norm_g, v_mla_kv_norm_g, v_mla_w_uq, v_mla_w_ukv, v_w_out, v_norm2_g, v_ffn_w1, v_ffn_w3, v_ffn_w2, v_final_norm_g):
    w = dict(c_ctx=c_ctx, norm1_g=norm1_g, w_mod=w_mod, b_mod=b_mod, w_in=w_in, gla_fg_up=gla_fg_up, gla_fg_b=gla_fg_b, gla_onorm_g=gla_onorm_g, conf_dw=conf_dw, conf_dw_b=conf_dw_b, conf_ln_g=conf_ln_g, conf_ln_b=conf_ln_b, sc_dw=sc_dw, mla_q_norm_g=mla_q_norm_g, mla_kv_norm_g=mla_kv_norm_g, mla_w_uq=mla_w_uq, mla_w_ukv=mla_w_ukv, w_out=w_out, norm2_g=norm2_g, ffn_w1=ffn_w1, ffn_w3=ffn_w3, ffn_w2=ffn_w2, final_norm_g=final_norm_g)
    m = dict(c_ctx=m_c_ctx, norm1_g=m_norm1_g, w_mod=m_w_mod, b_mod=m_b_mod, w_in=m_w_in, gla_fg_up=m_gla_fg_up, gla_fg_b=m_gla_fg_b, gla_onorm_g=m_gla_onorm_g, conf_dw=m_conf_dw, conf_dw_b=m_conf_dw_b, conf_ln_g=m_conf_ln_g, conf_ln_b=m_conf_ln_b, sc_dw=m_sc_dw, mla_q_norm_g=m_mla_q_norm_g, mla_kv_norm_g=m_mla_kv_norm_g, mla_w_uq=m_mla_w_uq, mla_w_ukv=m_mla_w_ukv, w_out=m_w_out, norm2_g=m_norm2_g, ffn_w1=m_ffn_w1, ffn_w3=m_ffn_w3, ffn_w2=m_ffn_w2, final_norm_g=m_final_norm_g)
    v = dict(c_ctx=v_c_ctx, norm1_g=v_norm1_g, w_mod=v_w_mod, b_mod=v_b_mod, w_in=v_w_in, gla_fg_up=v_gla_fg_up, gla_fg_b=v_gla_fg_b, gla_onorm_g=v_gla_onorm_g, conf_dw=v_conf_dw, conf_dw_b=v_conf_dw_b, conf_ln_g=v_conf_ln_g, conf_ln_b=v_conf_ln_b, sc_dw=v_sc_dw, mla_q_norm_g=v_mla_q_norm_g, mla_kv_norm_g=v_mla_kv_norm_g, mla_w_uq=v_mla_w_uq, mla_w_ukv=v_mla_w_ukv, w_out=v_w_out, norm2_g=v_norm2_g, ffn_w1=v_ffn_w1, ffn_w3=v_ffn_w3, ffn_w2=v_ffn_w2, final_norm_g=v_final_norm_g)
    return _step(w, m, v, x, c, ctx, loss_target)
```

```python
import functools
import math

import jax
import jax.numpy as jnp
from jax import lax
from jax.experimental import pallas as pl
from jax.experimental.pallas import tpu as pltpu

F32 = jnp.float32
BF16 = jnp.bfloat16
MESH = pl.DeviceIdType.MESH
N_DEV = 8

EPS = 1e-6
GRID_W = 64
HEADS = 4
HEAD_W = 128
MIX_W = HEADS * HEAD_W
GATE_RANK = 16
GATE_NORM = 16.0
GLA_CHUNK = 128
CONF_K = 31
SC_K = 3
Q_RANK = 384
KV_RANK = 128
ROPE = 64
ROPE_FREQS = 16
ROPE_BASE = 10000.0
ATT_SCALE = (HEAD_W + ROPE) ** -0.5
CONV_HALO = 16

ADAM_LR = 0.001
ADAM_B1 = 0.9
ADAM_B2 = 0.999
ADAM_EPS = 1e-08
ADAM_WD = 0.01
ADAM_STEP = 10

LANES = 128
VMEM_LIMIT = 56 * 2 ** 20
ROW_BLOCK_BYTES = 10 * 2 ** 20

Z_Q, Z_K, Z_V, Z_G, Z_LR, Z_A, Z_GATE, Z_BG, Z_CG, Z_H, Z_CQ, Z_CKV, Z_KR, Z_END = (
    0, 512, 1024, 1536, 2048, 2176, 2688, 3200, 3712, 4224, 4736, 5120, 5248, 5376)
IN_W = 5216


def _pcall(body, **kw):
    return pl.pallas_call(body, **kw)


def _params(sem=None):
    return pltpu.CompilerParams(dimension_semantics=sem, vmem_limit_bytes=VMEM_LIMIT)


def _pick(dim, cap, mult):
    d = (min(cap, dim) // mult) * mult
    while d >= mult:
        if dim % d == 0:
            return d
        d -= mult
    return dim


def _mm_call(name, a, b, out_shape, grid, a_spec, b_spec, o_spec, dims, k_axis=None, once_axis=None, out_dtype=F32):
    a_blk = tuple(d for d in a_spec.block_shape if d is not None)
    o_blk = tuple(d for d in o_spec.block_shape if d is not None)
    if a.dtype == BF16:
        once_axis = None
    scratch = ([pltpu.VMEM(o_blk, F32)] if k_axis is not None else []) + (
        [pltpu.VMEM(a_blk, BF16)] if once_axis is not None else [])
    nk = grid[k_axis] if k_axis is not None else 1

    def body(a_ref, b_ref, o_ref, *scr):
        if once_axis is not None:
            a_bf = scr[-1]

            @pl.when(pl.program_id(once_axis) == 0)
            def _():
                a_bf[...] = a_ref[...].astype(BF16)

            av = a_bf[...]
        else:
            av = a_ref[...].astype(BF16)
        prod = lax.dot_general(av, b_ref[...].astype(BF16), dims, preferred_element_type=F32)
        if k_axis is None:
            o_ref[...] = prod.astype(o_ref.dtype)
        else:
            acc, k = scr[0], pl.program_id(k_axis)

            @pl.when(k == 0)
            def _():
                acc[...] = prod

            @pl.when(k != 0)
            def _():
                acc[...] += prod

            @pl.when(k == nk - 1)
            def _():
                o_ref[...] = acc[...].astype(o_ref.dtype)

    return _pcall(
        body, name=name, grid=grid, in_specs=[a_spec, b_spec], out_specs=o_spec,
        out_shape=jax.ShapeDtypeStruct(out_shape, out_dtype), scratch_shapes=scratch,
        compiler_params=_params(("arbitrary",) * len(grid)),
    )(a, b)


NN = (((1,), (0,)), ((), ()))
NT = (((1,), (1,)), ((), ()))
TN = (((0,), (0,)), ((), ()))


def _matmul(a, b, mode, name, out_dtype=F32):
    if mode == "nn":
        (M, K), (_, N) = a.shape, b.shape
    elif mode == "nt":
        (M, K), (N, _) = a.shape, b.shape
    else:
        (K, M), (_, N) = a.shape, b.shape
    tn = _pick(N, 768, LANES)
    if mode == "tn":
        tm, tk = _pick(M, 512, LANES), _pick(K, 1088, 16)
    else:
        tm, tk = _pick(M, 1088, 16), _pick(K, 2048, LANES)
    if mode == "nn":
        a_spec = pl.BlockSpec((tm, tk), lambda i, j, k: (i, k))
        b_spec = pl.BlockSpec((tk, tn), lambda i, j, k: (k, j))
    elif mode == "nt":
        a_spec = pl.BlockSpec((tm, tk), lambda i, j, k: (i, k))
        b_spec = pl.BlockSpec((tn, tk), lambda i, j, k: (j, k))
    else:
        a_spec = pl.BlockSpec((tk, tm), lambda i, j, k: (k, i))
        b_spec = pl.BlockSpec((tk, tn), lambda i, j, k: (k, j))
    return _mm_call(name, a, b, (M, N), (M // tm, N // tn, K // tk), a_spec, b_spec,
                    pl.BlockSpec((tm, tn), lambda i, j, k: (i, j)), {"nn": NN, "nt": NT, "tn": TN}[mode], k_axis=2,
                    out_dtype=out_dtype)


@jax.custom_vjp
def mm(a, w, carrier):
    return _matmul(a, w, "nn", "mm_fwd")


def _mm_fwd(a, w, carrier):
    return _matmul(a, w, "nn", "mm_fwd"), (a, w)


def _mm_bwd(res, dc):
    a, w = res
    return _matmul(dc, w, "nt", "mm_da", a.dtype), jnp.zeros_like(w), _matmul(a, dc, "tn", "mm_dw")


mm.defvjp(_mm_fwd, _mm_bwd)


def _make_mm_cols(layer):
    def forward(a, G):
        (M, K), n = a.shape, G.shape[3]
        tm = _pick(M, 1088, 16)
        return _mm_call("mmc_fwd", a, G, (N_DEV, M, n), (M // tm, N_DEV),
                        pl.BlockSpec((tm, K), lambda i, d: (i, 0)),
                        pl.BlockSpec((None, None, K, n), lambda i, d: (d, layer, 0, 0)),
                        pl.BlockSpec((None, tm, n), lambda i, d: (d, i, 0)), NN, once_axis=1, out_dtype=BF16)

    def grad_a(do, G, dtype):
        (_, M, n), K = do.shape, G.shape[2]
        tm, tn = _pick(M, 1088, 16), _pick(K, 1024, LANES)
        return _mm_call("mmc_da", do, G, (M, K), (M // tm, K // tn, N_DEV),
                        pl.BlockSpec((None, tm, n), lambda i, j, d: (d, i, 0)),
                        pl.BlockSpec((None, None, tn, n), lambda i, j, d: (d, layer, j, 0)),
                        pl.BlockSpec((tm, tn), lambda i, j, d: (i, j)), NT, k_axis=2, out_dtype=dtype)

    def grad_w(a, do):
        (M, K), n = a.shape, do.shape[2]
        tm, tk = _pick(K, 1024, LANES), _pick(M, 1088, 16)
        return _mm_call("mmc_dw", a, do, (N_DEV, K, n), (N_DEV, K // tm, M // tk),
                        pl.BlockSpec((tk, tm), lambda d, i, k: (k, i)),
                        pl.BlockSpec((None, tk, n), lambda d, i, k: (d, k, 0)),
                        pl.BlockSpec((None, tm, n), lambda d, i, k: (d, i, 0)), TN, k_axis=2)

    @jax.custom_vjp
    def f(a, G, carrier):
        return forward(a, G)

    def f_fwd(a, G, carrier):
        return forward(a, G), (a, G)

    def f_bwd(res, do):
        a, G = res
        return grad_a(do, G, a.dtype), jnp.zeros_like(G), grad_w(a, do)

    f.defvjp(f_fwd, f_bwd)
    return f


def _make_mm_rows(layer):
    def a_spec(a, tm, r, where):
        if a.ndim == 3:
            return pl.BlockSpec((None, tm, r), lambda *g: (where(*g)[1], where(*g)[0], 0))
        return pl.BlockSpec((tm, r), lambda *g: where(*g))

    def forward(a, G):
        M, (r, N) = a.shape[-2], G.shape[2:]
        tm, tn = _pick(M, 1088, 16), _pick(N, 1024, LANES)
        return _mm_call("mmr_fwd", a, G, (M, N), (M // tm, N // tn, N_DEV),
                        a_spec(a, tm, r, lambda i, j, d: (i, d)),
                        pl.BlockSpec((None, None, r, tn), lambda i, j, d: (d, layer, 0, j)),
                        pl.BlockSpec((tm, tn), lambda i, j, d: (i, j)), NN, k_axis=2)

    def grad_a(dc, G, like):
        (M, N), r = dc.shape, G.shape[2]
        tm = _pick(M, 1088, 16)
        return _mm_call("mmr_da", dc, G, like.shape, (M // tm, N_DEV),
                        pl.BlockSpec((tm, N), lambda i, d: (i, 0)),
                        pl.BlockSpec((None, None, r, N), lambda i, d: (d, layer, 0, 0)),
                        a_spec(like, tm, r, lambda i, d: (i, d)), NT, once_axis=1, out_dtype=like.dtype)

    def grad_w(a, dc):
        (M, N), r = dc.shape, (a.shape[2] if a.ndim == 3 else a.shape[1] // N_DEV)
        tn, tk = _pick(N, 1024, LANES), _pick(M, 1088, 16)
        return _mm_call("mmr_dw", a, dc, (N_DEV, r, N), (N_DEV, N // tn, M // tk),
                        a_spec(a, tk, r, lambda d, j, k: (k, d)),
                        pl.BlockSpec((tk, tn), lambda d, j, k: (k, j)),
                        pl.BlockSpec((None, r, tn), lambda d, j, k: (d, 0, j)), TN, k_axis=2)

    @jax.custom_vjp
    def f(a, G, carrier):
        return forward(a, G)

    def f_fwd(a, G, carrier):
        return forward(a, G), (a, G)

    def f_bwd(res, dc):
        a, G = res
        return grad_a(dc, G, a), jnp.zeros_like(G), grad_w(a, dc)

    f.defvjp(f_fwd, f_bwd)
    return f


@functools.partial(jax.custom_vjp, nondiff_argnums=(1,))
def _split_cols(z, bounds):
    return tuple(z[:, a:b] for a, b in zip(bounds[:-1], bounds[1:]))


def _split_cols_fwd(z, bounds):
    return _split_cols(z, bounds), None


def _split_cols_bwd(bounds, _, cts):
    return (jnp.concatenate(cts, axis=1),)


_split_cols.defvjp(_split_cols_fwd, _split_cols_bwd)


def _rowop(fn, name, n_ctx=0, tile=256, lane_block=None, out_dtypes=None):
    def geometry(rows):
        L, w0 = rows[0].shape
        nj = w0 // lane_block if lane_block else 1
        width = 3 * sum(lane_block or r.shape[1] for r in rows)
        cap = max(16, ROW_BLOCK_BYTES // (4 * width) // 16 * 16)
        tl = _pick(math.gcd(L, n_ctx) if n_ctx else L, min(tile, cap), 16)
        return L, tl, n_ctx // tl, nj

    def block_w(x):
        return lane_block or x.shape[1]

    def row_spec(tl, x):
        if lane_block and x.shape[1] != lane_block:
            return pl.BlockSpec((tl, lane_block), lambda i, j: (i, j))
        return pl.BlockSpec((tl, block_w(x)), lambda i, j: (i, 0))

    def param_spec(p, nct):
        s, r, w = p.shape
        if s == 1:
            return pl.BlockSpec((1, r, w), lambda i, j: (0, 0, 0))
        return pl.BlockSpec((1, r, w), lambda i, j: (jnp.where(i < nct, 0, 1), 0, 0))

    def forward(rows, consts, params):
        L, tl, nct, nj = geometry(rows)
        nr, nc, npar = len(rows), len(consts), len(params)
        outs = jax.eval_shape(
            lambda: fn(*[jnp.zeros((tl, block_w(r)), F32) for r in rows + consts],
                       *[jnp.zeros(p.shape[1:], F32) for p in params]))

        def body(*refs):
            ins = [r[...].astype(F32) for r in refs[:nr + nc]] + [r[0] for r in refs[nr + nc:nr + nc + npar]]
            for o_ref, o in zip(refs[nr + nc + npar:], fn(*ins)):
                o_ref[...] = o.astype(o_ref.dtype)

        return _pcall(
            body, name=name + "_fwd", grid=(L // tl, nj),
            in_specs=[row_spec(tl, r) for r in rows + consts] + [param_spec(p, nct) for p in params],
            out_specs=[pl.BlockSpec((tl, o.shape[1]), lambda i, j: (i, j)) for o in outs],
            out_shape=[jax.ShapeDtypeStruct((L, o.shape[1] * nj), dt)
                       for o, dt in zip(outs, out_dtypes or (F32,) * len(outs))],
            compiler_params=_params(("parallel", "parallel")),
        )(*rows, *consts, *params)

    def backward(rows, consts, params, cts):
        L, tl, nct, nj = geometry(rows)
        nr, nc, npar, no = len(rows), len(consts), len(params), len(cts)

        def body(*refs):
            i, j = pl.program_id(0), pl.program_id(1)
            rv = [r[...].astype(F32) for r in refs[:nr]]
            cv = [r[...] for r in refs[nr:nr + nc]]
            pv = [r[0] for r in refs[nr + nc:nr + nc + npar]]
            ct = tuple(r[...].astype(F32) for r in refs[nr + nc + npar:nr + nc + npar + no])
            out_refs = refs[nr + nc + npar + no:]
            _, vjp = jax.vjp(lambda *d: tuple(fn(*d[:nr], *cv, *d[nr:])), *rv, *pv)
            grads = vjp(ct)
            for ref, g in zip(out_refs[:nr], grads[:nr]):
                ref[...] = g.astype(ref.dtype)
            for ref, g, p in zip(out_refs[nr:], grads[nr:], params):
                first_row = (i == 0) if (p.shape[0] == 1 or nct == 0) else ((i == 0) | (i == nct))
                first = jnp.logical_and(first_row, j == 0)

                @pl.when(first)
                def _():
                    ref[0] = g

                @pl.when(jnp.logical_not(first))
                def _():
                    ref[0] += g

        outs = _pcall(
            body, name=name + "_bwd", grid=(L // tl, nj),
            in_specs=[row_spec(tl, r) for r in rows + consts] + [param_spec(p, nct) for p in params]
            + [pl.BlockSpec((tl, c.shape[1] // nj), lambda i, j: (i, j)) for c in cts],
            out_specs=[row_spec(tl, r) for r in rows] + [param_spec(p, nct) for p in params],
            out_shape=[jax.ShapeDtypeStruct(r.shape, r.dtype) for r in rows]
            + [jax.ShapeDtypeStruct(p.shape, F32) for p in params],
            compiler_params=_params(("arbitrary", "arbitrary")),
        )(*rows, *consts, *params, *cts)
        return tuple(outs[:nr]), tuple(outs[nr:])

    @jax.custom_vjp
    def op(rows, consts, params):
        return tuple(forward(rows, consts, params))

    def op_fwd(rows, consts, params):
        return tuple(forward(rows, consts, params)), (rows, consts, params)

    def op_bwd(res, cts):
        rows, consts, params = res
        d_rows, d_params = backward(rows, consts, params, tuple(cts))
        return d_rows, tuple(jnp.zeros_like(c) for c in consts), d_params

    op.defvjp(op_fwd, op_bwd)
    return op


def _sigmoid(x):
    return 1.0 / (1.0 + jnp.exp(-x))


def _silu(x):
    return x * _sigmoid(x)


def _log_sigmoid(x):
    return jnp.minimum(x, 0.0) - jnp.log(1.0 + jnp.exp(-jnp.abs(x)))


def _rms(x, g):
    return x * lax.rsqrt(jnp.mean(x * x, axis=-1, keepdims=True) + EPS) * g


def _f_norm_mod(x, g, sc, sh):
    return (_rms(x, g) * (1.0 + sc) + sh,)


def _f_resid_norm_mod(x, o, gate, g, sc, sh):
    x1 = x + gate * o
    return x1, _rms(x1, g) * (1.0 + sc) + sh


def _f_resid(x, o, gate):
    return (x + gate * o,)


def _f_swiglu(a1, a3):
    return (_silu(a1) * a3,)


def _f_gla_gate(lr, up_f, up_b, b_f, b_b):
    dot = functools.partial(jnp.dot, preferred_element_type=F32)
    return (_log_sigmoid(dot(lr, up_f) + b_f) / GATE_NORM, _log_sigmoid(dot(lr, up_b) + b_b) / GATE_NORM)


def _f_gla_finish(o_f, o_b, gate, g):
    return (_rms(o_f + o_b, g) * _silu(gate),)


def _f_glu(a, gate):
    return (a * _sigmoid(gate),)


def _f_ln_silu(u, dw_b, g, b):
    u = u + dw_b
    xc = u - jnp.mean(u, axis=-1, keepdims=True)
    y = xc * lax.rsqrt(jnp.mean(xc * xc, axis=-1, keepdims=True) + EPS)
    return (_silu(y * g + b),)


def _f_mul(a, b):
    return (a * b,)


def _f_rms(x, g):
    return (_rms(x, g),)


def _f_rope(t, cos, sin):
    w = t.shape[1]
    r = lax.broadcasted_iota(jnp.int32, (w, w), 0)
    c = lax.broadcasted_iota(jnp.int32, (w, w), 1)
    perm = (jnp.bitwise_xor(r, ROPE_FREQS) == c).astype(F32)
    partner = jnp.dot(t, perm, precision=lax.Precision.HIGHEST, preferred_element_type=F32)
    return (t * cos + partner * sin,)


def _f_silu(x):
    return (_silu(x),)


def _f_add_bias(x, b):
    return (x + b,)


def _f_mul_silu_grad(d, x):
    _, vjp = jax.vjp(_silu, x)
    return (vjp(d)[0],)


def _conv_geometry(u, n_ctx):
    L, C = u.shape
    tl = _pick(math.gcd(L, n_ctx), 256, 8)
    return L, C, tl, n_ctx // tl, L // tl


def _conv_specs(tl, nt):
    prev = pl.BlockSpec((tl, LANES), lambda c, i: (jnp.maximum(i - 1, 0), c))
    cur = pl.BlockSpec((tl, LANES), lambda c, i: (i, c))
    nxt = pl.BlockSpec((tl, LANES), lambda c, i: (jnp.minimum(i + 1, nt - 1), c))
    return prev, cur, nxt


def _conv_window(prev_ref, cur_ref, next_ref, tl, nct, nt):
    i = pl.program_id(1)
    has_prev = jnp.logical_and(i != 0, i != nct)
    has_next = jnp.logical_and(i != nct - 1, i != nt - 1)
    prev = jnp.where(has_prev, prev_ref[tl - CONV_HALO:tl, :], 0.0)
    nxt = jnp.where(has_next, next_ref[0:CONV_HALO, :], 0.0)
    return jnp.concatenate([prev, cur_ref[...], nxt], axis=0)


def _shifted(window, off, tl):
    n = window.shape[0]
    if off == 0:
        return window[0:tl]
    return pltpu.roll(window, n - off, 0)[0:tl]


def _conv_apply(u, w, n_ctx, flip, name):
    L, C, tl, nct, nt = _conv_geometry(u, n_ctx)
    K = w.shape[0]
    pad = (K - 1) // 2
    prev, cur, nxt = _conv_specs(tl, nt)

    def body(p_ref, c_ref, n_ref, w_ref, o_ref):
        win = _conv_window(p_ref, c_ref, n_ref, tl, nct, nt)
        acc = jnp.zeros((tl, LANES), F32)
        for k in range(K):
            kk = K - 1 - k if flip else k
            acc = acc + _shifted(win, CONV_HALO - pad + k, tl) * w_ref[kk:kk + 1, :]
        o_ref[...] = acc

    return _pcall(
        body, name=name, grid=(C // LANES, nt),
        in_specs=[prev, cur, nxt, pl.BlockSpec((K, LANES), lambda c, i: (0, c))],
        out_specs=cur, out_shape=jax.ShapeDtypeStruct((L, C), F32),
        compiler_params=_params(("parallel", "parallel")),
    )(u, u, u, w)


def _conv_dw(u, dy, K, n_ctx, name):
    L, C, tl, nct, nt = _conv_geometry(u, n_ctx)
    pad = (K - 1) // 2
    prev, cur, nxt = _conv_specs(tl, nt)

    def body(p_ref, c_ref, n_ref, dy_ref, dw_ref):
        i = pl.program_id(1)

        @pl.when(i == 0)
        def _():
            dw_ref[...] = jnp.zeros_like(dw_ref)

        win = _conv_window(p_ref, c_ref, n_ref, tl, nct, nt)
        dy_t = dy_ref[...]
        for k in range(K):
            dw_ref[k:k + 1, :] += jnp.sum(_shifted(win, CONV_HALO - pad + k, tl) * dy_t, axis=0, keepdims=True)

    return _pcall(
        body, name=name, grid=(C // LANES, nt),
        in_specs=[prev, cur, nxt, cur],
        out_specs=pl.BlockSpec((K, LANES), lambda c, i: (0, c)),
        out_shape=jax.ShapeDtypeStruct((K, C), F32),
        compiler_params=_params(("parallel", "arbitrary")),
    )(u, u, u, dy)


def _make_conv(n_ctx):
    @jax.custom_vjp
    def conv(u, w):
        return _conv_apply(u, w, n_ctx, False, "conv_fwd")

    def conv_fwd(u, w):
        return _conv_apply(u, w, n_ctx, False, "conv_fwd"), (u, w)

    def conv_bwd(res, dy):
        u, w = res
        return _conv_apply(dy, w, n_ctx, True, "conv_du"), _conv_dw(u, dy, w.shape[0], n_ctx, "conv_dw")

    conv.defvjp(conv_fwd, conv_bwd)
    return conv


def _gla_chunk(q, k, v, g, st, reverse):
    C = q.shape[0]
    r = lax.broadcasted_iota(jnp.int32, (C, C), 0)
    c = lax.broadcasted_iota(jnp.int32, (C, C), 1)
    seen = (r <= c) if reverse else (r >= c)
    dot = functools.partial(lax.dot_general, preferred_element_type=F32)
    bcum = dot(seen.astype(F32), g, (((1,), (0,)), ((), ())), precision=lax.Precision.HIGHEST)
    total = jnp.sum(g, axis=0, keepdims=True)
    a = q * (HEAD_W ** -0.5) * jnp.exp(bcum)
    scores = jnp.where(seen, dot(a, k * jnp.exp(-bcum), (((1,), (1,)), ((), ()))), 0.0)
    o = dot(a, st, (((1,), (1,)), ((), ()))) + dot(scores, v, (((1,), (0,)), ((), ())))
    st_new = st * jnp.exp(total) + dot(v, k * jnp.exp(total - bcum), (((0,), (0,)), ((), ())))
    return o, st_new


def _gla_order(t, nc, ncc, reverse):
    if not reverse:
        return t
    return jnp.where(t < ncc, ncc - 1 - t, ncc + nc - 1 - t)


def _gla_fwd_call(q, k, v, g, n_ctx, reverse):
    L = q.shape[0]
    C = GLA_CHUNK
    nc, ncc = L // C, n_ctx // C
    spec = pl.BlockSpec((C, HEAD_W), lambda h, t: (_gla_order(t, nc, ncc, reverse), h))

    def body(q_ref, k_ref, v_ref, g_ref, o_ref, s_ref, st):
        @pl.when(pl.program_id(1) == 0)
        def _():
            st[...] = jnp.zeros_like(st)

        s_ref[0, 0] = st[...]
        o, st_new = _gla_chunk(q_ref[...], k_ref[...], v_ref[...], g_ref[...], st[...], reverse)
        o_ref[...] = o
        st[...] = st_new

    return _pcall(
        body, name="gla_fwd", grid=(HEADS, nc), in_specs=[spec] * 4,
        out_specs=[spec, pl.BlockSpec((1, 1, HEAD_W, HEAD_W), lambda h, t: (h, t, 0, 0))],
        out_shape=[jax.ShapeDtypeStruct((L, MIX_W), F32), jax.ShapeDtypeStruct((HEADS, nc, HEAD_W, HEAD_W), F32)],
        scratch_shapes=[pltpu.VMEM((HEAD_W, HEAD_W), F32)],
        compiler_params=_params(("parallel", "arbitrary")),
    )(q, k, v, g)


def _gla_bwd_call(q, k, v, g, states, do, n_ctx, reverse):
    L = q.shape[0]
    C = GLA_CHUNK
    nc, ncc = L // C, n_ctx // C
    spec = pl.BlockSpec((C, HEAD_W), lambda h, t: (_gla_order(nc - 1 - t, nc, ncc, reverse), h))

    def body(q_ref, k_ref, v_ref, g_ref, s_ref, do_ref, dq_ref, dk_ref, dv_ref, dg_ref, dst):
        @pl.when(pl.program_id(1) == 0)
        def _():
            dst[...] = jnp.zeros_like(dst)

        _, vjp = jax.vjp(functools.partial(_gla_chunk, reverse=reverse),
                         q_ref[...], k_ref[...], v_ref[...], g_ref[...], s_ref[0, 0])
        dq, dk, dv, dg, dst_prev = vjp((do_ref[...], dst[...]))
        dq_ref[...] = dq
        dk_ref[...] = dk
        dv_ref[...] = dv
        dg_ref[...] = dg
        dst[...] = dst_prev

    return _pcall(
        body, name="gla_bwd", grid=(HEADS, nc),
        in_specs=[spec] * 4 + [pl.BlockSpec((1, 1, HEAD_W, HEAD_W), lambda h, t: (h, nc - 1 - t, 0, 0)), spec],
        out_specs=[spec] * 4, out_shape=[jax.ShapeDtypeStruct((L, MIX_W), F32)] * 4,
        scratch_shapes=[pltpu.VMEM((HEAD_W, HEAD_W), F32)],
        compiler_params=_params(("parallel", "arbitrary")),
    )(q, k, v, g, states, do)


def _make_gla(n_ctx, reverse):
    @jax.custom_vjp
    def gla(q, k, v, g):
        return _gla_fwd_call(q, k, v, g, n_ctx, reverse)[0]

    def gla_fwd(q, k, v, g):
        o, states = _gla_fwd_call(q, k, v, g, n_ctx, reverse)
        return o, (q, k, v, g, states)

    def gla_bwd(res, do):
        q, k, v, g, states = res
        return tuple(_gla_bwd_call(q, k, v, g, states, do, n_ctx, reverse))

    gla.defvjp(gla_fwd, gla_bwd)
    return gla


def _att_probs(qn, qr, kn, kr, i, nct, n_ctx):
    nt_dims = (((1,), (1,)), ((), ()))
    s = lax.dot_general(qn, kn, nt_dims, preferred_element_type=F32)
    s = (s + lax.dot_general(qr, kr, nt_dims, preferred_element_type=F32)) * ATT_SCALE
    col = lax.broadcasted_iota(jnp.int32, s.shape, 1)
    s = jnp.where(col < jnp.where(i < nct, n_ctx, s.shape[1]), s, -1e30)
    p = jnp.exp(s - jnp.max(s, axis=-1, keepdims=True))
    return p / jnp.sum(p, axis=-1, keepdims=True)


def _att_geometry(qn, n_ctx):
    L = qn.shape[0]
    tq = _pick(math.gcd(L, n_ctx), 256, 8)
    q_spec = pl.BlockSpec((tq, HEAD_W), lambda h, i: (i, h))
    k_spec = pl.BlockSpec((L, HEAD_W), lambda h, i: (0, h))
    kr_spec = pl.BlockSpec((L, HEAD_W), lambda h, i: (0, 0))
    return L, tq, n_ctx // tq, q_spec, k_spec, kr_spec


def _att_fwd_call(qn, qr, kn, kr, v, n_ctx):
    L, tq, nct, q_spec, k_spec, kr_spec = _att_geometry(qn, n_ctx)

    def body(qn_ref, qr_ref, kn_ref, kr_ref, v_ref, o_ref):
        p = _att_probs(qn_ref[...].astype(BF16), qr_ref[...].astype(BF16), kn_ref[...].astype(BF16),
                       kr_ref[...].astype(BF16), pl.program_id(1), nct, n_ctx)
        o_ref[...] = jnp.dot(p.astype(BF16), v_ref[...].astype(BF16), preferred_element_type=F32).astype(BF16)

    return _pcall(
        body, name="att_fwd", grid=(HEADS, L // tq),
        in_specs=[q_spec, q_spec, k_spec, kr_spec, k_spec], out_specs=q_spec,
        out_shape=jax.ShapeDtypeStruct((L, MIX_W), BF16),
        compiler_params=_params(("parallel", "parallel")),
    )(qn, qr, kn, kr, v)


def _att_bwd_call(qn, qr, kn, kr, v, do, n_ctx):
    L, tq, nct, q_spec, k_spec, kr_spec = _att_geometry(qn, n_ctx)
    tn_dims = (((0,), (0,)), ((), ()))

    def body(qn_ref, qr_ref, kn_ref, kr_ref, v_ref, do_ref, dqn_ref, dqr_ref, dkn_ref, dkr_ref, dv_ref):
        h, i = pl.program_id(0), pl.program_id(1)
        qn, qr = qn_ref[...].astype(BF16), qr_ref[...].astype(BF16)
        kn, kr, vv = kn_ref[...].astype(BF16), kr_ref[...].astype(BF16), v_ref[...].astype(BF16)
        do = do_ref[...].astype(BF16)
        p = _att_probs(qn, qr, kn, kr, i, nct, n_ctx)
        dp = lax.dot_general(do, vv, (((1,), (1,)), ((), ())), preferred_element_type=F32)
        ds = (p * (dp - jnp.sum(p * dp, axis=-1, keepdims=True)) * ATT_SCALE).astype(BF16)
        dqn_ref[...] = jnp.dot(ds, kn, preferred_element_type=F32)
        dqr_ref[...] = jnp.dot(ds, kr, preferred_element_type=F32)

        @pl.when(i == 0)
        def _():
            dkn_ref[...] = jnp.zeros_like(dkn_ref)
            dv_ref[...] = jnp.zeros_like(dv_ref)

        @pl.when(jnp.logical_and(i == 0, h == 0))
        def _():
            dkr_ref[...] = jnp.zeros_like(dkr_ref)

        dkn_ref[...] += lax.dot_general(ds, qn, tn_dims, preferred_element_type=F32)
        dkr_ref[...] += lax.dot_general(ds, qr, tn_dims, preferred_element_type=F32)
        dv_ref[...] += lax.dot_general(p.astype(BF16), do, tn_dims, preferred_element_type=F32)

    return _pcall(
        body, name="att_bwd", grid=(HEADS, L // tq),
        in_specs=[q_spec, q_spec, k_spec, kr_spec, k_spec, q_spec],
        out_specs=[q_spec, q_spec, k_spec, kr_spec, k_spec],
        out_shape=[jax.ShapeDtypeStruct((L, MIX_W), F32)] * 3 + [jax.ShapeDtypeStruct((L, HEAD_W), F32),
                                                                 jax.ShapeDtypeStruct((L, MIX_W), F32)],
        compiler_params=_params(("arbitrary", "arbitrary")),
    )(qn, qr, kn, kr, v, do)


def _make_attention(n_ctx):
    @jax.custom_vjp
    def att(qn, qr, kn, kr, v):
        return _att_fwd_call(qn, qr, kn, kr, v, n_ctx)

    def att_fwd(qn, qr, kn, kr, v):
        return _att_fwd_call(qn, qr, kn, kr, v, n_ctx), (qn, qr, kn, kr, v)

    def att_bwd(res, do):
        return tuple(_att_bwd_call(*res, do, n_ctx))

    att.defvjp(att_fwd, att_bwd)
    return att


def _loss_call(x, g, target):
    L, D = x.shape
    tl = _pick(L, 256, 8)

    def f(xv, gv, tv):
        err = _rms(xv, gv) - tv
        return 0.5 * jnp.sum(err * err, axis=0, keepdims=True) / D

    def body(x_ref, g_ref, t_ref, loss_ref, dx_ref, dg_ref):
        i = pl.program_id(0)
        loss, vjp = jax.vjp(lambda xv, gv: f(xv, gv, t_ref[...]), x_ref[...], g_ref[...])
        dx, dg = vjp(jnp.ones_like(loss))
        dx_ref[...] = dx

        @pl.when(i == 0)
        def _():
            loss_ref[...] = loss
            dg_ref[...] = dg

        @pl.when(i != 0)
        def _():
            loss_ref[...] += loss
            dg_ref[...] += dg

    row = pl.BlockSpec((tl, D), lambda i: (i, 0))
    one = pl.BlockSpec((1, D), lambda i: (0, 0))
    return _pcall(
        body, name="loss", grid=(L // tl,), in_specs=[row, one, row], out_specs=[one, row, one],
        out_shape=[jax.ShapeDtypeStruct((1, D), F32), jax.ShapeDtypeStruct((L, D), F32),
                   jax.ShapeDtypeStruct((1, D), F32)],
        compiler_params=_params(("arbitrary",)),
    )(x, g, target)


def _sum_leading(x, name):
    n, R, W = x.shape
    tr = _pick(R, 512, 8)

    def body(x_ref, o_ref):
        acc = x_ref[0]
        for d in range(1, n):
            acc = acc + x_ref[d]
        o_ref[...] = acc

    return _pcall(
        body, name=name, grid=(R // tr,), in_specs=[pl.BlockSpec((n, tr, W), lambda i: (0, i, 0))],
        out_specs=pl.BlockSpec((tr, W), lambda i: (i, 0)), out_shape=jax.ShapeDtypeStruct((R, W), F32),
        compiler_params=_params(("parallel",)),
    )(x)


def _adamw(w, g, m, v):
    shape = w.shape
    W = shape[-1]
    as2d = lambda t: t.reshape(-1, W)
    R = as2d(w).shape[0]
    tr = _pick(R, max(8, (2 ** 17 // W) // 8 * 8), 8)

    def body(w_ref, g_ref, m_ref, v_ref, d_ref, nm_ref, nv_ref):
        gv = g_ref[...]
        m_new = ADAM_B1 * m_ref[...] + (1.0 - ADAM_B1) * gv
        v_new = ADAM_B2 * v_ref[...] + (1.0 - ADAM_B2) * (gv * gv)
        m_hat = m_new / (1.0 - ADAM_B1 ** ADAM_STEP)
        v_hat = v_new / (1.0 - ADAM_B2 ** ADAM_STEP)
        d_ref[...] = -ADAM_LR * (m_hat / (jnp.sqrt(v_hat) + ADAM_EPS) + ADAM_WD * w_ref[...])
        nm_ref[...] = m_new
        nv_ref[...] = v_new

    spec = pl.BlockSpec((tr, W), lambda i: (i, 0))
    outs = _pcall(
        body, name="adamw", grid=(R // tr,), in_specs=[spec] * 4, out_specs=[spec] * 3,
        out_shape=[jax.ShapeDtypeStruct((R, W), F32)] * 3, compiler_params=_params(("parallel",)),
    )(as2d(w), as2d(g), as2d(m), as2d(v))
    return tuple(o.reshape(shape) for o in outs)


HBM = pl.BlockSpec(memory_space=pltpu.HBM)


def _place():
    x, y, c = lax.axis_index("x"), lax.axis_index("y"), lax.axis_index("c")
    return x, y, c, [(1 - x, y), (x, 1 - y), (1 - x, 1 - y)]


def _all_gather(blocks, name):
    n = len(blocks)

    def body(*refs):
        x_refs, out_refs = refs[:n], refs[n:2 * n]
        send_sems, recv_sems, local_sems = refs[2 * n:]
        x, y, c, chips = _place()
        me, sibling = (x, y, c), (x, y, 1 - c)

        def slot(a, px, py, pc):
            return out_refs[a].at[4 * px + 2 * py + pc]

        def copy(a, k, blk, to, src=None):
            return pltpu.make_async_remote_copy(
                src_ref=slot(a, *blk) if src is None else src, dst_ref=slot(a, *blk),
                send_sem=send_sems.at[7 * a + k], recv_sem=recv_sems.at[7 * a + k], device_id=to, device_id_type=MESH)

        mine = [pltpu.make_async_copy(x_refs[a], slot(a, *me), local_sems.at[a]) for a in range(n)]
        for cp in mine:
            cp.start()
        first = []
        for a in range(n):
            first.append(copy(a, 0, me, sibling, src=x_refs[a]))
            first += [copy(a, 1 + j, me, (*chip, c), src=x_refs[a]) for j, chip in enumerate(chips)]
        for cp in first:
            cp.start()
        passed = []
        for j, chip in enumerate(chips):
            for a in range(n):
                copy(a, 1 + j, (*chip, c), me).wait_recv()
                passed.append(copy(a, 4 + j, (*chip, c), sibling))
                passed[-1].start()
        for a in range(n):
            copy(a, 0, sibling, me).wait_recv()
        for j, chip in enumerate(chips):
            for a in range(n):
                copy(a, 4 + j, (*chip, 1 - c), me).wait_recv()
        for cp in first + passed:
            cp.wait_send()
        for cp in mine:
            cp.wait()

    return _pcall(
        body, name=name, out_shape=[jax.ShapeDtypeStruct((N_DEV,) + b.shape, b.dtype) for b in blocks],
        in_specs=[HBM] * n, out_specs=[HBM] * n,
        scratch_shapes=[pltpu.SemaphoreType.DMA((7 * n,)), pltpu.SemaphoreType.DMA((7 * n,)),
                        pltpu.SemaphoreType.DMA((n,))],
    )(*blocks)


def _send_to_sibling(gs, name):
    n = len(gs)

    def body(*refs):
        g_refs, out_refs, (send_sems, recv_sems) = refs[:n], refs[n:2 * n], refs[2 * n:]
        x, y, c, _ = _place()
        copies = [pltpu.make_async_remote_copy(
            src_ref=g_refs[a].at[2 * q + 1 - c], dst_ref=out_refs[a].at[q], send_sem=send_sems.at[4 * a + q],
            recv_sem=recv_sems.at[4 * a + q], device_id=(x, y, 1 - c), device_id_type=MESH)
            for a in range(n) for q in range(4)]
        for cp in copies:
            cp.start()
        for cp in copies:
            cp.wait()

    return _pcall(
        body, name=name, out_shape=[jax.ShapeDtypeStruct((4,) + g.shape[1:], g.dtype) for g in gs],
        in_specs=[HBM] * n, out_specs=[HBM] * n,
        scratch_shapes=[pltpu.SemaphoreType.DMA((4 * n,)), pltpu.SemaphoreType.DMA((4 * n,))],
    )(*gs)


def _send_to_chips(ps, name):
    n = len(ps)

    def body(*refs):
        p_refs, out_refs, (send_sems, recv_sems) = refs[:n], refs[n:2 * n], refs[2 * n:]
        x, y, c, chips = _place()
        copies = [pltpu.make_async_remote_copy(
            src_ref=p_refs[a].at[2 * cx + cy], dst_ref=out_refs[a].at[j], send_sem=send_sems.at[3 * a + j],
            recv_sem=recv_sems.at[3 * a + j], device_id=(cx, cy, c), device_id_type=MESH)
            for a in range(n) for j, (cx, cy) in enumerate(chips)]
        for cp in copies:
            cp.start()
        for cp in copies:
            cp.wait()

    return _pcall(
        body, name=name, out_shape=[jax.ShapeDtypeStruct((3,) + p.shape[1:], p.dtype) for p in ps],
        in_specs=[HBM] * n, out_specs=[HBM] * n,
        scratch_shapes=[pltpu.SemaphoreType.DMA((3 * n,)), pltpu.SemaphoreType.DMA((3 * n,))],
    )(*ps)


def _add_rows(R, W):
    return _pick(R, max(16, 2 ** 19 // W // 16 * 16), 16)


def _add_sibling(g, recv, core):
    _, R, W = g.shape
    tr = _add_rows(R, W)

    def body(core_ref, g_ref, r_ref, o_ref):
        o_ref[...] = (g_ref[...] + r_ref[...]).astype(BF16)

    return _pcall(
        body, name="rs_add_sibling",
        grid_spec=pltpu.PrefetchScalarGridSpec(
            num_scalar_prefetch=1, grid=(4, R // tr),
            in_specs=[pl.BlockSpec((None, tr, W), lambda q, i, core_ref: (2 * q + core_ref[0], i, 0)),
                      pl.BlockSpec((None, tr, W), lambda q, i, core_ref: (q, i, 0))],
            out_specs=pl.BlockSpec((None, tr, W), lambda q, i, core_ref: (q, i, 0))),
        out_shape=jax.ShapeDtypeStruct((4, R, W), BF16), compiler_params=_params(("parallel", "parallel")),
    )(core, g, recv)


def _add_chips(p, recv, chip):
    _, R, W = p.shape
    tr = _add_rows(R, W)

    def body(chip_ref, p_ref, r_ref, o_ref):
        up = lambda t: t.astype(F32)
        o_ref[...] = ((up(p_ref[...]) + up(r_ref[0])) + up(r_ref[1])) + up(r_ref[2])

    return _pcall(
        body, name="rs_add_chips",
        grid_spec=pltpu.PrefetchScalarGridSpec(
            num_scalar_prefetch=1, grid=(R // tr,),
            in_specs=[pl.BlockSpec((None, tr, W), lambda i, chip_ref: (chip_ref[0], i, 0)),
                      pl.BlockSpec((3, tr, W), lambda i, chip_ref: (0, i, 0))],
            out_specs=pl.BlockSpec((tr, W), lambda i, chip_ref: (i, 0))),
        out_shape=jax.ShapeDtypeStruct((R, W), F32), compiler_params=_params(("parallel",)),
    )(chip, p, recv)


def _reduce_scatter(gs):
    x, y, c = lax.axis_index("x"), lax.axis_index("y"), lax.axis_index("c")
    core = jnp.reshape(c, (1,)).astype(jnp.int32)
    chip = jnp.reshape(2 * x + y, (1,)).astype(jnp.int32)
    from_sibling = _send_to_sibling(gs, "rs_sibling")
    chip_sums = [_add_sibling(g, r, core) for g, r in zip(gs, from_sibling)]
    from_chips = _send_to_chips(chip_sums, "rs_chips")
    return [_add_chips(p, r, chip) for p, r in zip(chip_sums, from_chips)]


def _rope_tables(seq, n_ctx):
    rows = seq // GRID_W
    row = jnp.repeat(jnp.arange(rows, dtype=F32), GRID_W)
    col = jnp.tile(jnp.arange(GRID_W, dtype=F32), rows)
    inv = ROPE_BASE ** (-jnp.arange(ROPE_FREQS, dtype=F32) * 2.0 / (ROPE // 2))
    ang_r, ang_c = row[:, None] * inv, col[:, None] * inv
    one, zero = jnp.ones((seq, ROPE), F32), jnp.zeros((seq, ROPE), F32)
    cos = jnp.concatenate([jnp.cos(ang_r), jnp.cos(ang_r), jnp.cos(ang_c), jnp.cos(ang_c), one], axis=1)
    sin = jnp.concatenate([-jnp.sin(ang_r), jnp.sin(ang_r), -jnp.sin(ang_c), jnp.sin(ang_c), zero], axis=1)
    cos = jnp.concatenate([jnp.ones((n_ctx, LANES), F32), cos], axis=0)
    sin = jnp.concatenate([jnp.zeros((n_ctx, LANES), F32), sin], axis=0)
    return cos, sin


def _shared(v):
    return v.reshape((1, 1, -1)) if v.ndim == 1 else v.reshape((1,) + v.shape)


Z_BOUNDS = (Z_Q, Z_K, Z_V, Z_G, Z_LR, Z_A, Z_GATE, Z_BG, Z_CG, Z_H, Z_CQ, Z_CKV, Z_KR, Z_END)
PLAIN = ("w_in", "w_qn", "w_qr", "w_kn", "w_v")
COLS = ("ffn_w1", "ffn_w3")
ROWS = ("w_out", "ffn_w2")


def _layer(t, p, car, plain, G, layer, mod_l, mod_c, n_ctx, tables):
    L, D = t.shape
    seg = lambda i: jnp.stack([mod_c[i * D:(i + 1) * D], mod_l[i * D:(i + 1) * D]]).reshape(2, 1, D)
    sh1, sc1, g1, sh2, sc2, g2 = (seg(i) for i in range(6))
    dense = lambda a, n: mm(a, plain[n], car[n])

    (h,) = _rowop(_f_norm_mod, "norm_mod", n_ctx, out_dtypes=(BF16,))((t,), (), (_shared(p["norm1_g"]), sc1, sh1))
    q, k, v, gate, lr, conf_a, conf_gate, sc_b, sc_c, sc_h, cq, ckv, kr = _split_cols(dense(h, "w_in"), Z_BOUNDS)

    up = p["gla_fg_up"]
    up_f = jnp.pad(up[0], ((0, LANES - GATE_RANK), (0, 0)))
    up_b = jnp.pad(up[1], ((GATE_RANK, LANES - 2 * GATE_RANK), (0, 0)))
    logd_f, logd_b = _rowop(_f_gla_gate, "gla_gate")(
        (lr,), (), (_shared(up_f), _shared(up_b), _shared(p["gla_fg_b"][0]), _shared(p["gla_fg_b"][1])))
    o_f = _make_gla(n_ctx, False)(q, k, v, logd_f)
    o_b = _make_gla(n_ctx, True)(q, k, v, logd_b)
    (gla,) = _rowop(_f_gla_finish, "gla_finish", lane_block=HEAD_W, out_dtypes=(BF16,))(
        (o_f, o_b, gate), (), (_shared(p["gla_onorm_g"]),))

    conv = _make_conv(n_ctx)
    (u,) = _rowop(_f_glu, "glu")((conf_a, conf_gate), (), ())
    (conf,) = _rowop(_f_ln_silu, "ln_silu", out_dtypes=(BF16,))(
        (conv(u, p["conf_dw"]),), (), (_shared(p["conf_dw_b"]), _shared(p["conf_ln_g"]), _shared(p["conf_ln_b"])))

    (ch,) = _rowop(_f_mul, "mul")((sc_c, sc_h), (), ())
    (sconv,) = _rowop(_f_mul, "mul_out", out_dtypes=(BF16,))((sc_b, conv(ch, p["sc_dw"])), (), ())

    (cq,) = _rowop(_f_rms, "rms")((cq,), (), (_shared(p["mla_q_norm_g"]),))
    (ckv,) = _rowop(_f_rms, "rms")((ckv,), (), (_shared(p["mla_kv_norm_g"]),))
    rope = _rowop(_f_rope, "rope", lane_block=LANES)
    (qr,) = rope((dense(cq, "w_qr"),), tables, ())
    (kr,) = rope((kr,), tables, ())
    mla = _make_attention(n_ctx)(dense(cq, "w_qn"), qr, dense(ckv, "w_kn"), kr, dense(ckv, "w_v"))

    o = _make_mm_rows(layer)(jnp.concatenate([gla, conf, sconv, mla], axis=1), G["w_out"], car["w_out"])
    t1, h2 = _rowop(_f_resid_norm_mod, "resid_norm_mod", n_ctx, out_dtypes=(F32, BF16))(
        (t, o), (), (g1, _shared(p["norm2_g"]), sc2, sh2))
    a1 = _make_mm_cols(layer)(h2, G["ffn_w1"], car["ffn_w1"])
    a3 = _make_mm_cols(layer)(h2, G["ffn_w3"], car["ffn_w3"])
    n_ff = a1.shape[2]
    (act,) = _rowop(_f_swiglu, "swiglu", out_dtypes=(BF16,))(
        (a1.reshape(N_DEV * L, n_ff), a3.reshape(N_DEV * L, n_ff)), (), ())
    f = _make_mm_rows(layer)(act.reshape(N_DEV, L, n_ff), G["ffn_w2"], car["ffn_w2"])
    (t2,) = _rowop(_f_resid, "resid", n_ctx)((t1, f), (), (g2,))
    return t2


def _trunk(t, smalls, cars, plains, G, mods_l, mods_c, n_ctx, seq):
    tables = _rope_tables(seq, n_ctx)
    for layer, (p, car, plain, mod_l, mod_c) in enumerate(zip(smalls, cars, plains, mods_l, mods_c)):
        t = _layer(t, p, car, plain, G, layer, mod_l, mod_c, n_ctx, tables)
    return t


def _plain_weights(G, layer, D):
    full = lambda n: jnp.concatenate([G[n][d, layer] for d in range(N_DEV)], axis=1)
    w_in = full("w_in")
    w_in = jnp.concatenate([w_in[:, :Z_LR + 2 * GATE_RANK], jnp.zeros((D, Z_A - Z_LR - 2 * GATE_RANK), BF16),
                            w_in[:, Z_LR + 2 * GATE_RANK:], jnp.zeros((D, Z_END - Z_KR - ROPE), BF16)], axis=1)
    w_uq = full("mla_w_uq").reshape(Q_RANK, HEADS, HEAD_W + ROPE)
    w_ukv = full("mla_w_ukv").reshape(KV_RANK, HEADS, 2 * HEAD_W)
    return {"w_in": w_in,
            "w_qn": w_uq[:, :, :HEAD_W].reshape(Q_RANK, MIX_W),
            "w_qr": jnp.pad(w_uq[:, :, HEAD_W:], ((0, 0), (0, 0), (0, LANES - ROPE))).reshape(Q_RANK, HEADS * LANES),
            "w_kn": w_ukv[:, :, :HEAD_W].reshape(KV_RANK, MIX_W),
            "w_v": w_ukv[:, :, HEAD_W:].reshape(KV_RANK, MIX_W)}


def _col_slabs(full):
    n = full.shape[1] // N_DEV
    return jnp.stack([full[:, d * n:(d + 1) * n] for d in range(N_DEV)])


def _shard_grads(d_car):
    d_in = d_car["w_in"]
    d_in = jnp.concatenate([d_in[:, :Z_LR + 2 * GATE_RANK], d_in[:, Z_A:Z_KR + ROPE]], axis=1)
    by_head = lambda g: g.reshape(g.shape[0], HEADS, -1)
    d_uq = jnp.concatenate([by_head(d_car["w_qn"]), by_head(d_car["w_qr"])[:, :, :ROPE]], axis=2)
    d_ukv = jnp.concatenate([by_head(d_car["w_kn"]), by_head(d_car["w_v"])], axis=2)
    out = {"w_in": _col_slabs(d_in), "mla_w_uq": _col_slabs(d_uq.reshape(Q_RANK, -1)),
           "mla_w_ukv": _col_slabs(d_ukv.reshape(KV_RANK, -1))}
    out.update({n: d_car[n] for n in COLS + ROWS})
    return out


BIG = ("w_in", "w_out", "ffn_w1", "ffn_w3", "ffn_w2", "mla_w_uq", "mla_w_ukv")
SMALL_SHARED = ("norm1_g", "gla_onorm_g", "conf_dw_b", "conf_ln_g", "conf_ln_b", "mla_q_norm_g", "mla_kv_norm_g",
                "norm2_g")
SMALL_SHARDED = ("gla_fg_up", "gla_fg_b", "conf_dw", "sc_dw")
WEIGHTS = ("c_ctx", "norm1_g", "w_mod", "b_mod", "w_in", "gla_fg_up", "gla_fg_b", "gla_onorm_g", "conf_dw",
           "conf_dw_b", "conf_ln_g", "conf_ln_b", "sc_dw", "mla_q_norm_g", "mla_kv_norm_g", "mla_w_uq", "mla_w_ukv",
           "w_out", "norm2_g", "ffn_w1", "ffn_w3", "ffn_w2", "final_norm_g")


def _gather_last(pieces8):
    moved = jnp.moveaxis(pieces8, 0, -2)
    return moved.reshape(moved.shape[:-2] + (-1,))


def _sum_devices(x8, name):
    shape = x8.shape[1:]
    return _sum_leading(x8.reshape(N_DEV, -1, shape[-1]), name).reshape(shape)


def _step(w, m, v, x, c, ctx, loss_target):
    depth = w["norm1_g"].shape[0]
    seq, D = x.shape[1], x.shape[2]
    n_ctx = ctx.shape[1]
    me = 4 * lax.axis_index("x") + 2 * lax.axis_index("y") + lax.axis_index("c")

    G = dict(zip(BIG, _all_gather([w[n].astype(BF16) for n in BIG], "gather_weights")))
    small8 = _all_gather([c] + [w[n] for n in SMALL_SHARDED], "gather_small")
    c_all = small8[0].reshape(N_DEV, D)
    small_full = {n: _gather_last(g) for n, g in zip(SMALL_SHARDED, small8[1:])}

    rows = jnp.concatenate([c_all, w["c_ctx"][None], jnp.zeros((16 - N_DEV - 1, D), F32)])
    (act,) = _rowop(_f_silu, "silu")((rows,), (), ())
    n_mod = w["w_mod"].shape[2]
    b_mine = lax.dynamic_slice_in_dim(w["b_mod"], me * n_mod, n_mod, axis=1)
    mod_cols = [_rowop(_f_add_bias, "add_bias")((_matmul(act, w["w_mod"][i], "nn", "mod_fwd"),), (),
                                                (_shared(b_mine[i]),))[0] for i in range(depth)]
    (mods8,) = _all_gather([jnp.stack(mod_cols)], "gather_mod")
    mods = jnp.moveaxis(mods8, 0, 2).reshape(depth, 16, N_DEV * n_mod)
    mods_l = [lax.dynamic_index_in_dim(mods[i], me, 0, keepdims=False) for i in range(depth)]
    mods_c = [mods[i, N_DEV] for i in range(depth)]

    smalls, plains, cars = [], [], []
    for i in range(depth):
        p = {n: w[n][i] for n in SMALL_SHARED}
        p.update({n: small_full[n][i] for n in SMALL_SHARDED})
        smalls.append(p)
        plains.append(_plain_weights(G, i, D))
        car = {n: lax.empty(a.shape, F32) for n, a in plains[i].items()}
        car.update({n: lax.empty((N_DEV,) + G[n].shape[2:], F32) for n in COLS + ROWS})
        cars.append(car)
    t0 = jnp.concatenate([ctx[0], x[0]], axis=0)
    t_out, vjp = jax.vjp(lambda t, ps, cs, ml, mc: _trunk(t, ps, cs, plains, G, ml, mc, n_ctx, seq),
                         t0, smalls, cars, mods_l, mods_c)
    loss_lanes, d_out, d_final_g = _loss_call(t_out[n_ctx:], w["final_norm_g"][None], loss_target[0])
    d_t0, d_smalls, d_cars, d_mods_l, d_mods_c = vjp(jnp.concatenate([jnp.zeros((n_ctx, D), F32), d_out], axis=0))
    grad_x = d_t0[n_ctx:][None]

    grads = {n: [] for n in BIG}
    for i in range(depth):
        sharded = _shard_grads(d_cars[i])
        for n, g in zip(BIG, _reduce_scatter([sharded[n] for n in BIG])):
            grads[n].append(g)
    grads = {n: jnp.stack(g) for n, g in grads.items()}

    names = SMALL_SHARED + SMALL_SHARDED
    d_mod = jnp.stack([jnp.stack([d_mods_l[i], d_mods_c[i]]) for i in range(depth)])
    parts = [loss_lanes, d_final_g, d_mod] + [jnp.stack([d_smalls[i][n] for i in range(depth)]) for n in names]
    parts8 = _all_gather(parts, "gather_partials")
    summed = [_sum_devices(p8, "sum_partials") for p8 in parts8]
    (loss_row,) = _rowop(lambda a: (jnp.sum(a, axis=-1, keepdims=True) + jnp.zeros_like(a),), "loss_sum")(
        (summed[0],), (), ())
    loss = loss_row[0, 0]
    grads["final_norm_g"] = summed[1].reshape(D)
    for n, g in zip(names, summed[3:]):
        if n in SMALL_SHARDED:
            g = lax.dynamic_slice_in_dim(g, me * w[n].shape[-1], w[n].shape[-1], axis=g.ndim - 1)
        grads[n] = g

    d_mod8, d_mod_c = parts8[2], summed[2][:, 1]
    grads["b_mod"] = _rowop(lambda a, b: (a + b,), "add")((summed[2][:, 0], d_mod_c), (), ())[0]
    d_rows = jnp.concatenate([jnp.moveaxis(d_mod8[:, :, 0], 0, 1), d_mod_c[:, None],
                              jnp.zeros((depth, 16 - N_DEV - 1, 6 * D), F32)], axis=1)
    d_rows = lax.dynamic_slice_in_dim(d_rows, me * n_mod, n_mod, axis=2)
    grads["w_mod"] = jnp.stack([_matmul(act, d_rows[i], "tn", "mod_dw") for i in range(depth)])
    d_act = _matmul(d_rows.transpose(1, 0, 2).reshape(16, depth * n_mod),
                    w["w_mod"].transpose(1, 0, 2).reshape(D, depth * n_mod), "nt", "mod_dact")
    (d_act8,) = _all_gather([d_act], "gather_dact")
    (d_rows_in,) = _rowop(_f_mul_silu_grad, "silu_grad")((_sum_devices(d_act8, "sum_dact"), rows), (), ())
    grads["c_ctx"] = d_rows_in[N_DEV]

    outs = {n: _adamw(w[n], grads[n], m[n], v[n]) for n in WEIGHTS}
    return (loss, grad_x, *[grads[n] for n in WEIGHTS], *[outs[n][0] for n in WEIGHTS],
            *[outs[n][1] for n in WEIGHTS], *[outs[n][2] for n in WEIGHTS])


def kernel(x, c, ctx, c_ctx, norm1_g, w_mod, b_mod, w_in, gla_fg_up, gla_fg_b, gla_onorm_g, conf_dw, conf_dw_b, conf_ln_g, conf_ln_b, sc_dw, mla_q_norm_g, mla_kv_norm_g, mla_w_uq, mla_w_ukv, w_out, norm2_g, ffn_w1, ffn_w3, ffn_w2, final_norm_g, loss_target, m_c_ctx, m_norm1_g, m_w_mod, m_b_mod, m_w_in, m_gla_fg_up, m_gla_fg_b, m_gla_onorm_g, m_conf_dw, m_conf_dw_b, m_conf_ln_g, m_conf_ln_b, m_sc_dw, m_mla_q_norm_g, m_mla_kv_norm_g, m_mla_w_uq, m_mla_w_ukv, m_w_out, m_norm2_g, m_ffn_w1, m_ffn_w3, m_ffn_w2, m_final_norm_g, v_c_ctx, v_norm1_g, v_w_mod, v_b_mod, v_w_in, v_gla_fg_up, v_gla_fg_b, v_gla_onorm_g, v_conf_dw, v_conf_dw_b, v_conf_ln_g, v_conf_ln_b, v_sc_dw, v_mla_q_norm_g, v_mla_kv_norm_g, v_mla_w_uq, v_mla_w_ukv, v_w_out, v_norm2_g, v_ffn_w1, v_ffn_w3, v_ffn_w2, v_final_norm_g):
    w = dict(c_ctx=c_ctx, norm1_g=norm1_g, w_mod=w_mod, b_mod=b_mod, w_in=w_in, gla_fg_up=gla_fg_up, gla_fg_b=gla_fg_b, gla_onorm_g=gla_onorm_g, conf_dw=conf_dw, conf_dw_b=conf_dw_b, conf_ln_g=conf_ln_g, conf_ln_b=conf_ln_b, sc_dw=sc_dw, mla_q_norm_g=mla_q_norm_g, mla_kv_norm_g=mla_kv_norm_g, mla_w_uq=mla_w_uq, mla_w_ukv=mla_w_ukv, w_out=w_out, norm2_g=norm2_g, ffn_w1=ffn_w1, ffn_w3=ffn_w3, ffn_w2=ffn_w2, final_norm_g=final_norm_g)
    m = dict(c_ctx=m_c_ctx, norm1_g=m_norm1_g, w_mod=m_w_mod, b_mod=m_b_mod, w_in=m_w_in, gla_fg_up=m_gla_fg_up, gla_fg_b=m_gla_fg_b, gla_onorm_g=m_gla_onorm_g, conf_dw=m_conf_dw, conf_dw_b=m_conf_dw_b, conf_ln_g=m_conf_ln_g, conf_ln_b=m_conf_ln_b, sc_dw=m_sc_dw, mla_q_norm_g=m_mla_q_norm_g, mla_kv_norm_g=m_mla_kv_norm_g, mla_w_uq=m_mla_w_uq, mla_w_ukv=m_mla_w_ukv, w_out=m_w_out, norm2_g=m_norm2_g, ffn_w1=m_ffn_w1, ffn_w3=m_ffn_w3, ffn_w2=m_ffn_w2, final_norm_g=m_final_norm_g)
    v = dict(c_ctx=v_c_ctx, norm1_g=v_norm1_g, w_mod=v_w_mod, b_mod=v_b_mod, w_in=v_w_in, gla_fg_up=v_gla_fg_up, gla_fg_b=v_gla_fg_b, gla_onorm_g=v_gla_onorm_g, conf_dw=v_conf_dw, conf_dw_b=v_conf_dw_b, conf_ln_g=v_conf_ln_g, conf_ln_b=v_conf_ln_b, sc_dw=v_sc_dw, mla_q_norm_g=v_mla_q_norm_g, mla_kv_norm_g=v_mla_kv_norm_g, mla_w_uq=v_mla_w_uq, mla_w_ukv=v_mla_w_ukv, w_out=v_w_out, norm2_g=v_norm2_g, ffn_w1=v_ffn_w1, ffn_w3=v_ffn_w3, ffn_w2=v_ffn_w2, final_norm_g=v_final_norm_g)
    return _step(w, m, v, x, c, ctx, loss_target)
```

```python
import functools
import math

import jax
import jax.numpy as jnp
from jax import lax
from jax.experimental import pallas as pl
from jax.experimental.pallas import tpu as pltpu

F32 = jnp.float32
BF16 = jnp.bfloat16
MESH = pl.DeviceIdType.MESH
N_DEV = 8

EPS = 1e-6
GRID_W = 64
HEADS = 4
HEAD_W = 128
MIX_W = HEADS * HEAD_W
GATE_RANK = 16
GATE_NORM = 16.0
GLA_CHUNK = 128
CONF_K = 31
SC_K = 3
Q_RANK = 384
KV_RANK = 128
ROPE = 64
ROPE_FREQS = 16
ROPE_BASE = 10000.0
ATT_SCALE = (HEAD_W + ROPE) ** -0.5
CONV_HALO = 16

ADAM_LR = 0.001
ADAM_B1 = 0.9
ADAM_B2 = 0.999
ADAM_EPS = 1e-08
ADAM_WD = 0.01
ADAM_STEP = 10

LANES = 128
VMEM_LIMIT = 56 * 2 ** 20
ROW_BLOCK_BYTES = 10 * 2 ** 20

Z_Q, Z_K, Z_V, Z_G, Z_LR, Z_A, Z_GATE, Z_BG, Z_CG, Z_H, Z_CQ, Z_CKV, Z_KR, Z_END = (
    0, 512, 1024, 1536, 2048, 2176, 2688, 3200, 3712, 4224, 4736, 5120, 5248, 5376)
IN_W = 5216


def _pcall(body, **kw):
    return pl.pallas_call(body, **kw)


def _params(sem=None):
    return pltpu.CompilerParams(dimension_semantics=sem, vmem_limit_bytes=VMEM_LIMIT)


def _pick(dim, cap, mult):
    d = (min(cap, dim) // mult) * mult
    while d >= mult:
        if dim % d == 0:
            return d
        d -= mult
    return dim


def _mm_call(name, a, b, out_shape, grid, a_spec, b_spec, o_spec, dims, k_axis=None, once_axis=None, out_dtype=F32):
    a_blk = tuple(d for d in a_spec.block_shape if d is not None)
    o_blk = tuple(d for d in o_spec.block_shape if d is not None)
    if a.dtype == BF16:
        once_axis = None
    scratch = ([pltpu.VMEM((math.prod(o_blk[:-1]), o_blk[-1]), F32)] if k_axis is not None else []) + (
        [pltpu.VMEM(a_blk, BF16)] if once_axis is not None else [])
    nk = grid[k_axis] if k_axis is not None else 1

    def body(a_ref, b_ref, o_ref, *scr):
        if once_axis is not None:
            a_bf = scr[-1]

            @pl.when(pl.program_id(once_axis) == 0)
            def _():
                a_bf[...] = a_ref[...].astype(BF16)

            av = a_bf[...]
        else:
            av = a_ref[...].astype(BF16)
        bv = b_ref[...].astype(BF16)
        if bv.ndim == 3:
            bv = bv.reshape(-1, bv.shape[-1])
        prod = lax.dot_general(av, bv, dims, preferred_element_type=F32)
        if k_axis is None:
            o_ref[...] = prod.astype(o_ref.dtype).reshape(o_ref.shape)
        else:
            acc, k = scr[0], pl.program_id(k_axis)

            @pl.when(k == 0)
            def _():
                acc[...] = prod

            @pl.when(k != 0)
            def _():
                acc[...] += prod

            @pl.when(k == nk - 1)
            def _():
                o_ref[...] = acc[...].astype(o_ref.dtype).reshape(o_ref.shape)

    return _pcall(
        body, name=name, grid=grid, in_specs=[a_spec, b_spec], out_specs=o_spec,
        out_shape=jax.ShapeDtypeStruct(out_shape, out_dtype), scratch_shapes=scratch,
        compiler_params=_params(("arbitrary",) * len(grid)),
    )(a, b)


NN = (((1,), (0,)), ((), ()))
NT = (((1,), (1,)), ((), ()))
TN = (((0,), (0,)), ((), ()))


def _matmul(a, b, mode, name, out_dtype=F32):
    if mode == "nn":
        (M, K), (_, N) = a.shape, b.shape
    elif mode == "nt":
        (M, K), (N, _) = a.shape, b.shape
    else:
        (K, M), (_, N) = a.shape, b.shape
    if mode == "nn":
        tm, tn, tk = _pick(M, 1088, 16), _pick(N, 768, LANES), _pick(K, 2048, LANES)
    elif mode == "nt":
        tm, tn, tk = _pick(M, 1088, 16), _pick(N, 2048, LANES), _pick(K, 1024, LANES)
    else:
        tm, tn, tk = _pick(M, 2048, LANES), _pick(N, 768, LANES), _pick(K, 1088, 16)
    if mode == "nn":
        a_spec = pl.BlockSpec((tm, tk), lambda i, j, k: (i, k))
        b_spec = pl.BlockSpec((tk, tn), lambda i, j, k: (k, j))
    elif mode == "nt":
        a_spec = pl.BlockSpec((tm, tk), lambda i, j, k: (i, k))
        b_spec = pl.BlockSpec((tn, tk), lambda i, j, k: (j, k))
    else:
        a_spec = pl.BlockSpec((tk, tm), lambda i, j, k: (k, i))
        b_spec = pl.BlockSpec((tk, tn), lambda i, j, k: (k, j))
    return _mm_call(name, a, b, (M, N), (M // tm, N // tn, K // tk), a_spec, b_spec,
                    pl.BlockSpec((tm, tn), lambda i, j, k: (i, j)), {"nn": NN, "nt": NT, "tn": TN}[mode], k_axis=2,
                    out_dtype=out_dtype)


@jax.custom_vjp
def mm(a, w, carrier):
    return _matmul(a, w, "nn", "mm_fwd")


def _mm_fwd(a, w, carrier):
    return _matmul(a, w, "nn", "mm_fwd"), (a, w)


def _mm_bwd(res, dc):
    a, w = res
    return _matmul(dc, w, "nt", "mm_da", a.dtype), jnp.zeros_like(w), _matmul(a, dc, "tn", "mm_dw")


mm.defvjp(_mm_fwd, _mm_bwd)


def _make_mm_cols(layer):
    def forward(a, G):
        (M, K), n = a.shape, G.shape[3]
        tm = _pick(M, 1088, 16)
        return _mm_call("mmc_fwd", a, G, (N_DEV, M, n), (M // tm, N_DEV),
                        pl.BlockSpec((tm, K), lambda i, d: (i, 0)),
                        pl.BlockSpec((None, None, K, n), lambda i, d: (d, layer, 0, 0)),
                        pl.BlockSpec((None, tm, n), lambda i, d: (d, i, 0)), NN, once_axis=1, out_dtype=BF16)

    def grad_a(do, G, dtype):
        (_, M, n), K = do.shape, G.shape[2]
        tm, tn = _pick(M, 1088, 16), _pick(K, 1024, LANES)
        return _mm_call("mmc_da", do, G, (M, K), (M // tm, K // tn, N_DEV),
                        pl.BlockSpec((None, tm, n), lambda i, j, d: (d, i, 0)),
                        pl.BlockSpec((None, None, tn, n), lambda i, j, d: (d, layer, j, 0)),
                        pl.BlockSpec((tm, tn), lambda i, j, d: (i, j)), NT, k_axis=2, out_dtype=dtype)

    def grad_w(a, do):
        (M, K), n = a.shape, do.shape[2]
        tm, tk = _pick(K, 2048, LANES), _pick(M, 1088, 16)
        return _mm_call("mmc_dw", a, do, (N_DEV, K, n), (N_DEV, K // tm, M // tk),
                        pl.BlockSpec((tk, tm), lambda d, i, k: (k, i)),
                        pl.BlockSpec((None, tk, n), lambda d, i, k: (d, k, 0)),
                        pl.BlockSpec((None, tm, n), lambda d, i, k: (d, i, 0)), TN, k_axis=2)

    @jax.custom_vjp
    def f(a, G, carrier):
        return forward(a, G)

    def f_fwd(a, G, carrier):
        return forward(a, G), (a, G)

    def f_bwd(res, do):
        a, G = res
        return grad_a(do, G, a.dtype), jnp.zeros_like(G), grad_w(a, do)

    f.defvjp(f_fwd, f_bwd)
    return f


def _make_mm_rows(layer):
    def forward(a, G):
        M, (r, N) = a.shape[-2], G.shape[2:]
        tm, tn = _pick(M, 1088, 16), _pick(N, 1024, LANES)
        if a.ndim == 2:
            return _mm_call("mmr2_fwd", a, G, (M, N), (M // tm, N // tn),
                            pl.BlockSpec((tm, N_DEV * r), lambda i, j: (i, 0)),
                            pl.BlockSpec((N_DEV, None, r, tn), lambda i, j: (0, layer, 0, j)),
                            pl.BlockSpec((tm, tn), lambda i, j: (i, j)), NN)
        return _mm_call("mmr_fwd", a, G, (M, N), (M // tm, N // tn, N_DEV),
                        pl.BlockSpec((None, tm, r), lambda i, j, d: (d, i, 0)),
                        pl.BlockSpec((None, None, r, tn), lambda i, j, d: (d, layer, 0, j)),
                        pl.BlockSpec((tm, tn), lambda i, j, d: (i, j)), NN, k_axis=2)

    def grad_a(dc, G, like):
        (M, N), r = dc.shape, G.shape[2]
        tm = _pick(M, 1088, 16)
        if like.ndim == 2:
            return _mm_call("mmr2_da", dc, G, like.shape, (M // tm,),
                            pl.BlockSpec((tm, N), lambda i: (i, 0)),
                            pl.BlockSpec((N_DEV, None, r, N), lambda i: (0, layer, 0, 0)),
                            pl.BlockSpec((tm, N_DEV * r), lambda i: (i, 0)), NT, out_dtype=like.dtype)
        return _mm_call("mmr_da", dc, G, like.shape, (M // tm, N_DEV),
                        pl.BlockSpec((tm, N), lambda i, d: (i, 0)),
                        pl.BlockSpec((None, None, r, N), lambda i, d: (d, layer, 0, 0)),
                        pl.BlockSpec((None, tm, r), lambda i, d: (d, i, 0)), NT, once_axis=1, out_dtype=like.dtype)

    def grad_w(a, dc):
        (M, N), tk = dc.shape, _pick(dc.shape[0], 1088, 16)
        if a.ndim == 2:
            r, tn = a.shape[1] // N_DEV, _pick(N, 1024, LANES)
            return _mm_call("mmr2_dw", a, dc, (N_DEV, r, N), (N // tn, M // tk),
                            pl.BlockSpec((tk, N_DEV * r), lambda j, k: (k, 0)),
                            pl.BlockSpec((tk, tn), lambda j, k: (k, j)),
                            pl.BlockSpec((N_DEV, r, tn), lambda j, k: (0, 0, j)), TN, k_axis=1)
        r, tn = a.shape[2], _pick(N, 2048, LANES)
        return _mm_call("mmr_dw", a, dc, (N_DEV, r, N), (N_DEV, N // tn, M // tk),
                        pl.BlockSpec((None, tk, r), lambda d, j, k: (d, k, 0)),
                        pl.BlockSpec((tk, tn), lambda d, j, k: (k, j)),
                        pl.BlockSpec((None, r, tn), lambda d, j, k: (d, 0, j)), TN, k_axis=2)

    @jax.custom_vjp
    def f(a, G, carrier):
        return forward(a, G)

    def f_fwd(a, G, carrier):
        return forward(a, G), (a, G)

    def f_bwd(res, dc):
        a, G = res
        dc = dc.astype(BF16)
        return grad_a(dc, G, a), jnp.zeros_like(G), grad_w(a, dc)

    f.defvjp(f_fwd, f_bwd)
    return f


@functools.partial(jax.custom_vjp, nondiff_argnums=(1,))
def _split_cols(z, bounds):
    return tuple(z[:, a:b] for a, b in zip(bounds[:-1], bounds[1:]))


def _split_cols_fwd(z, bounds):
    return _split_cols(z, bounds), None


def _split_cols_bwd(bounds, _, cts):
    return (jnp.concatenate(cts, axis=1),)


_split_cols.defvjp(_split_cols_fwd, _split_cols_bwd)


def _rowop(fn, name, n_ctx=0, tile=256, lane_block=None, out_dtypes=None):
    def geometry(rows):
        L, w0 = rows[0].shape
        nj = w0 // lane_block if lane_block else 1
        width = 3 * sum(lane_block or r.shape[1] for r in rows)
        cap = max(16, ROW_BLOCK_BYTES // (4 * width) // 16 * 16)
        tl = _pick(math.gcd(L, n_ctx) if n_ctx else L, min(tile, cap), 16)
        return L, tl, n_ctx // tl, nj

    def block_w(x):
        return lane_block or x.shape[1]

    def row_spec(tl, x):
        if lane_block and x.shape[1] != lane_block:
            return pl.BlockSpec((tl, lane_block), lambda i, j: (i, j))
        return pl.BlockSpec((tl, block_w(x)), lambda i, j: (i, 0))

    def param_spec(p, nct):
        s, r, w = p.shape
        if s == 1:
            return pl.BlockSpec((1, r, w), lambda i, j: (0, 0, 0))
        return pl.BlockSpec((1, r, w), lambda i, j: (jnp.where(i < nct, 0, 1), 0, 0))

    def forward(rows, consts, params):
        L, tl, nct, nj = geometry(rows)
        nr, nc, npar = len(rows), len(consts), len(params)
        outs = jax.eval_shape(
            lambda: fn(*[jnp.zeros((tl, block_w(r)), F32) for r in rows + consts],
                       *[jnp.zeros(p.shape[1:], F32) for p in params]))

        def body(*refs):
            ins = [r[...].astype(F32) for r in refs[:nr + nc]] + [r[0] for r in refs[nr + nc:nr + nc + npar]]
            for o_ref, o in zip(refs[nr + nc + npar:], fn(*ins)):
                o_ref[...] = o.astype(o_ref.dtype)

        return _pcall(
            body, name=name + "_fwd", grid=(L // tl, nj),
            in_specs=[row_spec(tl, r) for r in rows + consts] + [param_spec(p, nct) for p in params],
            out_specs=[pl.BlockSpec((tl, o.shape[1]), lambda i, j: (i, j)) for o in outs],
            out_shape=[jax.ShapeDtypeStruct((L, o.shape[1] * nj), dt)
                       for o, dt in zip(outs, out_dtypes or (F32,) * len(outs))],
            compiler_params=_params(("parallel", "parallel")),
        )(*rows, *consts, *params)

    def backward(rows, consts, params, cts):
        L, tl, nct, nj = geometry(rows)
        nr, nc, npar, no = len(rows), len(consts), len(params), len(cts)

        def body(*refs):
            i, j = pl.program_id(0), pl.program_id(1)
            rv = [r[...].astype(F32) for r in refs[:nr]]
            cv = [r[...] for r in refs[nr:nr + nc]]
            pv = [r[0] for r in refs[nr + nc:nr + nc + npar]]
            ct = tuple(r[...].astype(F32) for r in refs[nr + nc + npar:nr + nc + npar + no])
            out_refs = refs[nr + nc + npar + no:]
            _, vjp = jax.vjp(lambda *d: tuple(fn(*d[:nr], *cv, *d[nr:])), *rv, *pv)
            grads = vjp(ct)
            for ref, g in zip(out_refs[:nr], grads[:nr]):
                ref[...] = g.astype(ref.dtype)
            for ref, g, p in zip(out_refs[nr:], grads[nr:], params):
                first_row = (i == 0) if (p.shape[0] == 1 or nct == 0) else ((i == 0) | (i == nct))
                first = jnp.logical_and(first_row, j == 0)

                @pl.when(first)
                def _():
                    ref[0] = g

                @pl.when(jnp.logical_not(first))
                def _():
                    ref[0] += g

        outs = _pcall(
            body, name=name + "_bwd", grid=(L // tl, nj),
            in_specs=[row_spec(tl, r) for r in rows + consts] + [param_spec(p, nct) for p in params]
            + [pl.BlockSpec((tl, c.shape[1] // nj), lambda i, j: (i, j)) for c in cts],
            out_specs=[row_spec(tl, r) for r in rows] + [param_spec(p, nct) for p in params],
            out_shape=[jax.ShapeDtypeStruct(r.shape, r.dtype) for r in rows]
            + [jax.ShapeDtypeStruct(p.shape, F32) for p in params],
            compiler_params=_params(("arbitrary", "arbitrary")),
        )(*rows, *consts, *params, *cts)
        return tuple(outs[:nr]), tuple(outs[nr:])

    @jax.custom_vjp
    def op(rows, consts, params):
        return tuple(forward(rows, consts, params))

    def op_fwd(rows, consts, params):
        return tuple(forward(rows, consts, params)), (rows, consts, params)

    def op_bwd(res, cts):
        rows, consts, params = res
        d_rows, d_params = backward(rows, consts, params, tuple(cts))
        return d_rows, tuple(jnp.zeros_like(c) for c in consts), d_params

    op.defvjp(op_fwd, op_bwd)
    return op


def _sigmoid(x):
    return 1.0 / (1.0 + jnp.exp(-x))


def _silu(x):
    return x * _sigmoid(x)


def _log_sigmoid(x):
    return jnp.minimum(x, 0.0) - jnp.log(1.0 + jnp.exp(-jnp.abs(x)))


def _rms(x, g):
    return x * lax.rsqrt(jnp.mean(x * x, axis=-1, keepdims=True) + EPS) * g


def _f_norm_mod(x, g, sc, sh):
    return (_rms(x, g) * (1.0 + sc) + sh,)


def _f_resid_norm_mod(x, o, gate, g, sc, sh):
    x1 = x + gate * o
    return x1, _rms(x1, g) * (1.0 + sc) + sh


def _f_resid(x, o, gate):
    return (x + gate * o,)


def _f_swiglu(a1, a3):
    return (_silu(a1) * a3,)


def _f_gla_gate(lr, up_f, up_b, b_f, b_b):
    dot = functools.partial(jnp.dot, preferred_element_type=F32)
    return (_log_sigmoid(dot(lr, up_f) + b_f) / GATE_NORM, _log_sigmoid(dot(lr, up_b) + b_b) / GATE_NORM)


def _f_gla_finish(o_f, o_b, gate, g):
    return (_rms(o_f + o_b, g) * _silu(gate),)


def _f_glu(a, gate):
    return (a * _sigmoid(gate),)


def _f_ln_silu(u, dw_b, g, b):
    u = u + dw_b
    xc = u - jnp.mean(u, axis=-1, keepdims=True)
    y = xc * lax.rsqrt(jnp.mean(xc * xc, axis=-1, keepdims=True) + EPS)
    return (_silu(y * g + b),)


def _f_mul(a, b):
    return (a * b,)


def _f_rms(x, g):
    return (_rms(x, g),)


def _f_rope(t, cos, sin):
    w = t.shape[1]
    r = lax.broadcasted_iota(jnp.int32, (w, w), 0)
    c = lax.broadcasted_iota(jnp.int32, (w, w), 1)
    perm = (jnp.bitwise_xor(r, ROPE_FREQS) == c).astype(F32)
    partner = jnp.dot(t, perm, precision=lax.Precision.HIGHEST, preferred_element_type=F32)
    return (t * cos + partner * sin,)


def _f_silu(x):
    return (_silu(x),)


def _f_add_bias(x, b):
    return (x + b,)


def _f_mul_silu_grad(d, x):
    _, vjp = jax.vjp(_silu, x)
    return (vjp(d)[0],)


def _conv_geometry(u, n_ctx):
    L, C = u.shape
    tl = _pick(math.gcd(L, n_ctx), 256, 8)
    return L, C, tl, n_ctx // tl, L // tl


def _conv_specs(tl, nt):
    prev = pl.BlockSpec((tl, LANES), lambda c, i: (jnp.maximum(i - 1, 0), c))
    cur = pl.BlockSpec((tl, LANES), lambda c, i: (i, c))
    nxt = pl.BlockSpec((tl, LANES), lambda c, i: (jnp.minimum(i + 1, nt - 1), c))
    return prev, cur, nxt


def _conv_window(prev_ref, cur_ref, next_ref, tl, nct, nt):
    i = pl.program_id(1)
    has_prev = jnp.logical_and(i != 0, i != nct)
    has_next = jnp.logical_and(i != nct - 1, i != nt - 1)
    prev = jnp.where(has_prev, prev_ref[tl - CONV_HALO:tl, :], 0.0)
    nxt = jnp.where(has_next, next_ref[0:CONV_HALO, :], 0.0)
    return jnp.concatenate([prev, cur_ref[...], nxt], axis=0)


def _shifted(window, off, tl):
    n = window.shape[0]
    if off == 0:
        return window[0:tl]
    return pltpu.roll(window, n - off, 0)[0:tl]


def _conv_apply(u, w, n_ctx, flip, name):
    L, C, tl, nct, nt = _conv_geometry(u, n_ctx)
    K = w.shape[0]
    pad = (K - 1) // 2
    prev, cur, nxt = _conv_specs(tl, nt)

    def body(p_ref, c_ref, n_ref, w_ref, o_ref):
        win = _conv_window(p_ref, c_ref, n_ref, tl, nct, nt)
        acc = jnp.zeros((tl, LANES), F32)
        for k in range(K):
            kk = K - 1 - k if flip else k
            acc = acc + _shifted(win, CONV_HALO - pad + k, tl) * w_ref[kk:kk + 1, :]
        o_ref[...] = acc

    return _pcall(
        body, name=name, grid=(C // LANES, nt),
        in_specs=[prev, cur, nxt, pl.BlockSpec((K, LANES), lambda c, i: (0, c))],
        out_specs=cur, out_shape=jax.ShapeDtypeStruct((L, C), F32),
        compiler_params=_params(("parallel", "parallel")),
    )(u, u, u, w)


def _conv_dw(u, dy, K, n_ctx, name):
    L, C, tl, nct, nt = _conv_geometry(u, n_ctx)
    pad = (K - 1) // 2
    prev, cur, nxt = _conv_specs(tl, nt)

    def body(p_ref, c_ref, n_ref, dy_ref, dw_ref):
        i = pl.program_id(1)

        @pl.when(i == 0)
        def _():
            dw_ref[...] = jnp.zeros_like(dw_ref)

        win = _conv_window(p_ref, c_ref, n_ref, tl, nct, nt)
        dy_t = dy_ref[...]
        for k in range(K):
            dw_ref[k:k + 1, :] += jnp.sum(_shifted(win, CONV_HALO - pad + k, tl) * dy_t, axis=0, keepdims=True)

    return _pcall(
        body, name=name, grid=(C // LANES, nt),
        in_specs=[prev, cur, nxt, cur],
        out_specs=pl.BlockSpec((K, LANES), lambda c, i: (0, c)),
        out_shape=jax.ShapeDtypeStruct((K, C), F32),
        compiler_params=_params(("parallel", "arbitrary")),
    )(u, u, u, dy)


def _make_conv(n_ctx):
    @jax.custom_vjp
    def conv(u, w):
        return _conv_apply(u, w, n_ctx, False, "conv_fwd")

    def conv_fwd(u, w):
        return _conv_apply(u, w, n_ctx, False, "conv_fwd"), (u, w)

    def conv_bwd(res, dy):
        u, w = res
        return _conv_apply(dy, w, n_ctx, True, "conv_du"), _conv_dw(u, dy, w.shape[0], n_ctx, "conv_dw")

    conv.defvjp(conv_fwd, conv_bwd)
    return conv


def _gla_chunk(q, k, v, g, st, reverse):
    C = q.shape[0]
    r = lax.broadcasted_iota(jnp.int32, (C, C), 0)
    c = lax.broadcasted_iota(jnp.int32, (C, C), 1)
    seen = (r <= c) if reverse else (r >= c)
    dot = functools.partial(lax.dot_general, preferred_element_type=F32, precision=lax.Precision.DEFAULT)
    bcum = lax.dot_general(seen.astype(F32), g, (((1,), (0,)), ((), ())), preferred_element_type=F32,
                           precision=lax.Precision.HIGHEST)
    total = jnp.sum(g, axis=0, keepdims=True)
    a = q * (HEAD_W ** -0.5) * jnp.exp(bcum)
    scores = jnp.where(seen, dot(a, k * jnp.exp(-bcum), (((1,), (1,)), ((), ()))), 0.0)
    o = dot(a, st, (((1,), (1,)), ((), ()))) + dot(scores, v, (((1,), (0,)), ((), ())))
    st_new = st * jnp.exp(total) + dot(v, k * jnp.exp(total - bcum), (((0,), (0,)), ((), ())))
    return o, st_new


def _gla_order(t, nc, ncc, reverse):
    if not reverse:
        return t
    return jnp.where(t < ncc, ncc - 1 - t, ncc + nc - 1 - t)


def _gla_fwd_call(q, k, v, g, n_ctx, reverse):
    L = q.shape[0]
    C = GLA_CHUNK
    nc, ncc = L // C, n_ctx // C
    spec = pl.BlockSpec((C, HEAD_W), lambda h, t: (_gla_order(t, nc, ncc, reverse), h))

    def body(q_ref, k_ref, v_ref, g_ref, o_ref, s_ref, st):
        @pl.when(pl.program_id(1) == 0)
        def _():
            st[...] = jnp.zeros_like(st)

        s_ref[0, 0] = st[...]
        o, st_new = _gla_chunk(q_ref[...], k_ref[...], v_ref[...], g_ref[...], st[...], reverse)
        o_ref[...] = o
        st[...] = st_new

    return _pcall(
        body, name="gla_fwd", grid=(HEADS, nc), in_specs=[spec] * 4,
        out_specs=[spec, pl.BlockSpec((1, 1, HEAD_W, HEAD_W), lambda h, t: (h, t, 0, 0))],
        out_shape=[jax.ShapeDtypeStruct((L, MIX_W), F32), jax.ShapeDtypeStruct((HEADS, nc, HEAD_W, HEAD_W), F32)],
        scratch_shapes=[pltpu.VMEM((HEAD_W, HEAD_W), F32)],
        compiler_params=_params(("parallel", "arbitrary")),
    )(q, k, v, g)


def _gla_bwd_call(q, k, v, g, states, do, n_ctx, reverse):
    L = q.shape[0]
    C = GLA_CHUNK
    nc, ncc = L // C, n_ctx // C
    spec = pl.BlockSpec((C, HEAD_W), lambda h, t: (_gla_order(nc - 1 - t, nc, ncc, reverse), h))

    def body(q_ref, k_ref, v_ref, g_ref, s_ref, do_ref, dq_ref, dk_ref, dv_ref, dg_ref, dst):
        @pl.when(pl.program_id(1) == 0)
        def _():
            dst[...] = jnp.zeros_like(dst)

        _, vjp = jax.vjp(functools.partial(_gla_chunk, reverse=reverse),
                         q_ref[...], k_ref[...], v_ref[...], g_ref[...], s_ref[0, 0])
        dq, dk, dv, dg, dst_prev = vjp((do_ref[...], dst[...]))
        dq_ref[...] = dq
        dk_ref[...] = dk
        dv_ref[...] = dv
        dg_ref[...] = dg
        dst[...] = dst_prev

    return _pcall(
        body, name="gla_bwd", grid=(HEADS, nc),
        in_specs=[spec] * 4 + [pl.BlockSpec((1, 1, HEAD_W, HEAD_W), lambda h, t: (h, nc - 1 - t, 0, 0)), spec],
        out_specs=[spec] * 4, out_shape=[jax.ShapeDtypeStruct((L, MIX_W), F32)] * 4,
        scratch_shapes=[pltpu.VMEM((HEAD_W, HEAD_W), F32)],
        compiler_params=_params(("parallel", "arbitrary")),
    )(q, k, v, g, states, do)


def _make_gla(n_ctx, reverse):
    @jax.custom_vjp
    def gla(q, k, v, g):
        return _gla_fwd_call(q, k, v, g, n_ctx, reverse)[0]

    def gla_fwd(q, k, v, g):
        o, states = _gla_fwd_call(q, k, v, g, n_ctx, reverse)
        return o, (q, k, v, g, states)

    def gla_bwd(res, do):
        q, k, v, g, states = res
        return tuple(_gla_bwd_call(q, k, v, g, states, do, n_ctx, reverse))

    gla.defvjp(gla_fwd, gla_bwd)
    return gla


def _att_probs(qn, qr, kn, kr, i, nct, n_ctx):
    nt_dims = (((1,), (1,)), ((), ()))
    s = lax.dot_general(qn, kn, nt_dims, preferred_element_type=F32)
    s = (s + lax.dot_general(qr, kr, nt_dims, preferred_element_type=F32)) * ATT_SCALE
    col = lax.broadcasted_iota(jnp.int32, s.shape, 1)
    s = jnp.where(col < jnp.where(i < nct, n_ctx, s.shape[1]), s, -1e30)
    p = jnp.exp(s - jnp.max(s, axis=-1, keepdims=True))
    return p / jnp.sum(p, axis=-1, keepdims=True)


def _att_geometry(qn, n_ctx):
    L = qn.shape[0]
    tq = _pick(math.gcd(L, n_ctx), 256, 8)
    q_spec = pl.BlockSpec((tq, HEAD_W), lambda h, i: (i, h))
    k_spec = pl.BlockSpec((L, HEAD_W), lambda h, i: (0, h))
    kr_spec = pl.BlockSpec((L, HEAD_W), lambda h, i: (0, 0))
    return L, tq, n_ctx // tq, q_spec, k_spec, kr_spec


def _att_fwd_call(qn, qr, kn, kr, v, n_ctx):
    L, tq, nct, q_spec, k_spec, kr_spec = _att_geometry(qn, n_ctx)

    def body(qn_ref, qr_ref, kn_ref, kr_ref, v_ref, o_ref):
        p = _att_probs(qn_ref[...].astype(BF16), qr_ref[...].astype(BF16), kn_ref[...].astype(BF16),
                       kr_ref[...].astype(BF16), pl.program_id(1), nct, n_ctx)
        o_ref[...] = jnp.dot(p.astype(BF16), v_ref[...].astype(BF16), preferred_element_type=F32).astype(BF16)

    return _pcall(
        body, name="att_fwd", grid=(HEADS, L // tq),
        in_specs=[q_spec, q_spec, k_spec, kr_spec, k_spec], out_specs=q_spec,
        out_shape=jax.ShapeDtypeStruct((L, MIX_W), BF16),
        compiler_params=_params(("parallel", "parallel")),
    )(qn, qr, kn, kr, v)


def _att_bwd_call(qn, qr, kn, kr, v, do, n_ctx):
    L, tq, nct, q_spec, k_spec, kr_spec = _att_geometry(qn, n_ctx)
    tn_dims = (((0,), (0,)), ((), ()))

    def body(qn_ref, qr_ref, kn_ref, kr_ref, v_ref, do_ref, dqn_ref, dqr_ref, dkn_ref, dkr_ref, dv_ref):
        h, i = pl.program_id(0), pl.program_id(1)
        qn, qr = qn_ref[...].astype(BF16), qr_ref[...].astype(BF16)
        kn, kr, vv = kn_ref[...].astype(BF16), kr_ref[...].astype(BF16), v_ref[...].astype(BF16)
        do = do_ref[...].astype(BF16)
        p = _att_probs(qn, qr, kn, kr, i, nct, n_ctx)
        dp = lax.dot_general(do, vv, (((1,), (1,)), ((), ())), preferred_element_type=F32)
        ds = (p * (dp - jnp.sum(p * dp, axis=-1, keepdims=True)) * ATT_SCALE).astype(BF16)
        dqn_ref[...] = jnp.dot(ds, kn, preferred_element_type=F32)
        dqr_ref[...] = jnp.dot(ds, kr, preferred_element_type=F32)

        @pl.when(i == 0)
        def _():
            dkn_ref[...] = jnp.zeros_like(dkn_ref)
            dv_ref[...] = jnp.zeros_like(dv_ref)

        @pl.when(jnp.logical_and(i == 0, h == 0))
        def _():
            dkr_ref[...] = jnp.zeros_like(dkr_ref)

        dkn_ref[...] += lax.dot_general(ds, qn, tn_dims, preferred_element_type=F32)
        dkr_ref[...] += lax.dot_general(ds, qr, tn_dims, preferred_element_type=F32)
        dv_ref[...] += lax.dot_general(p.astype(BF16), do, tn_dims, preferred_element_type=F32)

    return _pcall(
        body, name="att_bwd", grid=(HEADS, L // tq),
        in_specs=[q_spec, q_spec, k_spec, kr_spec, k_spec, q_spec],
        out_specs=[q_spec, q_spec, k_spec, kr_spec, k_spec],
        out_shape=[jax.ShapeDtypeStruct((L, MIX_W), F32)] * 3 + [jax.ShapeDtypeStruct((L, HEAD_W), F32),
                                                                 jax.ShapeDtypeStruct((L, MIX_W), F32)],
        compiler_params=_params(("arbitrary", "arbitrary")),
    )(qn, qr, kn, kr, v, do)


def _make_attention(n_ctx):
    @jax.custom_vjp
    def att(qn, qr, kn, kr, v):
        return _att_fwd_call(qn, qr, kn, kr, v, n_ctx)

    def att_fwd(qn, qr, kn, kr, v):
        return _att_fwd_call(qn, qr, kn, kr, v, n_ctx), (qn, qr, kn, kr, v)

    def att_bwd(res, do):
        return tuple(_att_bwd_call(*res, do, n_ctx))

    att.defvjp(att_fwd, att_bwd)
    return att


def _loss_call(x, g, target):
    L, D = x.shape
    tl = _pick(L, 256, 8)

    def f(xv, gv, tv):
        err = _rms(xv, gv) - tv
        return 0.5 * jnp.sum(err * err, axis=0, keepdims=True) / D

    def body(x_ref, g_ref, t_ref, loss_ref, dx_ref, dg_ref):
        i = pl.program_id(0)
        loss, vjp = jax.vjp(lambda xv, gv: f(xv, gv, t_ref[...]), x_ref[...], g_ref[...])
        dx, dg = vjp(jnp.ones_like(loss))
        dx_ref[...] = dx

        @pl.when(i == 0)
        def _():
            loss_ref[...] = loss
            dg_ref[...] = dg

        @pl.when(i != 0)
        def _():
            loss_ref[...] += loss
            dg_ref[...] += dg

    row = pl.BlockSpec((tl, D), lambda i: (i, 0))
    one = pl.BlockSpec((1, D), lambda i: (0, 0))
    return _pcall(
        body, name="loss", grid=(L // tl,), in_specs=[row, one, row], out_specs=[one, row, one],
        out_shape=[jax.ShapeDtypeStruct((1, D), F32), jax.ShapeDtypeStruct((L, D), F32),
                   jax.ShapeDtypeStruct((1, D), F32)],
        compiler_params=_params(("arbitrary",)),
    )(x, g, target)


def _sum_leading(x, name):
    n, R, W = x.shape
    tr = _pick(R, 512, 8)

    def body(x_ref, o_ref):
        acc = x_ref[0]
        for d in range(1, n):
            acc = acc + x_ref[d]
        o_ref[...] = acc

    return _pcall(
        body, name=name, grid=(R // tr,), in_specs=[pl.BlockSpec((n, tr, W), lambda i: (0, i, 0))],
        out_specs=pl.BlockSpec((tr, W), lambda i: (i, 0)), out_shape=jax.ShapeDtypeStruct((R, W), F32),
        compiler_params=_params(("parallel",)),
    )(x)


def _adamw(w, g, m, v):
    shape = w.shape
    W = shape[-1]
    as2d = lambda t: t.reshape(-1, W)
    R = as2d(w).shape[0]
    tr = _pick(R, max(8, (2 ** 17 // W) // 8 * 8), 8)

    def body(w_ref, g_ref, m_ref, v_ref, d_ref, nm_ref, nv_ref):
        gv = g_ref[...]
        m_new = ADAM_B1 * m_ref[...] + (1.0 - ADAM_B1) * gv
        v_new = ADAM_B2 * v_ref[...] + (1.0 - ADAM_B2) * (gv * gv)
        m_hat = m_new / (1.0 - ADAM_B1 ** ADAM_STEP)
        v_hat = v_new / (1.0 - ADAM_B2 ** ADAM_STEP)
        d_ref[...] = -ADAM_LR * (m_hat / (jnp.sqrt(v_hat) + ADAM_EPS) + ADAM_WD * w_ref[...])
        nm_ref[...] = m_new
        nv_ref[...] = v_new

    spec = pl.BlockSpec((tr, W), lambda i: (i, 0))
    outs = _pcall(
        body, name="adamw", grid=(R // tr,), in_specs=[spec] * 4, out_specs=[spec] * 3,
        out_shape=[jax.ShapeDtypeStruct((R, W), F32)] * 3, compiler_params=_params(("parallel",)),
    )(as2d(w), as2d(g), as2d(m), as2d(v))
    return tuple(o.reshape(shape) for o in outs)


HBM = pl.BlockSpec(memory_space=pltpu.HBM)


def _place():
    x, y, c = lax.axis_index("x"), lax.axis_index("y"), lax.axis_index("c")
    return x, y, c, [(1 - x, y), (x, 1 - y), (1 - x, 1 - y)]


def _all_gather(blocks, name):
    n = len(blocks)

    def body(*refs):
        x_refs, out_refs = refs[:n], refs[n:2 * n]
        send_sems, recv_sems, local_sems = refs[2 * n:]
        x, y, c, chips = _place()
        me, sibling = (x, y, c), (x, y, 1 - c)

        def slot(a, px, py, pc):
            return out_refs[a].at[4 * px + 2 * py + pc]

        def copy(a, k, blk, to, src=None):
            return pltpu.make_async_remote_copy(
                src_ref=slot(a, *blk) if src is None else src, dst_ref=slot(a, *blk),
                send_sem=send_sems.at[7 * a + k], recv_sem=recv_sems.at[7 * a + k], device_id=to, device_id_type=MESH)

        mine = [pltpu.make_async_copy(x_refs[a], slot(a, *me), local_sems.at[a]) for a in range(n)]
        for cp in mine:
            cp.start()
        first = []
        for a in range(n):
            first.append(copy(a, 0, me, sibling, src=x_refs[a]))
            first += [copy(a, 1 + j, me, (*chip, c), src=x_refs[a]) for j, chip in enumerate(chips)]
        for cp in first:
            cp.start()
        passed = []
        for j, chip in enumerate(chips):
            for a in range(n):
                copy(a, 1 + j, (*chip, c), me).wait_recv()
                passed.append(copy(a, 4 + j, (*chip, c), sibling))
                passed[-1].start()
        for a in range(n):
            copy(a, 0, sibling, me).wait_recv()
        for j, chip in enumerate(chips):
            for a in range(n):
                copy(a, 4 + j, (*chip, 1 - c), me).wait_recv()
        for cp in first + passed:
            cp.wait_send()
        for cp in mine:
            cp.wait()

    return _pcall(
        body, name=name, out_shape=[jax.ShapeDtypeStruct((N_DEV,) + b.shape, b.dtype) for b in blocks],
        in_specs=[HBM] * n, out_specs=[HBM] * n,
        scratch_shapes=[pltpu.SemaphoreType.DMA((7 * n,)), pltpu.SemaphoreType.DMA((7 * n,)),
                        pltpu.SemaphoreType.DMA((n,))],
    )(*blocks)


def _send_to_sibling(gs, name):
    n = len(gs)

    def body(*refs):
        g_refs, out_refs, (send_sems, recv_sems) = refs[:n], refs[n:2 * n], refs[2 * n:]
        x, y, c, _ = _place()
        copies = [pltpu.make_async_remote_copy(
            src_ref=g_refs[a].at[2 * q + 1 - c], dst_ref=out_refs[a].at[q], send_sem=send_sems.at[4 * a + q],
            recv_sem=recv_sems.at[4 * a + q], device_id=(x, y, 1 - c), device_id_type=MESH)
            for a in range(n) for q in range(4)]
        for cp in copies:
            cp.start()
        for cp in copies:
            cp.wait()

    return _pcall(
        body, name=name, out_shape=[jax.ShapeDtypeStruct((4,) + g.shape[1:], g.dtype) for g in gs],
        in_specs=[HBM] * n, out_specs=[HBM] * n,
        scratch_shapes=[pltpu.SemaphoreType.DMA((4 * n,)), pltpu.SemaphoreType.DMA((4 * n,))],
    )(*gs)


def _send_to_chips(ps, name):
    n = len(ps)

    def body(*refs):
        p_refs, out_refs, (send_sems, recv_sems) = refs[:n], refs[n:2 * n], refs[2 * n:]
        x, y, c, chips = _place()
        copies = [pltpu.make_async_remote_copy(
            src_ref=p_refs[a].at[2 * cx + cy], dst_ref=out_refs[a].at[j], send_sem=send_sems.at[3 * a + j],
            recv_sem=recv_sems.at[3 * a + j], device_id=(cx, cy, c), device_id_type=MESH)
            for a in range(n) for j, (cx, cy) in enumerate(chips)]
        for cp in copies:
            cp.start()
        for cp in copies:
            cp.wait()

    return _pcall(
        body, name=name, out_shape=[jax.ShapeDtypeStruct((3,) + p.shape[1:], p.dtype) for p in ps],
        in_specs=[HBM] * n, out_specs=[HBM] * n,
        scratch_shapes=[pltpu.SemaphoreType.DMA((3 * n,)), pltpu.SemaphoreType.DMA((3 * n,))],
    )(*ps)


def _add_rows(R, W):
    return _pick(R, max(16, 2 ** 19 // W // 16 * 16), 16)


def _add_sibling(g, recv, core):
    _, R, W = g.shape
    tr = _add_rows(R, W)

    def body(core_ref, g_ref, r_ref, o_ref):
        o_ref[...] = (g_ref[...] + r_ref[...]).astype(BF16)

    return _pcall(
        body, name="rs_add_sibling",
        grid_spec=pltpu.PrefetchScalarGridSpec(
            num_scalar_prefetch=1, grid=(4, R // tr),
            in_specs=[pl.BlockSpec((None, tr, W), lambda q, i, core_ref: (2 * q + core_ref[0], i, 0)),
                      pl.BlockSpec((None, tr, W), lambda q, i, core_ref: (q, i, 0))],
            out_specs=pl.BlockSpec((None, tr, W), lambda q, i, core_ref: (q, i, 0))),
        out_shape=jax.ShapeDtypeStruct((4, R, W), BF16), compiler_params=_params(("parallel", "parallel")),
    )(core, g, recv)


def _add_chips(p, recv, chip):
    _, R, W = p.shape
    tr = _add_rows(R, W)

    def body(chip_ref, p_ref, r_ref, o_ref):
        up = lambda t: t.astype(F32)
        o_ref[...] = ((up(p_ref[...]) + up(r_ref[0])) + up(r_ref[1])) + up(r_ref[2])

    return _pcall(
        body, name="rs_add_chips",
        grid_spec=pltpu.PrefetchScalarGridSpec(
            num_scalar_prefetch=1, grid=(R // tr,),
            in_specs=[pl.BlockSpec((None, tr, W), lambda i, chip_ref: (chip_ref[0], i, 0)),
                      pl.BlockSpec((3, tr, W), lambda i, chip_ref: (0, i, 0))],
            out_specs=pl.BlockSpec((tr, W), lambda i, chip_ref: (i, 0))),
        out_shape=jax.ShapeDtypeStruct((R, W), F32), compiler_params=_params(("parallel",)),
    )(chip, p, recv)


def _reduce_scatter(gs):
    x, y, c = lax.axis_index("x"), lax.axis_index("y"), lax.axis_index("c")
    core = jnp.reshape(c, (1,)).astype(jnp.int32)
    chip = jnp.reshape(2 * x + y, (1,)).astype(jnp.int32)
    from_sibling = _send_to_sibling(gs, "rs_sibling")
    chip_sums = [_add_sibling(g, r, core) for g, r in zip(gs, from_sibling)]
    from_chips = _send_to_chips(chip_sums, "rs_chips")
    return [_add_chips(p, r, chip) for p, r in zip(chip_sums, from_chips)]


def _rope_tables(seq, n_ctx):
    rows = seq // GRID_W
    row = jnp.repeat(jnp.arange(rows, dtype=F32), GRID_W)
    col = jnp.tile(jnp.arange(GRID_W, dtype=F32), rows)
    inv = ROPE_BASE ** (-jnp.arange(ROPE_FREQS, dtype=F32) * 2.0 / (ROPE // 2))
    ang_r, ang_c = row[:, None] * inv, col[:, None] * inv
    one, zero = jnp.ones((seq, ROPE), F32), jnp.zeros((seq, ROPE), F32)
    cos = jnp.concatenate([jnp.cos(ang_r), jnp.cos(ang_r), jnp.cos(ang_c), jnp.cos(ang_c), one], axis=1)
    sin = jnp.concatenate([-jnp.sin(ang_r), jnp.sin(ang_r), -jnp.sin(ang_c), jnp.sin(ang_c), zero], axis=1)
    cos = jnp.concatenate([jnp.ones((n_ctx, LANES), F32), cos], axis=0)
    sin = jnp.concatenate([jnp.zeros((n_ctx, LANES), F32), sin], axis=0)
    return cos, sin


def _shared(v):
    return v.reshape((1, 1, -1)) if v.ndim == 1 else v.reshape((1,) + v.shape)


Z_BOUNDS = (Z_Q, Z_K, Z_V, Z_G, Z_LR, Z_A, Z_GATE, Z_BG, Z_CG, Z_H, Z_CQ, Z_CKV, Z_KR, Z_END)
PLAIN = ("w_in", "w_qn", "w_qr", "w_kn", "w_v")
COLS = ("ffn_w1", "ffn_w3")
ROWS = ("w_out", "ffn_w2")


def _layer(t, p, car, plain, G, layer, mod_l, mod_c, n_ctx, tables):
    L, D = t.shape
    seg = lambda i: jnp.stack([mod_c[i * D:(i + 1) * D], mod_l[i * D:(i + 1) * D]]).reshape(2, 1, D)
    sh1, sc1, g1, sh2, sc2, g2 = (seg(i) for i in range(6))
    dense = lambda a, n: mm(a, plain[n], car[n])

    (h,) = _rowop(_f_norm_mod, "norm_mod", n_ctx, out_dtypes=(BF16,))((t,), (), (_shared(p["norm1_g"]), sc1, sh1))
    q, k, v, gate, lr, conf_a, conf_gate, sc_b, sc_c, sc_h, cq, ckv, kr = _split_cols(dense(h, "w_in"), Z_BOUNDS)

    up = p["gla_fg_up"]
    up_f = jnp.pad(up[0], ((0, LANES - GATE_RANK), (0, 0)))
    up_b = jnp.pad(up[1], ((GATE_RANK, LANES - 2 * GATE_RANK), (0, 0)))
    logd_f, logd_b = _rowop(_f_gla_gate, "gla_gate")(
        (lr,), (), (_shared(up_f), _shared(up_b), _shared(p["gla_fg_b"][0]), _shared(p["gla_fg_b"][1])))
    o_f = _make_gla(n_ctx, False)(q, k, v, logd_f)
    o_b = _make_gla(n_ctx, True)(q, k, v, logd_b)
    (gla,) = _rowop(_f_gla_finish, "gla_finish", lane_block=HEAD_W, out_dtypes=(BF16,))(
        (o_f, o_b, gate), (), (_shared(p["gla_onorm_g"]),))

    conv = _make_conv(n_ctx)
    (u,) = _rowop(_f_glu, "glu")((conf_a, conf_gate), (), ())
    (conf,) = _rowop(_f_ln_silu, "ln_silu", out_dtypes=(BF16,))(
        (conv(u, p["conf_dw"]),), (), (_shared(p["conf_dw_b"]), _shared(p["conf_ln_g"]), _shared(p["conf_ln_b"])))

    (ch,) = _rowop(_f_mul, "mul")((sc_c, sc_h), (), ())
    (sconv,) = _rowop(_f_mul, "mul_out", out_dtypes=(BF16,))((sc_b, conv(ch, p["sc_dw"])), (), ())

    (cq,) = _rowop(_f_rms, "rms")((cq,), (), (_shared(p["mla_q_norm_g"]),))
    (ckv,) = _rowop(_f_rms, "rms")((ckv,), (), (_shared(p["mla_kv_norm_g"]),))
    rope = _rowop(_f_rope, "rope", lane_block=LANES)
    (qr,) = rope((dense(cq, "w_qr"),), tables, ())
    (kr,) = rope((kr,), tables, ())
    mla = _make_attention(n_ctx)(dense(cq, "w_qn"), qr, dense(ckv, "w_kn"), kr, dense(ckv, "w_v"))

    o = _make_mm_rows(layer)(jnp.concatenate([gla, conf, sconv, mla], axis=1), G["w_out"], car["w_out"])
    t1, h2 = _rowop(_f_resid_norm_mod, "resid_norm_mod", n_ctx, out_dtypes=(F32, BF16))(
        (t, o), (), (g1, _shared(p["norm2_g"]), sc2, sh2))
    a1 = _make_mm_cols(layer)(h2, G["ffn_w1"], car["ffn_w1"])
    a3 = _make_mm_cols(layer)(h2, G["ffn_w3"], car["ffn_w3"])
    n_ff = a1.shape[2]
    (act,) = _rowop(_f_swiglu, "swiglu", out_dtypes=(BF16,))(
        (a1.reshape(N_DEV * L, n_ff), a3.reshape(N_DEV * L, n_ff)), (), ())
    f = _make_mm_rows(layer)(act.reshape(N_DEV, L, n_ff), G["ffn_w2"], car["ffn_w2"])
    (t2,) = _rowop(_f_resid, "resid", n_ctx)((t1, f), (), (g2,))
    return t2


def _trunk(t, smalls, cars, plains, G, mods_l, mods_c, n_ctx, seq):
    tables = _rope_tables(seq, n_ctx)
    for layer, (p, car, plain, mod_l, mod_c) in enumerate(zip(smalls, cars, plains, mods_l, mods_c)):
        t = _layer(t, p, car, plain, G, layer, mod_l, mod_c, n_ctx, tables)
    return t


def _plain_weights(G, layer, D):
    full = lambda n: jnp.concatenate([G[n][d, layer] for d in range(N_DEV)], axis=1)
    w_in = full("w_in")
    w_in = jnp.concatenate([w_in[:, :Z_LR + 2 * GATE_RANK], jnp.zeros((D, Z_A - Z_LR - 2 * GATE_RANK), BF16),
                            w_in[:, Z_LR + 2 * GATE_RANK:], jnp.zeros((D, Z_END - Z_KR - ROPE), BF16)], axis=1)
    w_uq = full("mla_w_uq").reshape(Q_RANK, HEADS, HEAD_W + ROPE)
    w_ukv = full("mla_w_ukv").reshape(KV_RANK, HEADS, 2 * HEAD_W)
    return {"w_in": w_in,
            "w_qn": w_uq[:, :, :HEAD_W].reshape(Q_RANK, MIX_W),
            "w_qr": jnp.pad(w_uq[:, :, HEAD_W:], ((0, 0), (0, 0), (0, LANES - ROPE))).reshape(Q_RANK, HEADS * LANES),
            "w_kn": w_ukv[:, :, :HEAD_W].reshape(KV_RANK, MIX_W),
            "w_v": w_ukv[:, :, HEAD_W:].reshape(KV_RANK, MIX_W)}


def _col_slabs(full):
    n = full.shape[1] // N_DEV
    return jnp.stack([full[:, d * n:(d + 1) * n] for d in range(N_DEV)])


def _shard_grads(d_car):
    d_in = d_car["w_in"]
    d_in = jnp.concatenate([d_in[:, :Z_LR + 2 * GATE_RANK], d_in[:, Z_A:Z_KR + ROPE]], axis=1)
    by_head = lambda g: g.reshape(g.shape[0], HEADS, -1)
    d_uq = jnp.concatenate([by_head(d_car["w_qn"]), by_head(d_car["w_qr"])[:, :, :ROPE]], axis=2)
    d_ukv = jnp.concatenate([by_head(d_car["w_kn"]), by_head(d_car["w_v"])], axis=2)
    out = {"w_in": _col_slabs(d_in), "mla_w_uq": _col_slabs(d_uq.reshape(Q_RANK, -1)),
           "mla_w_ukv": _col_slabs(d_ukv.reshape(KV_RANK, -1))}
    out.update({n: d_car[n] for n in COLS + ROWS})
    return out


BIG = ("w_in", "w_out", "ffn_w1", "ffn_w3", "ffn_w2", "mla_w_uq", "mla_w_ukv")
SMALL_SHARED = ("norm1_g", "gla_onorm_g", "conf_dw_b", "conf_ln_g", "conf_ln_b", "mla_q_norm_g", "mla_kv_norm_g",
                "norm2_g")
SMALL_SHARDED = ("gla_fg_up", "gla_fg_b", "conf_dw", "sc_dw")
WEIGHTS = ("c_ctx", "norm1_g", "w_mod", "b_mod", "w_in", "gla_fg_up", "gla_fg_b", "gla_onorm_g", "conf_dw",
           "conf_dw_b", "conf_ln_g", "conf_ln_b", "sc_dw", "mla_q_norm_g", "mla_kv_norm_g", "mla_w_uq", "mla_w_ukv",
           "w_out", "norm2_g", "ffn_w1", "ffn_w3", "ffn_w2", "final_norm_g")


def _gather_last(pieces8):
    moved = jnp.moveaxis(pieces8, 0, -2)
    return moved.reshape(moved.shape[:-2] + (-1,))


def _sum_devices(x8, name):
    shape = x8.shape[1:]
    return _sum_leading(x8.reshape(N_DEV, -1, shape[-1]), name).reshape(shape)


def _step(w, m, v, x, c, ctx, loss_target):
    depth = w["norm1_g"].shape[0]
    seq, D = x.shape[1], x.shape[2]
    n_ctx = ctx.shape[1]
    me = 4 * lax.axis_index("x") + 2 * lax.axis_index("y") + lax.axis_index("c")

    G = dict(zip(BIG, _all_gather([w[n].astype(BF16) for n in BIG], "gather_weights")))
    small8 = _all_gather([c] + [w[n] for n in SMALL_SHARDED], "gather_small")
    c_all = small8[0].reshape(N_DEV, D)
    small_full = {n: _gather_last(g) for n, g in zip(SMALL_SHARDED, small8[1:])}

    rows = jnp.concatenate([c_all, w["c_ctx"][None], jnp.zeros((16 - N_DEV - 1, D), F32)])
    (act,) = _rowop(_f_silu, "silu")((rows,), (), ())
    n_mod = w["w_mod"].shape[2]
    b_mine = lax.dynamic_slice_in_dim(w["b_mod"], me * n_mod, n_mod, axis=1)
    mod_cols = [_rowop(_f_add_bias, "add_bias")((_matmul(act, w["w_mod"][i], "nn", "mod_fwd"),), (),
                                                (_shared(b_mine[i]),))[0] for i in range(depth)]
    (mods8,) = _all_gather([jnp.stack(mod_cols)], "gather_mod")
    mods = jnp.moveaxis(mods8, 0, 2).reshape(depth, 16, N_DEV * n_mod)
    mods_l = [lax.dynamic_index_in_dim(mods[i], me, 0, keepdims=False) for i in range(depth)]
    mods_c = [mods[i, N_DEV] for i in range(depth)]

    smalls, plains, cars = [], [], []
    for i in range(depth):
        p = {n: w[n][i] for n in SMALL_SHARED}
        p.update({n: small_full[n][i] for n in SMALL_SHARDED})
        smalls.append(p)
        plains.append(_plain_weights(G, i, D))
        car = {n: lax.empty(a.shape, F32) for n, a in plains[i].items()}
        car.update({n: lax.empty((N_DEV,) + G[n].shape[2:], F32) for n in COLS + ROWS})
        cars.append(car)
    t0 = jnp.concatenate([ctx[0], x[0]], axis=0)
    t_out, vjp = jax.vjp(lambda t, ps, cs, ml, mc: _trunk(t, ps, cs, plains, G, ml, mc, n_ctx, seq),
                         t0, smalls, cars, mods_l, mods_c)
    loss_lanes, d_out, d_final_g = _loss_call(t_out[n_ctx:], w["final_norm_g"][None], loss_target[0])
    d_t0, d_smalls, d_cars, d_mods_l, d_mods_c = vjp(jnp.concatenate([jnp.zeros((n_ctx, D), F32), d_out], axis=0))
    grad_x = d_t0[n_ctx:][None]

    grads = {n: [] for n in BIG}
    for i in range(depth):
        sharded = _shard_grads(d_cars[i])
        for n, g in zip(BIG, _reduce_scatter([sharded[n] for n in BIG])):
            grads[n].append(g)
    grads = {n: jnp.stack(g) for n, g in grads.items()}

    names = SMALL_SHARED + SMALL_SHARDED
    d_mod = jnp.stack([jnp.stack([d_mods_l[i], d_mods_c[i]]) for i in range(depth)])
    parts = [loss_lanes, d_final_g, d_mod] + [jnp.stack([d_smalls[i][n] for i in range(depth)]) for n in names]
    parts8 = _all_gather(parts, "gather_partials")
    summed = [_sum_devices(p8, "sum_partials") for p8 in parts8]
    (loss_row,) = _rowop(lambda a: (jnp.sum(a, axis=-1, keepdims=True) + jnp.zeros_like(a),), "loss_sum")(
        (summed[0],), (), ())
    loss = loss_row[0, 0]
    grads["final_norm_g"] = summed[1].reshape(D)
    for n, g in zip(names, summed[3:]):
        if n in SMALL_SHARDED:
            g = lax.dynamic_slice_in_dim(g, me * w[n].shape[-1], w[n].shape[-1], axis=g.ndim - 1)
        grads[n] = g

    d_mod8, d_mod_c = parts8[2], summed[2][:, 1]
    grads["b_mod"] = _rowop(lambda a, b: (a + b,), "add")((summed[2][:, 0], d_mod_c), (), ())[0]
    d_rows = jnp.concatenate([jnp.moveaxis(d_mod8[:, :, 0], 0, 1), d_mod_c[:, None],
                              jnp.zeros((depth, 16 - N_DEV - 1, 6 * D), F32)], axis=1)
    d_rows = lax.dynamic_slice_in_dim(d_rows, me * n_mod, n_mod, axis=2)
    grads["w_mod"] = jnp.stack([_matmul(act, d_rows[i], "tn", "mod_dw") for i in range(depth)])
    d_act = _matmul(d_rows.transpose(1, 0, 2).reshape(16, depth * n_mod),
                    w["w_mod"].transpose(1, 0, 2).reshape(D, depth * n_mod), "nt", "mod_dact")
    (d_act8,) = _all_gather([d_act], "gather_dact")
    (d_rows_in,) = _rowop(_f_mul_silu_grad, "silu_grad")((_sum_devices(d_act8, "sum_dact"), rows), (), ())
    grads["c_ctx"] = d_rows_in[N_DEV]

    outs = {n: _adamw(w[n], grads[n], m[n], v[n]) for n in WEIGHTS}
    return (loss, grad_x, *[grads[n] for n in WEIGHTS], *[outs[n][0] for n in WEIGHTS],
            *[outs[n][1] for n in WEIGHTS], *[outs[n][2] for n in WEIGHTS])


def kernel(x, c, ctx, c_ctx, norm1_g, w_mod, b_mod, w_in, gla_fg_up, gla_fg_b, gla_onorm_g, conf_dw, conf_dw_b, conf_ln_g, conf_ln_b, sc_dw, mla_q_norm_g, mla_kv_norm_g, mla_w_uq, mla_w_ukv, w_out, norm2_g, ffn_w1, ffn_w3, ffn_w2, final_norm_g, loss_target, m_c_ctx, m_norm1_g, m_w_mod, m_b_mod, m_w_in, m_gla_fg_up, m_gla_fg_b, m_gla_onorm_g, m_conf_dw, m_conf_dw_b, m_conf_ln_g, m_conf_ln_b, m_sc_dw, m_mla_q_norm_g, m_mla_kv_norm_g, m_mla_w_uq, m_mla_w_ukv, m_w_out, m_norm2_g, m_ffn_w1, m_ffn_w3, m_ffn_w2, m_final_norm_g, v_c_ctx, v_norm1_g, v_w_mod, v_b_mod, v_w_in, v_gla_fg_up, v_gla_fg_b, v_gla_onorm_g, v_conf_dw, v_conf_dw_b, v_conf_ln_g, v_conf_ln_b, v_sc_dw, v_mla_q_norm_g, v_mla_kv_norm_g, v_mla_w_uq, v_mla_w_ukv, v_w_out, v_norm2_g, v_ffn_w1, v_ffn_w3, v_ffn_w2, v_final_norm_g):
    w = dict(c_ctx=c_ctx, norm1_g=norm1_g, w_mod=w_mod, b_mod=b_mod, w_in=w_in, gla_fg_up=gla_fg_up, gla_fg_b=gla_fg_b, gla_onorm_g=gla_onorm_g, conf_dw=conf_dw, conf_dw_b=conf_dw_b, conf_ln_g=conf_ln_g, conf_ln_b=conf_ln_b, sc_dw=sc_dw, mla_q_norm_g=mla_q_norm_g, mla_kv_norm_g=mla_kv_norm_g, mla_w_uq=mla_w_uq, mla_w_ukv=mla_w_ukv, w_out=w_out, norm2_g=norm2_g, ffn_w1=ffn_w1, ffn_w3=ffn_w3, ffn_w2=ffn_w2, final_norm_g=final_norm_g)
    m = dict(c_ctx=m_c_ctx, norm1_g=m_norm1_g, w_mod=m_w_mod, b_mod=m_b_mod, w_in=m_w_in, gla_fg_up=m_gla_fg_up, gla_fg_b=m_gla_fg_b, gla_onorm_g=m_gla_onorm_g, conf_dw=m_conf_dw, conf_dw_b=m_conf_dw_b, conf_ln_g=m_conf_ln_g, conf_ln_b=m_conf_ln_b, sc_dw=m_sc_dw, mla_q_norm_g=m_mla_q_norm_g, mla_kv_norm_g=m_mla_kv_norm_g, mla_w_uq=m_mla_w_uq, mla_w_ukv=m_mla_w_ukv, w_out=m_w_out, norm2_g=m_norm2_g, ffn_w1=m_ffn_w1, ffn_w3=m_ffn_w3, ffn_w2=m_ffn_w2, final_norm_g=m_final_norm_g)
    v = dict(c_ctx=v_c_ctx, norm1_g=v_norm1_g, w_mod=v_w_mod, b_mod=v_b_mod, w_in=v_w_in, gla_fg_up=v_gla_fg_up, gla_fg_b=v_gla_fg_b, gla_onorm_g=v_gla_onorm_g, conf_dw=v_conf_dw, conf_dw_b=v_conf_dw_b, conf_ln_g=v_conf_ln_g, conf_ln_b=v_conf_ln_b, sc_dw=v_sc_dw, mla_q_norm_g=v_mla_q_norm_g, mla_kv_norm_g=v_mla_kv_norm_g, mla_w_uq=v_mla_w_uq, mla_w_ukv=v_mla_w_ukv, w_out=v_w_out, norm2_g=v_norm2_g, ffn_w1=v_ffn_w1, ffn_w3=v_ffn_w3, ffn_w2=v_ffn_w2, final_norm_g=v_final_norm_g)
    return _step(w, m, v, x, c, ctx, loss_target)
```

```python
import functools
import math

import jax
import jax.numpy as jnp
from jax import lax
from jax.experimental import pallas as pl
from jax.experimental.pallas import tpu as pltpu

F32 = jnp.float32
BF16 = jnp.bfloat16
MESH = pl.DeviceIdType.MESH
N_DEV = 8

EPS = 1e-6
GRID_W = 64
HEADS = 4
HEAD_W = 128
MIX_W = HEADS * HEAD_W
GATE_RANK = 16
GATE_NORM = 16.0
GLA_CHUNK = 128
CONF_K = 31
SC_K = 3
Q_RANK = 384
KV_RANK = 128
ROPE = 64
ROPE_FREQS = 16
ROPE_BASE = 10000.0
ATT_SCALE = (HEAD_W + ROPE) ** -0.5
CONV_HALO = 16

ADAM_LR = 0.001
ADAM_B1 = 0.9
ADAM_B2 = 0.999
ADAM_EPS = 1e-08
ADAM_WD = 0.01
ADAM_STEP = 10

LANES = 128
VMEM_LIMIT = 56 * 2 ** 20
ROW_BLOCK_BYTES = 10 * 2 ** 20

Z_Q, Z_K, Z_V, Z_G, Z_LR, Z_A, Z_GATE, Z_BG, Z_CG, Z_H, Z_CQ, Z_CKV, Z_KR, Z_END = (
    0, 512, 1024, 1536, 2048, 2176, 2688, 3200, 3712, 4224, 4736, 5120, 5248, 5376)
IN_W = 5216


def _pcall(body, **kw):
    return pl.pallas_call(body, **kw)


def _params(sem=None):
    return pltpu.CompilerParams(dimension_semantics=sem, vmem_limit_bytes=VMEM_LIMIT)


def _pick(dim, cap, mult):
    d = (min(cap, dim) // mult) * mult
    while d >= mult:
        if dim % d == 0:
            return d
        d -= mult
    return dim


def _mm_call(name, a, b, out_shape, grid, a_spec, b_spec, o_spec, dims, k_axis=None, once_axis=None, out_dtype=F32):
    a_blk = tuple(d for d in a_spec.block_shape if d is not None)
    o_blk = tuple(d for d in o_spec.block_shape if d is not None)
    if a.dtype == BF16:
        once_axis = None
    scratch = ([pltpu.VMEM((math.prod(o_blk[:-1]), o_blk[-1]), F32)] if k_axis is not None else []) + (
        [pltpu.VMEM(a_blk, BF16)] if once_axis is not None else [])
    nk = grid[k_axis] if k_axis is not None else 1

    def body(a_ref, b_ref, o_ref, *scr):
        if once_axis is not None:
            a_bf = scr[-1]

            @pl.when(pl.program_id(once_axis) == 0)
            def _():
                a_bf[...] = a_ref[...].astype(BF16)

            av = a_bf[...]
        else:
            av = a_ref[...].astype(BF16)
        bv = b_ref[...].astype(BF16)
        if bv.ndim == 3:
            bv = bv.reshape(-1, bv.shape[-1])
        prod = lax.dot_general(av, bv, dims, preferred_element_type=F32)
        if k_axis is None:
            o_ref[...] = prod.astype(o_ref.dtype).reshape(o_ref.shape)
        else:
            acc, k = scr[0], pl.program_id(k_axis)

            @pl.when(k == 0)
            def _():
                acc[...] = prod

            @pl.when(k != 0)
            def _():
                acc[...] += prod

            @pl.when(k == nk - 1)
            def _():
                o_ref[...] = acc[...].astype(o_ref.dtype).reshape(o_ref.shape)

    return _pcall(
        body, name=name, grid=grid, in_specs=[a_spec, b_spec], out_specs=o_spec,
        out_shape=jax.ShapeDtypeStruct(out_shape, out_dtype), scratch_shapes=scratch,
        compiler_params=_params(("arbitrary",) * len(grid)),
    )(a, b)


NN = (((1,), (0,)), ((), ()))
NT = (((1,), (1,)), ((), ()))
TN = (((0,), (0,)), ((), ()))


def _matmul(a, b, mode, name, out_dtype=F32):
    if mode == "nn":
        (M, K), (_, N) = a.shape, b.shape
    elif mode == "nt":
        (M, K), (N, _) = a.shape, b.shape
    else:
        (K, M), (_, N) = a.shape, b.shape
    if mode == "nn":
        tm, tn, tk = _pick(M, 1088, 16), _pick(N, 768, LANES), _pick(K, 2048, LANES)
    elif mode == "nt":
        tm, tn, tk = _pick(M, 1088, 16), _pick(N, 2048, LANES), _pick(K, 1024, LANES)
    else:
        tm, tn, tk = _pick(M, 2048, LANES), _pick(N, 768, LANES), _pick(K, 1088, 16)
    if mode == "nn":
        a_spec = pl.BlockSpec((tm, tk), lambda i, j, k: (i, k))
        b_spec = pl.BlockSpec((tk, tn), lambda i, j, k: (k, j))
    elif mode == "nt":
        a_spec = pl.BlockSpec((tm, tk), lambda i, j, k: (i, k))
        b_spec = pl.BlockSpec((tn, tk), lambda i, j, k: (j, k))
    else:
        a_spec = pl.BlockSpec((tk, tm), lambda i, j, k: (k, i))
        b_spec = pl.BlockSpec((tk, tn), lambda i, j, k: (k, j))
    return _mm_call(name, a, b, (M, N), (M // tm, N // tn, K // tk), a_spec, b_spec,
                    pl.BlockSpec((tm, tn), lambda i, j, k: (i, j)), {"nn": NN, "nt": NT, "tn": TN}[mode], k_axis=2,
                    out_dtype=out_dtype)


@jax.custom_vjp
def mm(a, w, carrier):
    return _matmul(a, w, "nn", "mm_fwd")


def _mm_fwd(a, w, carrier):
    return _matmul(a, w, "nn", "mm_fwd"), (a, w)


def _mm_bwd(res, dc):
    a, w = res
    return _matmul(dc, w, "nt", "mm_da", a.dtype), jnp.zeros_like(w), _matmul(a, dc, "tn", "mm_dw")


mm.defvjp(_mm_fwd, _mm_bwd)


def _make_mm_cols(layer):
    def forward(a, G):
        (M, K), n = a.shape, G.shape[3]
        tm = _pick(M, 1088, 16)
        return _mm_call("mmc_fwd", a, G, (N_DEV, M, n), (M // tm, N_DEV),
                        pl.BlockSpec((tm, K), lambda i, d: (i, 0)),
                        pl.BlockSpec((None, None, K, n), lambda i, d: (d, layer, 0, 0)),
                        pl.BlockSpec((None, tm, n), lambda i, d: (d, i, 0)), NN, once_axis=1, out_dtype=BF16)

    def grad_a(do, G, dtype):
        (_, M, n), K = do.shape, G.shape[2]
        tm, tn = _pick(M, 1088, 16), _pick(K, 1024, LANES)
        return _mm_call("mmc_da", do, G, (M, K), (M // tm, K // tn, N_DEV),
                        pl.BlockSpec((None, tm, n), lambda i, j, d: (d, i, 0)),
                        pl.BlockSpec((None, None, tn, n), lambda i, j, d: (d, layer, j, 0)),
                        pl.BlockSpec((tm, tn), lambda i, j, d: (i, j)), NT, k_axis=2, out_dtype=dtype)

    def grad_w(a, do):
        (M, K), n = a.shape, do.shape[2]
        tm, tk = _pick(K, 2048, LANES), _pick(M, 1088, 16)
        return _mm_call("mmc_dw", a, do, (N_DEV, K, n), (N_DEV, K // tm, M // tk),
                        pl.BlockSpec((tk, tm), lambda d, i, k: (k, i)),
                        pl.BlockSpec((None, tk, n), lambda d, i, k: (d, k, 0)),
                        pl.BlockSpec((None, tm, n), lambda d, i, k: (d, i, 0)), TN, k_axis=2)

    @jax.custom_vjp
    def f(a, G, carrier):
        return forward(a, G)

    def f_fwd(a, G, carrier):
        return forward(a, G), (a, G)

    def f_bwd(res, do):
        a, G = res
        return grad_a(do, G, a.dtype), jnp.zeros_like(G), grad_w(a, do)

    f.defvjp(f_fwd, f_bwd)
    return f


def _make_mm_rows(layer):
    def forward(a, G):
        M, (r, N) = a.shape[-2], G.shape[2:]
        tm, tn = _pick(M, 1088, 16), _pick(N, 1024, LANES)
        if a.ndim == 2:
            return _mm_call("mmr2_fwd", a, G, (M, N), (M // tm, N // tn),
                            pl.BlockSpec((tm, N_DEV * r), lambda i, j: (i, 0)),
                            pl.BlockSpec((N_DEV, None, r, tn), lambda i, j: (0, layer, 0, j)),
                            pl.BlockSpec((tm, tn), lambda i, j: (i, j)), NN)
        return _mm_call("mmr_fwd", a, G, (M, N), (M // tm, N // tn, N_DEV),
                        pl.BlockSpec((None, tm, r), lambda i, j, d: (d, i, 0)),
                        pl.BlockSpec((None, None, r, tn), lambda i, j, d: (d, layer, 0, j)),
                        pl.BlockSpec((tm, tn), lambda i, j, d: (i, j)), NN, k_axis=2)

    def grad_a(dc, G, like):
        (M, N), r = dc.shape, G.shape[2]
        tm = _pick(M, 1088, 16)
        if like.ndim == 2:
            return _mm_call("mmr2_da", dc, G, like.shape, (M // tm,),
                            pl.BlockSpec((tm, N), lambda i: (i, 0)),
                            pl.BlockSpec((N_DEV, None, r, N), lambda i: (0, layer, 0, 0)),
                            pl.BlockSpec((tm, N_DEV * r), lambda i: (i, 0)), NT, out_dtype=like.dtype)
        return _mm_call("mmr_da", dc, G, like.shape, (M // tm, N_DEV),
                        pl.BlockSpec((tm, N), lambda i, d: (i, 0)),
                        pl.BlockSpec((None, None, r, N), lambda i, d: (d, layer, 0, 0)),
                        pl.BlockSpec((None, tm, r), lambda i, d: (d, i, 0)), NT, once_axis=1, out_dtype=like.dtype)

    def grad_w(a, dc):
        (M, N), tk = dc.shape, _pick(dc.shape[0], 1088, 16)
        if a.ndim == 2:
            r, tn = a.shape[1] // N_DEV, _pick(N, 1024, LANES)
            return _mm_call("mmr2_dw", a, dc, (N_DEV, r, N), (N // tn, M // tk),
                            pl.BlockSpec((tk, N_DEV * r), lambda j, k: (k, 0)),
                            pl.BlockSpec((tk, tn), lambda j, k: (k, j)),
                            pl.BlockSpec((N_DEV, r, tn), lambda j, k: (0, 0, j)), TN, k_axis=1)
        r, tn = a.shape[2], _pick(N, 2048, LANES)
        return _mm_call("mmr_dw", a, dc, (N_DEV, r, N), (N_DEV, N // tn, M // tk),
                        pl.BlockSpec((None, tk, r), lambda d, j, k: (d, k, 0)),
                        pl.BlockSpec((tk, tn), lambda d, j, k: (k, j)),
                        pl.BlockSpec((None, r, tn), lambda d, j, k: (d, 0, j)), TN, k_axis=2)

    @jax.custom_vjp
    def f(a, G, carrier):
        return forward(a, G)

    def f_fwd(a, G, carrier):
        return forward(a, G), (a, G)

    def f_bwd(res, dc):
        a, G = res
        dc = dc.astype(BF16)
        return grad_a(dc, G, a), jnp.zeros_like(G), grad_w(a, dc)

    f.defvjp(f_fwd, f_bwd)
    return f


@functools.partial(jax.custom_vjp, nondiff_argnums=(1,))
def _split_cols(z, bounds):
    return tuple(z[:, a:b] for a, b in zip(bounds[:-1], bounds[1:]))


def _split_cols_fwd(z, bounds):
    return _split_cols(z, bounds), None


def _split_cols_bwd(bounds, _, cts):
    return (jnp.concatenate(cts, axis=1),)


_split_cols.defvjp(_split_cols_fwd, _split_cols_bwd)


def _rowop(fn, name, n_ctx=0, tile=256, lane_block=None, out_dtypes=None):
    def geometry(rows):
        L, w0 = rows[0].shape
        nj = w0 // lane_block if lane_block else 1
        width = 3 * sum(lane_block or r.shape[1] for r in rows)
        cap = max(16, ROW_BLOCK_BYTES // (4 * width) // 16 * 16)
        tl = _pick(math.gcd(L, n_ctx) if n_ctx else L, min(tile, cap), 16)
        return L, tl, n_ctx // tl, nj

    def block_w(x):
        return lane_block or x.shape[1]

    def row_spec(tl, x):
        if lane_block and x.shape[1] != lane_block:
            return pl.BlockSpec((tl, lane_block), lambda i, j: (i, j))
        return pl.BlockSpec((tl, block_w(x)), lambda i, j: (i, 0))

    def param_spec(p, nct):
        s, r, w = p.shape
        if s == 1:
            return pl.BlockSpec((1, r, w), lambda i, j: (0, 0, 0))
        return pl.BlockSpec((1, r, w), lambda i, j: (jnp.where(i < nct, 0, 1), 0, 0))

    def forward(rows, consts, params):
        L, tl, nct, nj = geometry(rows)
        nr, nc, npar = len(rows), len(consts), len(params)
        outs = jax.eval_shape(
            lambda: fn(*[jnp.zeros((tl, block_w(r)), F32) for r in rows + consts],
                       *[jnp.zeros(p.shape[1:], F32) for p in params]))

        def body(*refs):
            ins = [r[...].astype(F32) for r in refs[:nr + nc]] + [r[0] for r in refs[nr + nc:nr + nc + npar]]
            for o_ref, o in zip(refs[nr + nc + npar:], fn(*ins)):
                o_ref[...] = o.astype(o_ref.dtype)

        return _pcall(
            body, name=name + "_fwd", grid=(L // tl, nj),
            in_specs=[row_spec(tl, r) for r in rows + consts] + [param_spec(p, nct) for p in params],
            out_specs=[pl.BlockSpec((tl, o.shape[1]), lambda i, j: (i, j)) for o in outs],
            out_shape=[jax.ShapeDtypeStruct((L, o.shape[1] * nj), dt)
                       for o, dt in zip(outs, out_dtypes or (F32,) * len(outs))],
            compiler_params=_params(("parallel", "parallel")),
        )(*rows, *consts, *params)

    def backward(rows, consts, params, cts):
        L, tl, nct, nj = geometry(rows)
        nr, nc, npar, no = len(rows), len(consts), len(params), len(cts)

        def body(*refs):
            i, j = pl.program_id(0), pl.program_id(1)
            rv = [r[...].astype(F32) for r in refs[:nr]]
            cv = [r[...] for r in refs[nr:nr + nc]]
            pv = [r[0] for r in refs[nr + nc:nr + nc + npar]]
            ct = tuple(r[...].astype(F32) for r in refs[nr + nc + npar:nr + nc + npar + no])
            out_refs = refs[nr + nc + npar + no:]
            _, vjp = jax.vjp(lambda *d: tuple(fn(*d[:nr], *cv, *d[nr:])), *rv, *pv)
            grads = vjp(ct)
            for ref, g in zip(out_refs[:nr], grads[:nr]):
                ref[...] = g.astype(ref.dtype)
            for ref, g, p in zip(out_refs[nr:], grads[nr:], params):
                first_row = (i == 0) if (p.shape[0] == 1 or nct == 0) else ((i == 0) | (i == nct))
                first = jnp.logical_and(first_row, j == 0)

                @pl.when(first)
                def _():
                    ref[0] = g

                @pl.when(jnp.logical_not(first))
                def _():
                    ref[0] += g

        outs = _pcall(
            body, name=name + "_bwd", grid=(L // tl, nj),
            in_specs=[row_spec(tl, r) for r in rows + consts] + [param_spec(p, nct) for p in params]
            + [pl.BlockSpec((tl, c.shape[1] // nj), lambda i, j: (i, j)) for c in cts],
            out_specs=[row_spec(tl, r) for r in rows] + [param_spec(p, nct) for p in params],
            out_shape=[jax.ShapeDtypeStruct(r.shape, r.dtype) for r in rows]
            + [jax.ShapeDtypeStruct(p.shape, F32) for p in params],
            compiler_params=_params(("arbitrary", "arbitrary")),
        )(*rows, *consts, *params, *cts)
        return tuple(outs[:nr]), tuple(outs[nr:])

    @jax.custom_vjp
    def op(rows, consts, params):
        return tuple(forward(rows, consts, params))

    def op_fwd(rows, consts, params):
        return tuple(forward(rows, consts, params)), (rows, consts, params)

    def op_bwd(res, cts):
        rows, consts, params = res
        d_rows, d_params = backward(rows, consts, params, tuple(cts))
        return d_rows, tuple(jnp.zeros_like(c) for c in consts), d_params

    op.defvjp(op_fwd, op_bwd)
    return op


def _sigmoid(x):
    return 1.0 / (1.0 + jnp.exp(-x))


def _silu(x):
    return x * _sigmoid(x)


def _log_sigmoid(x):
    return jnp.minimum(x, 0.0) - jnp.log(1.0 + jnp.exp(-jnp.abs(x)))


def _rms(x, g):
    return x * lax.rsqrt(jnp.mean(x * x, axis=-1, keepdims=True) + EPS) * g


def _f_norm_mod(x, g, sc, sh):
    return (_rms(x, g) * (1.0 + sc) + sh,)


def _f_resid_norm_mod(x, o, gate, g, sc, sh):
    x1 = x + gate * o
    return x1, _rms(x1, g) * (1.0 + sc) + sh


def _f_resid(x, o, gate):
    return (x + gate * o,)


def _f_swiglu(a1, a3):
    return (_silu(a1) * a3,)


def _f_gla_gate(lr, up_f, up_b, b_f, b_b):
    dot = functools.partial(jnp.dot, preferred_element_type=F32)
    return (_log_sigmoid(dot(lr, up_f) + b_f) / GATE_NORM, _log_sigmoid(dot(lr, up_b) + b_b) / GATE_NORM)


def _f_gla_finish(o_f, o_b, gate, g):
    return (_rms(o_f + o_b, g) * _silu(gate),)


def _f_glu(a, gate):
    return (a * _sigmoid(gate),)


def _f_ln_silu(u, dw_b, g, b):
    u = u + dw_b
    xc = u - jnp.mean(u, axis=-1, keepdims=True)
    y = xc * lax.rsqrt(jnp.mean(xc * xc, axis=-1, keepdims=True) + EPS)
    return (_silu(y * g + b),)


def _f_mul(a, b):
    return (a * b,)


def _f_rms(x, g):
    return (_rms(x, g),)


def _f_rope(t, cos, sin):
    w = t.shape[1]
    r = lax.broadcasted_iota(jnp.int32, (w, w), 0)
    c = lax.broadcasted_iota(jnp.int32, (w, w), 1)
    perm = (jnp.bitwise_xor(r, ROPE_FREQS) == c).astype(F32)
    partner = jnp.dot(t, perm, precision=lax.Precision.HIGHEST, preferred_element_type=F32)
    return (t * cos + partner * sin,)


def _f_silu(x):
    return (_silu(x),)


def _f_add_bias(x, b):
    return (x + b,)


def _f_mul_silu_grad(d, x):
    _, vjp = jax.vjp(_silu, x)
    return (vjp(d)[0],)


def _conv_geometry(u, n_ctx):
    L, C = u.shape
    tl = _pick(math.gcd(L, n_ctx), 256, 8)
    return L, C, tl, n_ctx // tl, L // tl


def _conv_specs(tl, nt):
    prev = pl.BlockSpec((tl, LANES), lambda c, i: (jnp.maximum(i - 1, 0), c))
    cur = pl.BlockSpec((tl, LANES), lambda c, i: (i, c))
    nxt = pl.BlockSpec((tl, LANES), lambda c, i: (jnp.minimum(i + 1, nt - 1), c))
    return prev, cur, nxt


def _conv_window(prev_ref, cur_ref, next_ref, tl, nct, nt):
    i = pl.program_id(1)
    has_prev = jnp.logical_and(i != 0, i != nct)
    has_next = jnp.logical_and(i != nct - 1, i != nt - 1)
    prev = jnp.where(has_prev, prev_ref[tl - CONV_HALO:tl, :], 0.0)
    nxt = jnp.where(has_next, next_ref[0:CONV_HALO, :], 0.0)
    return jnp.concatenate([prev, cur_ref[...], nxt], axis=0)


def _shifted(window, off, tl):
    n = window.shape[0]
    if off == 0:
        return window[0:tl]
    return pltpu.roll(window, n - off, 0)[0:tl]


def _conv_apply(u, w, n_ctx, flip, name):
    L, C, tl, nct, nt = _conv_geometry(u, n_ctx)
    K = w.shape[0]
    pad = (K - 1) // 2
    prev, cur, nxt = _conv_specs(tl, nt)

    def body(p_ref, c_ref, n_ref, w_ref, o_ref):
        win = _conv_window(p_ref, c_ref, n_ref, tl, nct, nt)
        acc = jnp.zeros((tl, LANES), F32)
        for k in range(K):
            kk = K - 1 - k if flip else k
            acc = acc + _shifted(win, CONV_HALO - pad + k, tl) * w_ref[kk:kk + 1, :]
        o_ref[...] = acc

    return _pcall(
        body, name=name, grid=(C // LANES, nt),
        in_specs=[prev, cur, nxt, pl.BlockSpec((K, LANES), lambda c, i: (0, c))],
        out_specs=cur, out_shape=jax.ShapeDtypeStruct((L, C), F32),
        compiler_params=_params(("parallel", "parallel")),
    )(u, u, u, w)


def _conv_dw(u, dy, K, n_ctx, name):
    L, C, tl, nct, nt = _conv_geometry(u, n_ctx)
    pad = (K - 1) // 2
    prev, cur, nxt = _conv_specs(tl, nt)

    def body(p_ref, c_ref, n_ref, dy_ref, dw_ref):
        i = pl.program_id(1)

        @pl.when(i == 0)
        def _():
            dw_ref[...] = jnp.zeros_like(dw_ref)

        win = _conv_window(p_ref, c_ref, n_ref, tl, nct, nt)
        dy_t = dy_ref[...]
        for k in range(K):
            dw_ref[k:k + 1, :] += jnp.sum(_shifted(win, CONV_HALO - pad + k, tl) * dy_t, axis=0, keepdims=True)

    return _pcall(
        body, name=name, grid=(C // LANES, nt),
        in_specs=[prev, cur, nxt, cur],
        out_specs=pl.BlockSpec((K, LANES), lambda c, i: (0, c)),
        out_shape=jax.ShapeDtypeStruct((K, C), F32),
        compiler_params=_params(("parallel", "arbitrary")),
    )(u, u, u, dy)


def _make_conv(n_ctx):
    @jax.custom_vjp
    def conv(u, w):
        return _conv_apply(u, w, n_ctx, False, "conv_fwd")

    def conv_fwd(u, w):
        return _conv_apply(u, w, n_ctx, False, "conv_fwd"), (u, w)

    def conv_bwd(res, dy):
        u, w = res
        return _conv_apply(dy, w, n_ctx, True, "conv_du"), _conv_dw(u, dy, w.shape[0], n_ctx, "conv_dw")

    conv.defvjp(conv_fwd, conv_bwd)
    return conv


def _gla_chunk(q, k, v, g, st, reverse):
    C = q.shape[0]
    r = lax.broadcasted_iota(jnp.int32, (C, C), 0)
    c = lax.broadcasted_iota(jnp.int32, (C, C), 1)
    seen = (r <= c) if reverse else (r >= c)
    dot = functools.partial(lax.dot_general, preferred_element_type=F32, precision=lax.Precision.DEFAULT)
    bcum = lax.dot_general(seen.astype(F32), g, (((1,), (0,)), ((), ())), preferred_element_type=F32,
                           precision=lax.Precision.HIGHEST)
    total = jnp.sum(g, axis=0, keepdims=True)
    a = q * (HEAD_W ** -0.5) * jnp.exp(bcum)
    scores = jnp.where(seen, dot(a, k * jnp.exp(-bcum), (((1,), (1,)), ((), ()))), 0.0)
    o = dot(a, st, (((1,), (1,)), ((), ()))) + dot(scores, v, (((1,), (0,)), ((), ())))
    st_new = st * jnp.exp(total) + dot(v, k * jnp.exp(total - bcum), (((0,), (0,)), ((), ())))
    return o, st_new


def _gla_order(t, nc, ncc, reverse):
    if not reverse:
        return t
    return jnp.where(t < ncc, ncc - 1 - t, ncc + nc - 1 - t)


def _gla_fwd_call(q, k, v, g, n_ctx, reverse):
    L = q.shape[0]
    C = GLA_CHUNK
    nc, ncc = L // C, n_ctx // C
    spec = pl.BlockSpec((C, MIX_W), lambda t: (_gla_order(t, nc, ncc, reverse), 0))

    def body(q_ref, k_ref, v_ref, g_ref, o_ref, s_ref, st):
        @pl.when(pl.program_id(0) == 0)
        def _():
            st[...] = jnp.zeros_like(st)

        for h in range(HEADS):
            hs = slice(h * HEAD_W, (h + 1) * HEAD_W)
            s_ref[h, 0] = st[h]
            o, st_new = _gla_chunk(q_ref[:, hs], k_ref[:, hs], v_ref[:, hs], g_ref[:, hs], st[h], reverse)
            o_ref[:, hs] = o
            st[h] = st_new

    return _pcall(
        body, name="gla_fwd", grid=(nc,), in_specs=[spec] * 4,
        out_specs=[spec, pl.BlockSpec((HEADS, 1, HEAD_W, HEAD_W), lambda t: (0, t, 0, 0))],
        out_shape=[jax.ShapeDtypeStruct((L, MIX_W), F32), jax.ShapeDtypeStruct((HEADS, nc, HEAD_W, HEAD_W), F32)],
        scratch_shapes=[pltpu.VMEM((HEADS, HEAD_W, HEAD_W), F32)],
        compiler_params=_params(("arbitrary",)),
    )(q, k, v, g)


def _gla_bwd_call(q, k, v, g, states, do, n_ctx, reverse):
    L = q.shape[0]
    C = GLA_CHUNK
    nc, ncc = L // C, n_ctx // C
    spec = pl.BlockSpec((C, MIX_W), lambda t: (_gla_order(nc - 1 - t, nc, ncc, reverse), 0))

    def body(q_ref, k_ref, v_ref, g_ref, s_ref, do_ref, dq_ref, dk_ref, dv_ref, dg_ref, dst):
        @pl.when(pl.program_id(0) == 0)
        def _():
            dst[...] = jnp.zeros_like(dst)

        for h in range(HEADS):
            hs = slice(h * HEAD_W, (h + 1) * HEAD_W)
            _, vjp = jax.vjp(functools.partial(_gla_chunk, reverse=reverse),
                             q_ref[:, hs], k_ref[:, hs], v_ref[:, hs], g_ref[:, hs], s_ref[h, 0])
            dq, dk, dv, dg, dst_prev = vjp((do_ref[:, hs], dst[h]))
            dq_ref[:, hs] = dq
            dk_ref[:, hs] = dk
            dv_ref[:, hs] = dv
            dg_ref[:, hs] = dg
            dst[h] = dst_prev

    return _pcall(
        body, name="gla_bwd", grid=(nc,),
        in_specs=[spec] * 4 + [pl.BlockSpec((HEADS, 1, HEAD_W, HEAD_W), lambda t: (0, nc - 1 - t, 0, 0)), spec],
        out_specs=[spec] * 4, out_shape=[jax.ShapeDtypeStruct((L, MIX_W), F32)] * 4,
        scratch_shapes=[pltpu.VMEM((HEADS, HEAD_W, HEAD_W), F32)],
        compiler_params=_params(("arbitrary",)),
    )(q, k, v, g, states, do)


def _make_gla(n_ctx, reverse):
    @jax.custom_vjp
    def gla(q, k, v, g):
        return _gla_fwd_call(q, k, v, g, n_ctx, reverse)[0]

    def gla_fwd(q, k, v, g):
        o, states = _gla_fwd_call(q, k, v, g, n_ctx, reverse)
        return o, (q, k, v, g, states)

    def gla_bwd(res, do):
        q, k, v, g, states = res
        return tuple(_gla_bwd_call(q, k, v, g, states, do, n_ctx, reverse))

    gla.defvjp(gla_fwd, gla_bwd)
    return gla


def _att_scaled(q_ref):
    return (q_ref[...] * ATT_SCALE).astype(BF16)


def _att_probs(qn, qr, kn, kr, i, nct, n_ctx):
    nt_dims = (((1,), (1,)), ((), ()))
    s = lax.dot_general(qn, kn, nt_dims, preferred_element_type=F32)
    s = s + lax.dot_general(qr, kr, nt_dims, preferred_element_type=F32)
    col = lax.broadcasted_iota(jnp.int32, (1, s.shape[1]), 1)
    s = s + jnp.where(col < jnp.where(i < nct, n_ctx, s.shape[1]), 0.0, -1e30)
    p = jnp.exp(s - jnp.max(s, axis=-1, keepdims=True))
    return p * (1.0 / jnp.sum(p, axis=-1, keepdims=True))


def _att_geometry(qn, n_ctx):
    L = qn.shape[0]
    tq = _pick(math.gcd(L, n_ctx), 256, 8)
    q_spec = pl.BlockSpec((tq, HEAD_W), lambda h, i: (i, h))
    k_spec = pl.BlockSpec((L, HEAD_W), lambda h, i: (0, h))
    kr_spec = pl.BlockSpec((L, HEAD_W), lambda h, i: (0, 0))
    return L, tq, n_ctx // tq, q_spec, k_spec, kr_spec


def _att_fwd_call(qn, qr, kn, kr, v, n_ctx):
    L, tq, nct, q_spec, k_spec, kr_spec = _att_geometry(qn, n_ctx)

    def body(qn_ref, qr_ref, kn_ref, kr_ref, v_ref, o_ref):
        p = _att_probs(_att_scaled(qn_ref), _att_scaled(qr_ref), kn_ref[...].astype(BF16),
                       kr_ref[...].astype(BF16), pl.program_id(1), nct, n_ctx)
        o_ref[...] = jnp.dot(p.astype(BF16), v_ref[...].astype(BF16), preferred_element_type=F32).astype(BF16)

    return _pcall(
        body, name="att_fwd", grid=(HEADS, L // tq),
        in_specs=[q_spec, q_spec, k_spec, kr_spec, k_spec], out_specs=q_spec,
        out_shape=jax.ShapeDtypeStruct((L, MIX_W), BF16),
        compiler_params=_params(("parallel", "parallel")),
    )(qn, qr, kn, kr, v)


def _att_bwd_call(qn, qr, kn, kr, v, do, n_ctx):
    L, tq, nct, q_spec, k_spec, kr_spec = _att_geometry(qn, n_ctx)
    tn_dims = (((0,), (0,)), ((), ()))

    def body(qn_ref, qr_ref, kn_ref, kr_ref, v_ref, do_ref, dqn_ref, dqr_ref, dkn_ref, dkr_ref, dv_ref):
        h, i = pl.program_id(0), pl.program_id(1)
        qn, qr = _att_scaled(qn_ref), _att_scaled(qr_ref)
        kn, kr, vv = kn_ref[...].astype(BF16), kr_ref[...].astype(BF16), v_ref[...].astype(BF16)
        do = do_ref[...].astype(BF16)
        p = _att_probs(qn, qr, kn, kr, i, nct, n_ctx)
        dp = lax.dot_general(do, vv, (((1,), (1,)), ((), ())), preferred_element_type=F32)
        ds = (p * (dp - jnp.sum(p * dp, axis=-1, keepdims=True))).astype(BF16)
        dqn_ref[...] = jnp.dot(ds, kn, preferred_element_type=F32) * ATT_SCALE
        dqr_ref[...] = jnp.dot(ds, kr, preferred_element_type=F32) * ATT_SCALE

        @pl.when(i == 0)
        def _():
            dkn_ref[...] = jnp.zeros_like(dkn_ref)
            dv_ref[...] = jnp.zeros_like(dv_ref)

        @pl.when(jnp.logical_and(i == 0, h == 0))
        def _():
            dkr_ref[...] = jnp.zeros_like(dkr_ref)

        dkn_ref[...] += lax.dot_general(ds, qn, tn_dims, preferred_element_type=F32)
        dkr_ref[...] += lax.dot_general(ds, qr, tn_dims, preferred_element_type=F32)
        dv_ref[...] += lax.dot_general(p.astype(BF16), do, tn_dims, preferred_element_type=F32)

    return _pcall(
        body, name="att_bwd", grid=(HEADS, L // tq),
        in_specs=[q_spec, q_spec, k_spec, kr_spec, k_spec, q_spec],
        out_specs=[q_spec, q_spec, k_spec, kr_spec, k_spec],
        out_shape=[jax.ShapeDtypeStruct((L, MIX_W), F32)] * 3 + [jax.ShapeDtypeStruct((L, HEAD_W), F32),
                                                                 jax.ShapeDtypeStruct((L, MIX_W), F32)],
        compiler_params=_params(("arbitrary", "arbitrary")),
    )(qn, qr, kn, kr, v, do)


def _make_attention(n_ctx):
    @jax.custom_vjp
    def att(qn, qr, kn, kr, v):
        return _att_fwd_call(qn, qr, kn, kr, v, n_ctx)

    def att_fwd(qn, qr, kn, kr, v):
        return _att_fwd_call(qn, qr, kn, kr, v, n_ctx), (qn, qr, kn, kr, v)

    def att_bwd(res, do):
        return tuple(_att_bwd_call(*res, do, n_ctx))

    att.defvjp(att_fwd, att_bwd)
    return att


def _loss_call(x, g, target):
    L, D = x.shape
    tl = _pick(L, 256, 8)

    def f(xv, gv, tv):
        err = _rms(xv, gv) - tv
        return 0.5 * jnp.sum(err * err, axis=0, keepdims=True) / D

    def body(x_ref, g_ref, t_ref, loss_ref, dx_ref, dg_ref):
        i = pl.program_id(0)
        loss, vjp = jax.vjp(lambda xv, gv: f(xv, gv, t_ref[...]), x_ref[...], g_ref[...])
        dx, dg = vjp(jnp.ones_like(loss))
        dx_ref[...] = dx

        @pl.when(i == 0)
        def _():
            loss_ref[...] = loss
            dg_ref[...] = dg

        @pl.when(i != 0)
        def _():
            loss_ref[...] += loss
            dg_ref[...] += dg

    row = pl.BlockSpec((tl, D), lambda i: (i, 0))
    one = pl.BlockSpec((1, D), lambda i: (0, 0))
    return _pcall(
        body, name="loss", grid=(L // tl,), in_specs=[row, one, row], out_specs=[one, row, one],
        out_shape=[jax.ShapeDtypeStruct((1, D), F32), jax.ShapeDtypeStruct((L, D), F32),
                   jax.ShapeDtypeStruct((1, D), F32)],
        compiler_params=_params(("arbitrary",)),
    )(x, g, target)


def _sum_leading(x, name):
    n, R, W = x.shape
    tr = _pick(R, 512, 8)

    def body(x_ref, o_ref):
        acc = x_ref[0]
        for d in range(1, n):
            acc = acc + x_ref[d]
        o_ref[...] = acc

    return _pcall(
        body, name=name, grid=(R // tr,), in_specs=[pl.BlockSpec((n, tr, W), lambda i: (0, i, 0))],
        out_specs=pl.BlockSpec((tr, W), lambda i: (i, 0)), out_shape=jax.ShapeDtypeStruct((R, W), F32),
        compiler_params=_params(("parallel",)),
    )(x)


def _adamw(w, g, m, v):
    shape = w.shape
    W = shape[-1]
    as2d = lambda t: t.reshape(-1, W)
    R = as2d(w).shape[0]
    tr = _pick(R, max(8, (2 ** 17 // W) // 8 * 8), 8)

    def body(w_ref, g_ref, m_ref, v_ref, d_ref, nm_ref, nv_ref):
        gv = g_ref[...]
        m_new = ADAM_B1 * m_ref[...] + (1.0 - ADAM_B1) * gv
        v_new = ADAM_B2 * v_ref[...] + (1.0 - ADAM_B2) * (gv * gv)
        m_hat = m_new / (1.0 - ADAM_B1 ** ADAM_STEP)
        v_hat = v_new / (1.0 - ADAM_B2 ** ADAM_STEP)
        d_ref[...] = -ADAM_LR * (m_hat / (jnp.sqrt(v_hat) + ADAM_EPS) + ADAM_WD * w_ref[...])
        nm_ref[...] = m_new
        nv_ref[...] = v_new

    spec = pl.BlockSpec((tr, W), lambda i: (i, 0))
    outs = _pcall(
        body, name="adamw", grid=(R // tr,), in_specs=[spec] * 4, out_specs=[spec] * 3,
        out_shape=[jax.ShapeDtypeStruct((R, W), F32)] * 3, compiler_params=_params(("parallel",)),
    )(as2d(w), as2d(g), as2d(m), as2d(v))
    return tuple(o.reshape(shape) for o in outs)


HBM = pl.BlockSpec(memory_space=pltpu.HBM)


def _place():
    x, y, c = lax.axis_index("x"), lax.axis_index("y"), lax.axis_index("c")
    return x, y, c, [(1 - x, y), (x, 1 - y), (1 - x, 1 - y)]


def _all_gather(blocks, name):
    n = len(blocks)

    def body(*refs):
        x_refs, out_refs = refs[:n], refs[n:2 * n]
        send_sems, recv_sems, local_sems = refs[2 * n:]
        x, y, c, chips = _place()
        me, sibling = (x, y, c), (x, y, 1 - c)

        def slot(a, px, py, pc):
            return out_refs[a].at[4 * px + 2 * py + pc]

        def copy(a, k, blk, to, src=None):
            return pltpu.make_async_remote_copy(
                src_ref=slot(a, *blk) if src is None else src, dst_ref=slot(a, *blk),
                send_sem=send_sems.at[7 * a + k], recv_sem=recv_sems.at[7 * a + k], device_id=to, device_id_type=MESH)

        mine = [pltpu.make_async_copy(x_refs[a], slot(a, *me), local_sems.at[a]) for a in range(n)]
        for cp in mine:
            cp.start()
        first = []
        for a in range(n):
            first.append(copy(a, 0, me, sibling, src=x_refs[a]))
            first += [copy(a, 1 + j, me, (*chip, c), src=x_refs[a]) for j, chip in enumerate(chips)]
        for cp in first:
            cp.start()
        passed = []
        for j, chip in enumerate(chips):
            for a in range(n):
                copy(a, 1 + j, (*chip, c), me).wait_recv()
                passed.append(copy(a, 4 + j, (*chip, c), sibling))
                passed[-1].start()
        for a in range(n):
            copy(a, 0, sibling, me).wait_recv()
        for j, chip in enumerate(chips):
            for a in range(n):
                copy(a, 4 + j, (*chip, 1 - c), me).wait_recv()
        for cp in first + passed:
            cp.wait_send()
        for cp in mine:
            cp.wait()

    return _pcall(
        body, name=name, out_shape=[jax.ShapeDtypeStruct((N_DEV,) + b.shape, b.dtype) for b in blocks],
        in_specs=[HBM] * n, out_specs=[HBM] * n,
        scratch_shapes=[pltpu.SemaphoreType.DMA((7 * n,)), pltpu.SemaphoreType.DMA((7 * n,)),
                        pltpu.SemaphoreType.DMA((n,))],
    )(*blocks)


def _send_to_sibling(gs, name):
    n = len(gs)

    def body(*refs):
        g_refs, out_refs, (send_sems, recv_sems) = refs[:n], refs[n:2 * n], refs[2 * n:]
        x, y, c, _ = _place()
        copies = [pltpu.make_async_remote_copy(
            src_ref=g_refs[a].at[2 * q + 1 - c], dst_ref=out_refs[a].at[q], send_sem=send_sems.at[4 * a + q],
            recv_sem=recv_sems.at[4 * a + q], device_id=(x, y, 1 - c), device_id_type=MESH)
            for a in range(n) for q in range(4)]
        for cp in copies:
            cp.start()
        for cp in copies:
            cp.wait()

    return _pcall(
        body, name=name, out_shape=[jax.ShapeDtypeStruct((4,) + g.shape[1:], g.dtype) for g in gs],
        in_specs=[HBM] * n, out_specs=[HBM] * n,
        scratch_shapes=[pltpu.SemaphoreType.DMA((4 * n,)), pltpu.SemaphoreType.DMA((4 * n,))],
    )(*gs)


def _send_to_chips(ps, name):
    n = len(ps)

    def body(*refs):
        p_refs, out_refs, (send_sems, recv_sems) = refs[:n], refs[n:2 * n], refs[2 * n:]
        x, y, c, chips = _place()
        copies = [pltpu.make_async_remote_copy(
            src_ref=p_refs[a].at[2 * cx + cy], dst_ref=out_refs[a].at[j], send_sem=send_sems.at[3 * a + j],
            recv_sem=recv_sems.at[3 * a + j], device_id=(cx, cy, c), device_id_type=MESH)
            for a in range(n) for j, (cx, cy) in enumerate(chips)]
        for cp in copies:
            cp.start()
        for cp in copies:
            cp.wait()

    return _pcall(
        body, name=name, out_shape=[jax.ShapeDtypeStruct((3,) + p.shape[1:], p.dtype) for p in ps],
        in_specs=[HBM] * n, out_specs=[HBM] * n,
        scratch_shapes=[pltpu.SemaphoreType.DMA((3 * n,)), pltpu.SemaphoreType.DMA((3 * n,))],
    )(*ps)


def _add_rows(R, W):
    return _pick(R, max(16, 2 ** 19 // W // 16 * 16), 16)


def _add_sibling(g, recv, core):
    _, R, W = g.shape
    tr = _add_rows(R, W)

    def body(core_ref, g_ref, r_ref, o_ref):
        o_ref[...] = (g_ref[...] + r_ref[...]).astype(BF16)

    return _pcall(
        body, name="rs_add_sibling",
        grid_spec=pltpu.PrefetchScalarGridSpec(
            num_scalar_prefetch=1, grid=(4, R // tr),
            in_specs=[pl.BlockSpec((None, tr, W), lambda q, i, core_ref: (2 * q + core_ref[0], i, 0)),
                      pl.BlockSpec((None, tr, W), lambda q, i, core_ref: (q, i, 0))],
            out_specs=pl.BlockSpec((None, tr, W), lambda q, i, core_ref: (q, i, 0))),
        out_shape=jax.ShapeDtypeStruct((4, R, W), BF16), compiler_params=_params(("parallel", "parallel")),
    )(core, g, recv)


def _add_chips(p, recv, chip):
    _, R, W = p.shape
    tr = _add_rows(R, W)

    def body(chip_ref, p_ref, r_ref, o_ref):
        up = lambda t: t.astype(F32)
        o_ref[...] = ((up(p_ref[...]) + up(r_ref[0])) + up(r_ref[1])) + up(r_ref[2])

    return _pcall(
        body, name="rs_add_chips",
        grid_spec=pltpu.PrefetchScalarGridSpec(
            num_scalar_prefetch=1, grid=(R // tr,),
            in_specs=[pl.BlockSpec((None, tr, W), lambda i, chip_ref: (chip_ref[0], i, 0)),
                      pl.BlockSpec((3, tr, W), lambda i, chip_ref: (0, i, 0))],
            out_specs=pl.BlockSpec((tr, W), lambda i, chip_ref: (i, 0))),
        out_shape=jax.ShapeDtypeStruct((R, W), F32), compiler_params=_params(("parallel",)),
    )(chip, p, recv)


def _reduce_scatter(gs):
    x, y, c = lax.axis_index("x"), lax.axis_index("y"), lax.axis_index("c")
    core = jnp.reshape(c, (1,)).astype(jnp.int32)
    chip = jnp.reshape(2 * x + y, (1,)).astype(jnp.int32)
    from_sibling = _send_to_sibling(gs, "rs_sibling")
    chip_sums = [_add_sibling(g, r, core) for g, r in zip(gs, from_sibling)]
    from_chips = _send_to_chips(chip_sums, "rs_chips")
    return [_add_chips(p, r, chip) for p, r in zip(chip_sums, from_chips)]


def _rope_tables(seq, n_ctx):
    rows = seq // GRID_W
    row = jnp.repeat(jnp.arange(rows, dtype=F32), GRID_W)
    col = jnp.tile(jnp.arange(GRID_W, dtype=F32), rows)
    inv = ROPE_BASE ** (-jnp.arange(ROPE_FREQS, dtype=F32) * 2.0 / (ROPE // 2))
    ang_r, ang_c = row[:, None] * inv, col[:, None] * inv
    one, zero = jnp.ones((seq, ROPE), F32), jnp.zeros((seq, ROPE), F32)
    cos = jnp.concatenate([jnp.cos(ang_r), jnp.cos(ang_r), jnp.cos(ang_c), jnp.cos(ang_c), one], axis=1)
    sin = jnp.concatenate([-jnp.sin(ang_r), jnp.sin(ang_r), -jnp.sin(ang_c), jnp.sin(ang_c), zero], axis=1)
    cos = jnp.concatenate([jnp.ones((n_ctx, LANES), F32), cos], axis=0)
    sin = jnp.concatenate([jnp.zeros((n_ctx, LANES), F32), sin], axis=0)
    return cos, sin


def _shared(v):
    return v.reshape((1, 1, -1)) if v.ndim == 1 else v.reshape((1,) + v.shape)


Z_BOUNDS = (Z_Q, Z_K, Z_V, Z_G, Z_LR, Z_A, Z_GATE, Z_BG, Z_CG, Z_H, Z_CQ, Z_CKV, Z_KR, Z_END)
PLAIN = ("w_in", "w_qn", "w_qr", "w_kn", "w_v")
COLS = ("ffn_w1", "ffn_w3")
ROWS = ("w_out", "ffn_w2")


def _layer(t, p, car, plain, G, layer, mod_l, mod_c, n_ctx, tables):
    L, D = t.shape
    seg = lambda i: jnp.stack([mod_c[i * D:(i + 1) * D], mod_l[i * D:(i + 1) * D]]).reshape(2, 1, D)
    sh1, sc1, g1, sh2, sc2, g2 = (seg(i) for i in range(6))
    dense = lambda a, n: mm(a, plain[n], car[n])

    (h,) = _rowop(_f_norm_mod, "norm_mod", n_ctx, out_dtypes=(BF16,))((t,), (), (_shared(p["norm1_g"]), sc1, sh1))
    q, k, v, gate, lr, conf_a, conf_gate, sc_b, sc_c, sc_h, cq, ckv, kr = _split_cols(dense(h, "w_in"), Z_BOUNDS)

    up = p["gla_fg_up"]
    up_f = jnp.pad(up[0], ((0, LANES - GATE_RANK), (0, 0)))
    up_b = jnp.pad(up[1], ((GATE_RANK, LANES - 2 * GATE_RANK), (0, 0)))
    logd_f, logd_b = _rowop(_f_gla_gate, "gla_gate")(
        (lr,), (), (_shared(up_f), _shared(up_b), _shared(p["gla_fg_b"][0]), _shared(p["gla_fg_b"][1])))
    o_f = _make_gla(n_ctx, False)(q, k, v, logd_f)
    o_b = _make_gla(n_ctx, True)(q, k, v, logd_b)
    (gla,) = _rowop(_f_gla_finish, "gla_finish", lane_block=HEAD_W, out_dtypes=(BF16,))(
        (o_f, o_b, gate), (), (_shared(p["gla_onorm_g"]),))

    conv = _make_conv(n_ctx)
    (u,) = _rowop(_f_glu, "glu")((conf_a, conf_gate), (), ())
    (conf,) = _rowop(_f_ln_silu, "ln_silu", out_dtypes=(BF16,))(
        (conv(u, p["conf_dw"]),), (), (_shared(p["conf_dw_b"]), _shared(p["conf_ln_g"]), _shared(p["conf_ln_b"])))

    (ch,) = _rowop(_f_mul, "mul")((sc_c, sc_h), (), ())
    (sconv,) = _rowop(_f_mul, "mul_out", out_dtypes=(BF16,))((sc_b, conv(ch, p["sc_dw"])), (), ())

    (cq,) = _rowop(_f_rms, "rms")((cq,), (), (_shared(p["mla_q_norm_g"]),))
    (ckv,) = _rowop(_f_rms, "rms")((ckv,), (), (_shared(p["mla_kv_norm_g"]),))
    rope = _rowop(_f_rope, "rope", lane_block=LANES)
    (qr,) = rope((dense(cq, "w_qr"),), tables, ())
    (kr,) = rope((kr,), tables, ())
    mla = _make_attention(n_ctx)(dense(cq, "w_qn"), qr, dense(ckv, "w_kn"), kr, dense(ckv, "w_v"))

    o = _make_mm_rows(layer)(jnp.concatenate([gla, conf, sconv, mla], axis=1), G["w_out"], car["w_out"])
    t1, h2 = _rowop(_f_resid_norm_mod, "resid_norm_mod", n_ctx, out_dtypes=(F32, BF16))(
        (t, o), (), (g1, _shared(p["norm2_g"]), sc2, sh2))
    a1 = _make_mm_cols(layer)(h2, G["ffn_w1"], car["ffn_w1"])
    a3 = _make_mm_cols(layer)(h2, G["ffn_w3"], car["ffn_w3"])
    n_ff = a1.shape[2]
    (act,) = _rowop(_f_swiglu, "swiglu", tile=1024, out_dtypes=(BF16,))(
        (a1.reshape(N_DEV * L, n_ff), a3.reshape(N_DEV * L, n_ff)), (), ())
    f = _make_mm_rows(layer)(act.reshape(N_DEV, L, n_ff), G["ffn_w2"], car["ffn_w2"])
    (t2,) = _rowop(_f_resid, "resid", n_ctx)((t1, f), (), (g2,))
    return t2


def _trunk(t, smalls, cars, plains, G, mods_l, mods_c, n_ctx, seq):
    tables = _rope_tables(seq, n_ctx)
    for layer, (p, car, plain, mod_l, mod_c) in enumerate(zip(smalls, cars, plains, mods_l, mods_c)):
        t = _layer(t, p, car, plain, G, layer, mod_l, mod_c, n_ctx, tables)
    return t


def _plain_weights(G, layer, D):
    full = lambda n: jnp.concatenate([G[n][d, layer] for d in range(N_DEV)], axis=1)
    w_in = full("w_in")
    w_in = jnp.concatenate([w_in[:, :Z_LR + 2 * GATE_RANK], jnp.zeros((D, Z_A - Z_LR - 2 * GATE_RANK), BF16),
                            w_in[:, Z_LR + 2 * GATE_RANK:], jnp.zeros((D, Z_END - Z_KR - ROPE), BF16)], axis=1)
    w_uq = full("mla_w_uq").reshape(Q_RANK, HEADS, HEAD_W + ROPE)
    w_ukv = full("mla_w_ukv").reshape(KV_RANK, HEADS, 2 * HEAD_W)
    return {"w_in": w_in,
            "w_qn": w_uq[:, :, :HEAD_W].reshape(Q_RANK, MIX_W),
            "w_qr": jnp.pad(w_uq[:, :, HEAD_W:], ((0, 0), (0, 0), (0, LANES - ROPE))).reshape(Q_RANK, HEADS * LANES),
            "w_kn": w_ukv[:, :, :HEAD_W].reshape(KV_RANK, MIX_W),
            "w_v": w_ukv[:, :, HEAD_W:].reshape(KV_RANK, MIX_W)}


def _col_slabs(full):
    n = full.shape[1] // N_DEV
    return jnp.stack([full[:, d * n:(d + 1) * n] for d in range(N_DEV)])


def _shard_grads(d_car):
    d_in = d_car["w_in"]
    d_in = jnp.concatenate([d_in[:, :Z_LR + 2 * GATE_RANK], d_in[:, Z_A:Z_KR + ROPE]], axis=1)
    by_head = lambda g: g.reshape(g.shape[0], HEADS, -1)
    d_uq = jnp.concatenate([by_head(d_car["w_qn"]), by_head(d_car["w_qr"])[:, :, :ROPE]], axis=2)
    d_ukv = jnp.concatenate([by_head(d_car["w_kn"]), by_head(d_car["w_v"])], axis=2)
    out = {"w_in": _col_slabs(d_in), "mla_w_uq": _col_slabs(d_uq.reshape(Q_RANK, -1)),
           "mla_w_ukv": _col_slabs(d_ukv.reshape(KV_RANK, -1))}
    out.update({n: d_car[n] for n in COLS + ROWS})
    return out


BIG = ("w_in", "w_out", "ffn_w1", "ffn_w3", "ffn_w2", "mla_w_uq", "mla_w_ukv")
SMALL_SHARED = ("norm1_g", "gla_onorm_g", "conf_dw_b", "conf_ln_g", "conf_ln_b", "mla_q_norm_g", "mla_kv_norm_g",
                "norm2_g")
SMALL_SHARDED = ("gla_fg_up", "gla_fg_b", "conf_dw", "sc_dw")
WEIGHTS = ("c_ctx", "norm1_g", "w_mod", "b_mod", "w_in", "gla_fg_up", "gla_fg_b", "gla_onorm_g", "conf_dw",
           "conf_dw_b", "conf_ln_g", "conf_ln_b", "sc_dw", "mla_q_norm_g", "mla_kv_norm_g", "mla_w_uq", "mla_w_ukv",
           "w_out", "norm2_g", "ffn_w1", "ffn_w3", "ffn_w2", "final_norm_g")


def _gather_last(pieces8):
    moved = jnp.moveaxis(pieces8, 0, -2)
    return moved.reshape(moved.shape[:-2] + (-1,))


def _sum_devices(x8, name):
    shape = x8.shape[1:]
    return _sum_leading(x8.reshape(N_DEV, -1, shape[-1]), name).reshape(shape)


def _step(w, m, v, x, c, ctx, loss_target):
    depth = w["norm1_g"].shape[0]
    seq, D = x.shape[1], x.shape[2]
    n_ctx = ctx.shape[1]
    me = 4 * lax.axis_index("x") + 2 * lax.axis_index("y") + lax.axis_index("c")

    G = dict(zip(BIG, _all_gather([w[n].astype(BF16) for n in BIG], "gather_weights")))
    small8 = _all_gather([c] + [w[n] for n in SMALL_SHARDED], "gather_small")
    c_all = small8[0].reshape(N_DEV, D)
    small_full = {n: _gather_last(g) for n, g in zip(SMALL_SHARDED, small8[1:])}

    rows = jnp.concatenate([c_all, w["c_ctx"][None], jnp.zeros((16 - N_DEV - 1, D), F32)])
    (act,) = _rowop(_f_silu, "silu")((rows,), (), ())
    n_mod = w["w_mod"].shape[2]
    b_mine = lax.dynamic_slice_in_dim(w["b_mod"], me * n_mod, n_mod, axis=1)
    mod_cols = [_rowop(_f_add_bias, "add_bias")((_matmul(act, w["w_mod"][i], "nn", "mod_fwd"),), (),
                                                (_shared(b_mine[i]),))[0] for i in range(depth)]
    (mods8,) = _all_gather([jnp.stack(mod_cols)], "gather_mod")
    mods = jnp.moveaxis(mods8, 0, 2).reshape(depth, 16, N_DEV * n_mod)
    mods_l = [lax.dynamic_index_in_dim(mods[i], me, 0, keepdims=False) for i in range(depth)]
    mods_c = [mods[i, N_DEV] for i in range(depth)]

    smalls, plains, cars = [], [], []
    for i in range(depth):
        p = {n: w[n][i] for n in SMALL_SHARED}
        p.update({n: small_full[n][i] for n in SMALL_SHARDED})
        smalls.append(p)
        plains.append(_plain_weights(G, i, D))
        car = {n: lax.empty(a.shape, F32) for n, a in plains[i].items()}
        car.update({n: lax.empty((N_DEV,) + G[n].shape[2:], F32) for n in COLS + ROWS})
        cars.append(car)
    t0 = jnp.concatenate([ctx[0], x[0]], axis=0)
    t_out, vjp = jax.vjp(lambda t, ps, cs, ml, mc: _trunk(t, ps, cs, plains, G, ml, mc, n_ctx, seq),
                         t0, smalls, cars, mods_l, mods_c)
    loss_lanes, d_out, d_final_g = _loss_call(t_out[n_ctx:], w["final_norm_g"][None], loss_target[0])
    d_t0, d_smalls, d_cars, d_mods_l, d_mods_c = vjp(jnp.concatenate([jnp.zeros((n_ctx, D), F32), d_out], axis=0))
    grad_x = d_t0[n_ctx:][None]

    grads = {n: [] for n in BIG}
    for i in range(depth):
        sharded = _shard_grads(d_cars[i])
        for n, g in zip(BIG, _reduce_scatter([sharded[n] for n in BIG])):
            grads[n].append(g)
    grads = {n: jnp.stack(g) for n, g in grads.items()}

    names = SMALL_SHARED + SMALL_SHARDED
    d_mod = jnp.stack([jnp.stack([d_mods_l[i], d_mods_c[i]]) for i in range(depth)])
    parts = [loss_lanes, d_final_g, d_mod] + [jnp.stack([d_smalls[i][n] for i in range(depth)]) for n in names]
    parts8 = _all_gather(parts, "gather_partials")
    summed = [_sum_devices(p8, "sum_partials") for p8 in parts8]
    (loss_row,) = _rowop(lambda a: (jnp.sum(a, axis=-1, keepdims=True) + jnp.zeros_like(a),), "loss_sum")(
        (summed[0],), (), ())
    loss = loss_row[0, 0]
    grads["final_norm_g"] = summed[1].reshape(D)
    for n, g in zip(names, summed[3:]):
        if n in SMALL_SHARDED:
            g = lax.dynamic_slice_in_dim(g, me * w[n].shape[-1], w[n].shape[-1], axis=g.ndim - 1)
        grads[n] = g

    d_mod8, d_mod_c = parts8[2], summed[2][:, 1]
    grads["b_mod"] = _rowop(lambda a, b: (a + b,), "add")((summed[2][:, 0], d_mod_c), (), ())[0]
    d_rows = jnp.concatenate([jnp.moveaxis(d_mod8[:, :, 0], 0, 1), d_mod_c[:, None],
                              jnp.zeros((depth, 16 - N_DEV - 1, 6 * D), F32)], axis=1)
    d_rows = lax.dynamic_slice_in_dim(d_rows, me * n_mod, n_mod, axis=2)
    grads["w_mod"] = jnp.stack([_matmul(act, d_rows[i], "tn", "mod_dw") for i in range(depth)])
    d_act = _matmul(d_rows.transpose(1, 0, 2).reshape(16, depth * n_mod),
                    w["w_mod"].transpose(1, 0, 2).reshape(D, depth * n_mod), "nt", "mod_dact")
    (d_act8,) = _all_gather([d_act], "gather_dact")
    (d_rows_in,) = _rowop(_f_mul_silu_grad, "silu_grad")((_sum_devices(d_act8, "sum_dact"), rows), (), ())
    grads["c_ctx"] = d_rows_in[N_DEV]

    outs = {n: _adamw(w[n], grads[n], m[n], v[n]) for n in WEIGHTS}
    return (loss, grad_x, *[grads[n] for n in WEIGHTS], *[outs[n][0] for n in WEIGHTS],
            *[outs[n][1] for n in WEIGHTS], *[outs[n][2] for n in WEIGHTS])


def kernel(x, c, ctx, c_ctx, norm1_g, w_mod, b_mod, w_in, gla_fg_up, gla_fg_b, gla_onorm_g, conf_dw, conf_dw_b, conf_ln_g, conf_ln_b, sc_dw, mla_q_norm_g, mla_kv_norm_g, mla_w_uq, mla_w_ukv, w_out, norm2_g, ffn_w1, ffn_w3, ffn_w2, final_norm_g, loss_target, m_c_ctx, m_norm1_g, m_w_mod, m_b_mod, m_w_in, m_gla_fg_up, m_gla_fg_b, m_gla_onorm_g, m_conf_dw, m_conf_dw_b, m_conf_ln_g, m_conf_ln_b, m_sc_dw, m_mla_q_norm_g, m_mla_kv_norm_g, m_mla_w_uq, m_mla_w_ukv, m_w_out, m_norm2_g, m_ffn_w1, m_ffn_w3, m_ffn_w2, m_final_norm_g, v_c_ctx, v_norm1_g, v_w_mod, v_b_mod, v_w_in, v_gla_fg_up, v_gla_fg_b, v_gla_onorm_g, v_conf_dw, v_conf_dw_b, v_conf_ln_g, v_conf_ln_b, v_sc_dw, v_mla_q_norm_g, v_mla_kv_norm_g, v_mla_w_uq, v_mla_w_ukv, v_w_out, v_norm2_g, v_ffn_w1, v_ffn_w3, v_ffn_w2, v_final_norm_g):
    w = dict(c_ctx=c_ctx, norm1_g=norm1_g, w_mod=w_mod, b_mod=b_mod, w_in=w_in, gla_fg_up=gla_fg_up, gla_fg_b=gla_fg_b, gla_onorm_g=gla_onorm_g, conf_dw=conf_dw, conf_dw_b=conf_dw_b, conf_ln_g=conf_ln_g, conf_ln_b=conf_ln_b, sc_dw=sc_dw, mla_q_norm_g=mla_q_norm_g, mla_kv_norm_g=mla_kv_norm_g, mla_w_uq=mla_w_uq, mla_w_ukv=mla_w_ukv, w_out=w_out, norm2_g=norm2_g, ffn_w1=ffn_w1, ffn_w3=ffn_w3, ffn_w2=ffn_w2, final_norm_g=final_norm_g)
    m = dict(c_ctx=m_c_ctx, norm1_g=m_norm1_g, w_mod=m_w_mod, b_mod=m_b_mod, w_in=m_w_in, gla_fg_up=m_gla_fg_up, gla_fg_b=m_gla_fg_b, gla_onorm_g=m_gla_onorm_g, conf_dw=m_conf_dw, conf_dw_b=m_conf_dw_b, conf_ln_g=m_conf_ln_g, conf_ln_b=m_conf_ln_b, sc_dw=m_sc_dw, mla_q_norm_g=m_mla_q_norm_g, mla_kv_norm_g=m_mla_kv_norm_g, mla_w_uq=m_mla_w_uq, mla_w_ukv=m_mla_w_ukv, w_out=m_w_out, norm2_g=m_norm2_g, ffn_w1=m_ffn_w1, ffn_w3=m_ffn_w3, ffn_w2=m_ffn_w2, final_norm_g=m_final_norm_g)
    v = dict(c_ctx=v_c_ctx, norm1_g=v_norm1_g, w_mod=v_w_mod, b_mod=v_b_mod, w_in=v_w_in, gla_fg_up=v_gla_fg_up, gla_fg_b=v_gla_fg_b, gla_onorm_g=v_gla_onorm_g, conf_dw=v_conf_dw, conf_dw_b=v_conf_dw_b, conf_ln_g=v_conf_ln_g, conf_ln_b=v_conf_ln_b, sc_dw=v_sc_dw, mla_q_norm_g=v_mla_q_norm_g, mla_kv_norm_g=v_mla_kv_norm_g, mla_w_uq=v_mla_w_uq, mla_w_ukv=v_mla_w_ukv, w_out=v_w_out, norm2_g=v_norm2_g, ffn_w1=v_ffn_w1, ffn_w3=v_ffn_w3, ffn_w2=v_ffn_w2, final_norm_g=v_final_norm_g)
    return _step(w, m, v, x, c, ctx, loss_target)
```

```python
import functools
import math

import jax
import jax.numpy as jnp
from jax import lax
from jax.experimental import pallas as pl
from jax.experimental.pallas import tpu as pltpu

F32 = jnp.float32
BF16 = jnp.bfloat16
MESH = pl.DeviceIdType.MESH
N_DEV = 8

EPS = 1e-6
GRID_W = 64
HEADS = 4
HEAD_W = 128
MIX_W = HEADS * HEAD_W
GATE_RANK = 16
GATE_NORM = 16.0
GLA_CHUNK = 128
CONF_K = 31
SC_K = 3
Q_RANK = 384
KV_RANK = 128
ROPE = 64
ROPE_FREQS = 16
ROPE_BASE = 10000.0
ATT_SCALE = (HEAD_W + ROPE) ** -0.5
CONV_HALO = 16

ADAM_LR = 0.001
ADAM_B1 = 0.9
ADAM_B2 = 0.999
ADAM_EPS = 1e-08
ADAM_WD = 0.01
ADAM_STEP = 10

LANES = 128
VMEM_LIMIT = 56 * 2 ** 20
ROW_BLOCK_BYTES = 10 * 2 ** 20

Z_Q, Z_K, Z_V, Z_G, Z_LR, Z_A, Z_GATE, Z_BG, Z_CG, Z_H, Z_CQ, Z_CKV, Z_KR, Z_END = (
    0, 512, 1024, 1536, 2048, 2176, 2688, 3200, 3712, 4224, 4736, 5120, 5248, 5376)
IN_W = 5216


def _pcall(body, **kw):
    return pl.pallas_call(body, **kw)


def _params(sem=None):
    return pltpu.CompilerParams(dimension_semantics=sem, vmem_limit_bytes=VMEM_LIMIT)


def _pick(dim, cap, mult):
    d = (min(cap, dim) // mult) * mult
    while d >= mult:
        if dim % d == 0:
            return d
        d -= mult
    return dim


def _mm_call(name, a, b, out_shape, grid, a_spec, b_spec, o_spec, dims, k_axis=None, once_axis=None, out_dtype=F32,
             gather=()):
    a_blk = tuple(d for d in a_spec.block_shape if d is not None)
    o_blk = tuple(d for d in o_spec.block_shape if d is not None)
    if a.dtype == BF16:
        once_axis = None
    scratch = ([pltpu.VMEM((math.prod(o_blk[:-1]), o_blk[-1]), F32)] if k_axis is not None else []) + (
        [pltpu.VMEM(a_blk, BF16)] if once_axis is not None else [])
    nk = grid[k_axis] if k_axis is not None else 1
    ng = len(gather)

    def body(*refs):
        a_ref, b_ref, o_ref, scr = refs[0], refs[1], refs[2 + ng], refs[3 + 2 * ng:]
        if ng:
            exchange = _Gather(refs[2:2 + ng], refs[3 + ng:3 + 2 * ng], *scr[-3:])
            scr = scr[:-3]
            steps = [pl.program_id(ax) for ax in range(len(grid))]
            at_first = functools.reduce(jnp.logical_and, [s == 0 for s in steps])
            at_last = functools.reduce(jnp.logical_and, [s == g - 1 for s, g in zip(steps, grid)])
            pl.when(at_first)(exchange.start)
        product(a_ref, b_ref, o_ref, scr)
        if ng:
            pl.when(at_last)(exchange.finish)

    def product(a_ref, b_ref, o_ref, scr):
        if once_axis is not None:
            a_bf = scr[-1]

            @pl.when(pl.program_id(once_axis) == 0)
            def _():
                a_bf[...] = a_ref[...].astype(BF16)

            av = a_bf[...]
        else:
            av = a_ref[...].astype(BF16)
        bv = b_ref[...].astype(BF16)
        if bv.ndim == 3:
            bv = bv.reshape(-1, bv.shape[-1])
        prod = lax.dot_general(av, bv, dims, preferred_element_type=F32)
        if k_axis is None:
            o_ref[...] = prod.astype(o_ref.dtype).reshape(o_ref.shape)
        else:
            acc, k = scr[0], pl.program_id(k_axis)

            @pl.when(k == 0)
            def _():
                acc[...] = prod

            @pl.when(k != 0)
            def _():
                acc[...] += prod

            @pl.when(k == nk - 1)
            def _():
                o_ref[...] = acc[...].astype(o_ref.dtype).reshape(o_ref.shape)

    outs = _pcall(
        body, name=name, grid=grid, in_specs=[a_spec, b_spec] + [HBM] * ng, out_specs=[o_spec] + [HBM] * ng,
        out_shape=[jax.ShapeDtypeStruct(out_shape, out_dtype)]
        + [jax.ShapeDtypeStruct((N_DEV,) + g.shape, g.dtype) for g in gather],
        scratch_shapes=scratch + (_Gather.scratch(ng) if ng else []),
        compiler_params=_params(("arbitrary",) * len(grid)),
    )(a, b, *gather)
    return (outs[0], tuple(outs[1:])) if ng else outs[0]


NN = (((1,), (0,)), ((), ()))
NT = (((1,), (1,)), ((), ()))
TN = (((0,), (0,)), ((), ()))


def _matmul(a, b, mode, name, out_dtype=F32, gather=()):
    if mode == "nn":
        (M, K), (_, N) = a.shape, b.shape
    elif mode == "nt":
        (M, K), (N, _) = a.shape, b.shape
    else:
        (K, M), (_, N) = a.shape, b.shape
    if mode == "nn":
        tm, tn, tk = _pick(M, 1088, 16), _pick(N, 768, LANES), _pick(K, 2048, LANES)
    elif mode == "nt":
        tm, tn, tk = _pick(M, 1088, 16), _pick(N, 2048, LANES), _pick(K, 1024, LANES)
    else:
        tm, tn, tk = _pick(M, 2048, LANES), _pick(N, 768, LANES), _pick(K, 1088, 16)
    if mode == "nn":
        a_spec = pl.BlockSpec((tm, tk), lambda i, j, k: (i, k))
        b_spec = pl.BlockSpec((tk, tn), lambda i, j, k: (k, j))
    elif mode == "nt":
        a_spec = pl.BlockSpec((tm, tk), lambda i, j, k: (i, k))
        b_spec = pl.BlockSpec((tn, tk), lambda i, j, k: (j, k))
    else:
        a_spec = pl.BlockSpec((tk, tm), lambda i, j, k: (k, i))
        b_spec = pl.BlockSpec((tk, tn), lambda i, j, k: (k, j))
    return _mm_call(name, a, b, (M, N), (M // tm, N // tn, K // tk), a_spec, b_spec,
                    pl.BlockSpec((tm, tn), lambda i, j, k: (i, j)), {"nn": NN, "nt": NT, "tn": TN}[mode], k_axis=2,
                    out_dtype=out_dtype, gather=gather)


def _with_gathered(result, nxt):
    return result if nxt else (result, ())


def _no_grads(res, nxt):
    return jnp.zeros_like(res), tuple(jnp.zeros_like(t) for t in nxt)


@jax.custom_vjp
def mm(a, w, carrier, nxt):
    return _with_gathered(_matmul(a, w, "nn", "mm_fwd", gather=nxt), nxt)


def _mm_fwd(a, w, carrier, nxt):
    return mm(a, w, carrier, nxt), (a, w, nxt)


def _mm_bwd(res, cts):
    a, w, nxt = res
    dc = cts[0]
    no_w, no_nxt = _no_grads(w, nxt)
    return _matmul(dc, w, "nt", "mm_da", a.dtype), no_w, _matmul(a, dc, "tn", "mm_dw", BF16), no_nxt


mm.defvjp(_mm_fwd, _mm_bwd)


def _make_mm_cols(layer):
    def forward(a, G, nxt):
        (M, K), n = a.shape, G.shape[3]
        tm = _pick(M, 1088, 16)
        return _with_gathered(_mm_call(
            "mmc_fwd", a, G, (N_DEV, M, n), (M // tm, N_DEV),
            pl.BlockSpec((tm, K), lambda i, d: (i, 0)),
            pl.BlockSpec((None, None, K, n), lambda i, d: (d, layer, 0, 0)),
            pl.BlockSpec((None, tm, n), lambda i, d: (d, i, 0)), NN, once_axis=1, out_dtype=BF16, gather=nxt), nxt)

    def grad_a(do, G, dtype):
        (_, M, n), K = do.shape, G.shape[2]
        tm, tn = _pick(M, 1088, 16), _pick(K, 1024, LANES)
        return _mm_call("mmc_da", do, G, (M, K), (M // tm, K // tn, N_DEV),
                        pl.BlockSpec((None, tm, n), lambda i, j, d: (d, i, 0)),
                        pl.BlockSpec((None, None, tn, n), lambda i, j, d: (d, layer, j, 0)),
                        pl.BlockSpec((tm, tn), lambda i, j, d: (i, j)), NT, k_axis=2, out_dtype=dtype)

    def grad_w(a, do):
        (M, K), n = a.shape, do.shape[2]
        tm, tk = _pick(K, 2048, LANES), _pick(M, 1088, 16)
        return _mm_call("mmc_dw", a, do, (N_DEV, K, n), (N_DEV, K // tm, M // tk),
                        pl.BlockSpec((tk, tm), lambda d, i, k: (k, i)),
                        pl.BlockSpec((None, tk, n), lambda d, i, k: (d, k, 0)),
                        pl.BlockSpec((None, tm, n), lambda d, i, k: (d, i, 0)), TN, k_axis=2, out_dtype=BF16)

    @jax.custom_vjp
    def f(a, G, carrier, nxt):
        return forward(a, G, nxt)

    def f_fwd(a, G, carrier, nxt):
        return forward(a, G, nxt), (a, G, nxt)

    def f_bwd(res, cts):
        a, G, nxt = res
        no_g, no_nxt = _no_grads(G, nxt)
        return grad_a(cts[0], G, a.dtype), no_g, grad_w(a, cts[0]), no_nxt

    f.defvjp(f_fwd, f_bwd)
    return f


def _make_mm_rows(layer):
    def forward(a, G, nxt):
        M, (r, N) = a.shape[-2], G.shape[2:]
        tm, tn = _pick(M, 1088, 16), _pick(N, 1024, LANES)
        if a.ndim == 2:
            return _with_gathered(_mm_call(
                "mmr2_fwd", a, G, (M, N), (M // tm, N // tn),
                pl.BlockSpec((tm, N_DEV * r), lambda i, j: (i, 0)),
                pl.BlockSpec((N_DEV, None, r, tn), lambda i, j: (0, layer, 0, j)),
                pl.BlockSpec((tm, tn), lambda i, j: (i, j)), NN, gather=nxt), nxt)
        return _with_gathered(_mm_call(
            "mmr_fwd", a, G, (M, N), (M // tm, N // tn, N_DEV),
            pl.BlockSpec((None, tm, r), lambda i, j, d: (d, i, 0)),
            pl.BlockSpec((None, None, r, tn), lambda i, j, d: (d, layer, 0, j)),
            pl.BlockSpec((tm, tn), lambda i, j, d: (i, j)), NN, k_axis=2, gather=nxt), nxt)

    def grad_a(dc, G, like):
        (M, N), r = dc.shape, G.shape[2]
        tm = _pick(M, 1088, 16)
        if like.ndim == 2:
            return _mm_call("mmr2_da", dc, G, like.shape, (M // tm,),
                            pl.BlockSpec((tm, N), lambda i: (i, 0)),
                            pl.BlockSpec((N_DEV, None, r, N), lambda i: (0, layer, 0, 0)),
                            pl.BlockSpec((tm, N_DEV * r), lambda i: (i, 0)), NT, out_dtype=like.dtype)
        return _mm_call("mmr_da", dc, G, like.shape, (M // tm, N_DEV),
                        pl.BlockSpec((tm, N), lambda i, d: (i, 0)),
                        pl.BlockSpec((None, None, r, N), lambda i, d: (d, layer, 0, 0)),
                        pl.BlockSpec((None, tm, r), lambda i, d: (d, i, 0)), NT, once_axis=1, out_dtype=like.dtype)

    def grad_w(a, dc):
        (M, N), tk = dc.shape, _pick(dc.shape[0], 1088, 16)
        if a.ndim == 2:
            r, tn = a.shape[1] // N_DEV, _pick(N, 1024, LANES)
            return _mm_call("mmr2_dw", a, dc, (N_DEV, r, N), (N // tn, M // tk),
                            pl.BlockSpec((tk, N_DEV * r), lambda j, k: (k, 0)),
                            pl.BlockSpec((tk, tn), lambda j, k: (k, j)),
                            pl.BlockSpec((N_DEV, r, tn), lambda j, k: (0, 0, j)), TN, k_axis=1, out_dtype=BF16)
        r, tn = a.shape[2], _pick(N, 2048, LANES)
        return _mm_call("mmr_dw", a, dc, (N_DEV, r, N), (N_DEV, N // tn, M // tk),
                        pl.BlockSpec((None, tk, r), lambda d, j, k: (d, k, 0)),
                        pl.BlockSpec((tk, tn), lambda d, j, k: (k, j)),
                        pl.BlockSpec((None, r, tn), lambda d, j, k: (d, 0, j)), TN, k_axis=2, out_dtype=BF16)

    @jax.custom_vjp
    def f(a, G, carrier, nxt):
        return forward(a, G, nxt)

    def f_fwd(a, G, carrier, nxt):
        return forward(a, G, nxt), (a, G, nxt)

    def f_bwd(res, cts):
        a, G, nxt = res
        dc = cts[0].astype(BF16)
        no_g, no_nxt = _no_grads(G, nxt)
        return grad_a(dc, G, a), no_g, grad_w(a, dc), no_nxt

    f.defvjp(f_fwd, f_bwd)
    return f


@functools.partial(jax.custom_vjp, nondiff_argnums=(1,))
def _split_cols(z, bounds):
    return tuple(z[:, a:b] for a, b in zip(bounds[:-1], bounds[1:]))


def _split_cols_fwd(z, bounds):
    return _split_cols(z, bounds), None


def _split_cols_bwd(bounds, _, cts):
    return (jnp.concatenate(cts, axis=1),)


_split_cols.defvjp(_split_cols_fwd, _split_cols_bwd)


def _rowop(fn, name, n_ctx=0, tile=256, lane_block=None, out_dtypes=None):
    def geometry(rows):
        L, w0 = rows[0].shape
        nj = w0 // lane_block if lane_block else 1
        width = 3 * sum(lane_block or r.shape[1] for r in rows)
        cap = max(16, ROW_BLOCK_BYTES // (4 * width) // 16 * 16)
        tl = _pick(math.gcd(L, n_ctx) if n_ctx else L, min(tile, cap), 16)
        return L, tl, n_ctx // tl, nj

    def block_w(x):
        return lane_block or x.shape[1]

    def row_spec(tl, x):
        if lane_block and x.shape[1] != lane_block:
            return pl.BlockSpec((tl, lane_block), lambda i, j: (i, j))
        return pl.BlockSpec((tl, block_w(x)), lambda i, j: (i, 0))

    def param_spec(p, nct):
        s, r, w = p.shape
        if s == 1:
            return pl.BlockSpec((1, r, w), lambda i, j: (0, 0, 0))
        return pl.BlockSpec((1, r, w), lambda i, j: (jnp.where(i < nct, 0, 1), 0, 0))

    def forward(rows, consts, params):
        L, tl, nct, nj = geometry(rows)
        nr, nc, npar = len(rows), len(consts), len(params)
        outs = jax.eval_shape(
            lambda: fn(*[jnp.zeros((tl, block_w(r)), F32) for r in rows + consts],
                       *[jnp.zeros(p.shape[1:], F32) for p in params]))

        def body(*refs):
            ins = [r[...].astype(F32) for r in refs[:nr + nc]] + [r[0] for r in refs[nr + nc:nr + nc + npar]]
            for o_ref, o in zip(refs[nr + nc + npar:], fn(*ins)):
                o_ref[...] = o.astype(o_ref.dtype)

        return _pcall(
            body, name=name + "_fwd", grid=(L // tl, nj),
            in_specs=[row_spec(tl, r) for r in rows + consts] + [param_spec(p, nct) for p in params],
            out_specs=[pl.BlockSpec((tl, o.shape[1]), lambda i, j: (i, j)) for o in outs],
            out_shape=[jax.ShapeDtypeStruct((L, o.shape[1] * nj), dt)
                       for o, dt in zip(outs, out_dtypes or (F32,) * len(outs))],
            compiler_params=_params(("parallel", "parallel")),
        )(*rows, *consts, *params)

    def backward(rows, consts, params, cts):
        L, tl, nct, nj = geometry(rows)
        nr, nc, npar, no = len(rows), len(consts), len(params), len(cts)

        def body(*refs):
            i, j = pl.program_id(0), pl.program_id(1)
            rv = [r[...].astype(F32) for r in refs[:nr]]
            cv = [r[...] for r in refs[nr:nr + nc]]
            pv = [r[0] for r in refs[nr + nc:nr + nc + npar]]
            ct = tuple(r[...].astype(F32) for r in refs[nr + nc + npar:nr + nc + npar + no])
            out_refs = refs[nr + nc + npar + no:]
            _, vjp = jax.vjp(lambda *d: tuple(fn(*d[:nr], *cv, *d[nr:])), *rv, *pv)
            grads = vjp(ct)
            for ref, g in zip(out_refs[:nr], grads[:nr]):
                ref[...] = g.astype(ref.dtype)
            for ref, g, p in zip(out_refs[nr:], grads[nr:], params):
                first_row = (i == 0) if (p.shape[0] == 1 or nct == 0) else ((i == 0) | (i == nct))
                first = jnp.logical_and(first_row, j == 0)

                @pl.when(first)
                def _():
                    ref[0] = g

                @pl.when(jnp.logical_not(first))
                def _():
                    ref[0] += g

        outs = _pcall(
            body, name=name + "_bwd", grid=(L // tl, nj),
            in_specs=[row_spec(tl, r) for r in rows + consts] + [param_spec(p, nct) for p in params]
            + [pl.BlockSpec((tl, c.shape[1] // nj), lambda i, j: (i, j)) for c in cts],
            out_specs=[row_spec(tl, r) for r in rows] + [param_spec(p, nct) for p in params],
            out_shape=[jax.ShapeDtypeStruct(r.shape, r.dtype) for r in rows]
            + [jax.ShapeDtypeStruct(p.shape, F32) for p in params],
            compiler_params=_params(("arbitrary", "arbitrary")),
        )(*rows, *consts, *params, *cts)
        return tuple(outs[:nr]), tuple(outs[nr:])

    @jax.custom_vjp
    def op(rows, consts, params):
        return tuple(forward(rows, consts, params))

    def op_fwd(rows, consts, params):
        return tuple(forward(rows, consts, params)), (rows, consts, params)

    def op_bwd(res, cts):
        rows, consts, params = res
        d_rows, d_params = backward(rows, consts, params, tuple(cts))
        return d_rows, tuple(jnp.zeros_like(c) for c in consts), d_params

    op.defvjp(op_fwd, op_bwd)
    return op


def _sigmoid(x):
    return 1.0 / (1.0 + jnp.exp(-x))


def _silu(x):
    return x * _sigmoid(x)


def _log_sigmoid(x):
    return jnp.minimum(x, 0.0) - jnp.log(1.0 + jnp.exp(-jnp.abs(x)))


def _rms(x, g):
    return x * lax.rsqrt(jnp.mean(x * x, axis=-1, keepdims=True) + EPS) * g


def _f_norm_mod(x, g, sc, sh):
    return (_rms(x, g) * (1.0 + sc) + sh,)


def _f_resid_norm_mod(x, o, gate, g, sc, sh):
    x1 = x + gate * o
    return x1, _rms(x1, g) * (1.0 + sc) + sh


def _f_resid(x, o, gate):
    return (x + gate * o,)


def _f_swiglu(a1, a3):
    return (_silu(a1) * a3,)


def _f_gla_gate(lr, up_f, up_b, b_f, b_b):
    dot = functools.partial(jnp.dot, preferred_element_type=F32)
    return (_log_sigmoid(dot(lr, up_f) + b_f) / GATE_NORM, _log_sigmoid(dot(lr, up_b) + b_b) / GATE_NORM)


def _f_gla_finish(o_f, o_b, gate, g):
    return (_rms(o_f + o_b, g) * _silu(gate),)


def _f_glu(a, gate):
    return (a * _sigmoid(gate),)


def _f_ln_silu(u, dw_b, g, b):
    u = u + dw_b
    xc = u - jnp.mean(u, axis=-1, keepdims=True)
    y = xc * lax.rsqrt(jnp.mean(xc * xc, axis=-1, keepdims=True) + EPS)
    return (_silu(y * g + b),)


def _f_mul(a, b):
    return (a * b,)


def _f_rms(x, g):
    return (_rms(x, g),)


def _f_rope(t, cos, sin):
    w = t.shape[1]
    r = lax.broadcasted_iota(jnp.int32, (w, w), 0)
    c = lax.broadcasted_iota(jnp.int32, (w, w), 1)
    perm = (jnp.bitwise_xor(r, ROPE_FREQS) == c).astype(F32)
    partner = jnp.dot(t, perm, precision=lax.Precision.HIGHEST, preferred_element_type=F32)
    return (t * cos + partner * sin,)


def _f_silu(x):
    return (_silu(x),)


def _f_add_bias(x, b):
    return (x + b,)


def _f_mul_silu_grad(d, x):
    _, vjp = jax.vjp(_silu, x)
    return (vjp(d)[0],)


def _conv_geometry(u, n_ctx):
    L, C = u.shape
    tl = _pick(math.gcd(L, n_ctx), 256, 8)
    return L, C, tl, n_ctx // tl, L // tl


def _conv_specs(tl, nt):
    prev = pl.BlockSpec((tl, LANES), lambda c, i: (jnp.maximum(i - 1, 0), c))
    cur = pl.BlockSpec((tl, LANES), lambda c, i: (i, c))
    nxt = pl.BlockSpec((tl, LANES), lambda c, i: (jnp.minimum(i + 1, nt - 1), c))
    return prev, cur, nxt


def _conv_window(prev_ref, cur_ref, next_ref, tl, nct, nt):
    i = pl.program_id(1)
    has_prev = jnp.logical_and(i != 0, i != nct)
    has_next = jnp.logical_and(i != nct - 1, i != nt - 1)
    prev = jnp.where(has_prev, prev_ref[tl - CONV_HALO:tl, :], 0.0)
    nxt = jnp.where(has_next, next_ref[0:CONV_HALO, :], 0.0)
    return jnp.concatenate([prev, cur_ref[...], nxt], axis=0)


def _shifted(window, off, tl):
    n = window.shape[0]
    if off == 0:
        return window[0:tl]
    return pltpu.roll(window, n - off, 0)[0:tl]


def _conv_apply(u, w, n_ctx, flip, name):
    L, C, tl, nct, nt = _conv_geometry(u, n_ctx)
    K = w.shape[0]
    pad = (K - 1) // 2
    prev, cur, nxt = _conv_specs(tl, nt)

    def body(p_ref, c_ref, n_ref, w_ref, o_ref):
        win = _conv_window(p_ref, c_ref, n_ref, tl, nct, nt)
        acc = jnp.zeros((tl, LANES), F32)
        for k in range(K):
            kk = K - 1 - k if flip else k
            acc = acc + _shifted(win, CONV_HALO - pad + k, tl) * w_ref[kk:kk + 1, :]
        o_ref[...] = acc

    return _pcall(
        body, name=name, grid=(C // LANES, nt),
        in_specs=[prev, cur, nxt, pl.BlockSpec((K, LANES), lambda c, i: (0, c))],
        out_specs=cur, out_shape=jax.ShapeDtypeStruct((L, C), F32),
        compiler_params=_params(("parallel", "parallel")),
    )(u, u, u, w)


def _conv_dw(u, dy, K, n_ctx, name):
    L, C, tl, nct, nt = _conv_geometry(u, n_ctx)
    pad = (K - 1) // 2
    prev, cur, nxt = _conv_specs(tl, nt)

    def body(p_ref, c_ref, n_ref, dy_ref, dw_ref):
        i = pl.program_id(1)

        @pl.when(i == 0)
        def _():
            dw_ref[...] = jnp.zeros_like(dw_ref)

        win = _conv_window(p_ref, c_ref, n_ref, tl, nct, nt)
        dy_t = dy_ref[...]
        for k in range(K):
            dw_ref[k:k + 1, :] += jnp.sum(_shifted(win, CONV_HALO - pad + k, tl) * dy_t, axis=0, keepdims=True)

    return _pcall(
        body, name=name, grid=(C // LANES, nt),
        in_specs=[prev, cur, nxt, cur],
        out_specs=pl.BlockSpec((K, LANES), lambda c, i: (0, c)),
        out_shape=jax.ShapeDtypeStruct((K, C), F32),
        compiler_params=_params(("parallel", "arbitrary")),
    )(u, u, u, dy)


def _make_conv(n_ctx):
    @jax.custom_vjp
    def conv(u, w):
        return _conv_apply(u, w, n_ctx, False, "conv_fwd")

    def conv_fwd(u, w):
        return _conv_apply(u, w, n_ctx, False, "conv_fwd"), (u, w)

    def conv_bwd(res, dy):
        u, w = res
        return _conv_apply(dy, w, n_ctx, True, "conv_du"), _conv_dw(u, dy, w.shape[0], n_ctx, "conv_dw")

    conv.defvjp(conv_fwd, conv_bwd)
    return conv


def _gla_chunk(q, k, v, g, st, reverse):
    C = q.shape[0]
    r = lax.broadcasted_iota(jnp.int32, (C, C), 0)
    c = lax.broadcasted_iota(jnp.int32, (C, C), 1)
    seen = (r <= c) if reverse else (r >= c)
    dot = functools.partial(lax.dot_general, preferred_element_type=F32, precision=lax.Precision.DEFAULT)
    bcum = lax.dot_general(seen.astype(F32), g, (((1,), (0,)), ((), ())), preferred_element_type=F32,
                           precision=lax.Precision.HIGHEST)
    total = jnp.sum(g, axis=0, keepdims=True)
    a = q * (HEAD_W ** -0.5) * jnp.exp(bcum)
    scores = jnp.where(seen, dot(a, k * jnp.exp(-bcum), (((1,), (1,)), ((), ()))), 0.0)
    o = dot(a, st, (((1,), (1,)), ((), ()))) + dot(scores, v, (((1,), (0,)), ((), ())))
    st_new = st * jnp.exp(total) + dot(v, k * jnp.exp(total - bcum), (((0,), (0,)), ((), ())))
    return o, st_new


def _gla_order(t, nc, ncc, reverse):
    if not reverse:
        return t
    return jnp.where(t < ncc, ncc - 1 - t, ncc + nc - 1 - t)


def _gla_fwd_call(q, k, v, g, n_ctx, reverse):
    L = q.shape[0]
    C = GLA_CHUNK
    nc, ncc = L // C, n_ctx // C
    spec = pl.BlockSpec((C, MIX_W), lambda t: (_gla_order(t, nc, ncc, reverse), 0))

    def body(q_ref, k_ref, v_ref, g_ref, o_ref, s_ref, st):
        @pl.when(pl.program_id(0) == 0)
        def _():
            st[...] = jnp.zeros_like(st)

        for h in range(HEADS):
            hs = slice(h * HEAD_W, (h + 1) * HEAD_W)
            s_ref[h, 0] = st[h]
            o, st_new = _gla_chunk(q_ref[:, hs], k_ref[:, hs], v_ref[:, hs], g_ref[:, hs], st[h], reverse)
            o_ref[:, hs] = o
            st[h] = st_new

    return _pcall(
        body, name="gla_fwd", grid=(nc,), in_specs=[spec] * 4,
        out_specs=[spec, pl.BlockSpec((HEADS, 1, HEAD_W, HEAD_W), lambda t: (0, t, 0, 0))],
        out_shape=[jax.ShapeDtypeStruct((L, MIX_W), F32), jax.ShapeDtypeStruct((HEADS, nc, HEAD_W, HEAD_W), F32)],
        scratch_shapes=[pltpu.VMEM((HEADS, HEAD_W, HEAD_W), F32)],
        compiler_params=_params(("arbitrary",)),
    )(q, k, v, g)


def _gla_bwd_call(q, k, v, g, states, do, n_ctx, reverse):
    L = q.shape[0]
    C = GLA_CHUNK
    nc, ncc = L // C, n_ctx // C
    spec = pl.BlockSpec((C, MIX_W), lambda t: (_gla_order(nc - 1 - t, nc, ncc, reverse), 0))

    def body(q_ref, k_ref, v_ref, g_ref, s_ref, do_ref, dq_ref, dk_ref, dv_ref, dg_ref, dst):
        @pl.when(pl.program_id(0) == 0)
        def _():
            dst[...] = jnp.zeros_like(dst)

        for h in range(HEADS):
            hs = slice(h * HEAD_W, (h + 1) * HEAD_W)
            _, vjp = jax.vjp(functools.partial(_gla_chunk, reverse=reverse),
                             q_ref[:, hs], k_ref[:, hs], v_ref[:, hs], g_ref[:, hs], s_ref[h, 0])
            dq, dk, dv, dg, dst_prev = vjp((do_ref[:, hs], dst[h]))
            dq_ref[:, hs] = dq
            dk_ref[:, hs] = dk
            dv_ref[:, hs] = dv
            dg_ref[:, hs] = dg
            dst[h] = dst_prev

    return _pcall(
        body, name="gla_bwd", grid=(nc,),
        in_specs=[spec] * 4 + [pl.BlockSpec((HEADS, 1, HEAD_W, HEAD_W), lambda t: (0, nc - 1 - t, 0, 0)), spec],
        out_specs=[spec] * 4, out_shape=[jax.ShapeDtypeStruct((L, MIX_W), F32)] * 4,
        scratch_shapes=[pltpu.VMEM((HEADS, HEAD_W, HEAD_W), F32)],
        compiler_params=_params(("arbitrary",)),
    )(q, k, v, g, states, do)


def _make_gla(n_ctx, reverse):
    @jax.custom_vjp
    def gla(q, k, v, g):
        return _gla_fwd_call(q, k, v, g, n_ctx, reverse)[0]

    def gla_fwd(q, k, v, g):
        o, states = _gla_fwd_call(q, k, v, g, n_ctx, reverse)
        return o, (q, k, v, g, states)

    def gla_bwd(res, do):
        q, k, v, g, states = res
        return tuple(_gla_bwd_call(q, k, v, g, states, do, n_ctx, reverse))

    gla.defvjp(gla_fwd, gla_bwd)
    return gla


def _att_scaled(q_ref):
    return (q_ref[...] * ATT_SCALE).astype(BF16)


def _att_probs(qn, qr, kn, kr, i, nct, n_ctx):
    nt_dims = (((1,), (1,)), ((), ()))
    s = lax.dot_general(qn, kn, nt_dims, preferred_element_type=F32)
    s = s + lax.dot_general(qr, kr, nt_dims, preferred_element_type=F32)
    col = lax.broadcasted_iota(jnp.int32, (1, s.shape[1]), 1)
    s = s + jnp.where(col < jnp.where(i < nct, n_ctx, s.shape[1]), 0.0, -1e30)
    p = jnp.exp(s - jnp.max(s, axis=-1, keepdims=True))
    return p * (1.0 / jnp.sum(p, axis=-1, keepdims=True))


def _att_geometry(qn, n_ctx):
    L = qn.shape[0]
    tq = _pick(math.gcd(L, n_ctx), 256, 8)
    q_spec = pl.BlockSpec((tq, HEAD_W), lambda h, i: (i, h))
    k_spec = pl.BlockSpec((L, HEAD_W), lambda h, i: (0, h))
    kr_spec = pl.BlockSpec((L, HEAD_W), lambda h, i: (0, 0))
    return L, tq, n_ctx // tq, q_spec, k_spec, kr_spec


def _att_fwd_call(qn, qr, kn, kr, v, n_ctx):
    L, tq, nct, q_spec, k_spec, kr_spec = _att_geometry(qn, n_ctx)

    def body(qn_ref, qr_ref, kn_ref, kr_ref, v_ref, o_ref):
        p = _att_probs(_att_scaled(qn_ref), _att_scaled(qr_ref), kn_ref[...].astype(BF16),
                       kr_ref[...].astype(BF16), pl.program_id(1), nct, n_ctx)
        o_ref[...] = jnp.dot(p.astype(BF16), v_ref[...].astype(BF16), preferred_element_type=F32).astype(BF16)

    return _pcall(
        body, name="att_fwd", grid=(HEADS, L // tq),
        in_specs=[q_spec, q_spec, k_spec, kr_spec, k_spec], out_specs=q_spec,
        out_shape=jax.ShapeDtypeStruct((L, MIX_W), BF16),
        compiler_params=_params(("parallel", "parallel")),
    )(qn, qr, kn, kr, v)


def _att_bwd_call(qn, qr, kn, kr, v, do, n_ctx):
    L, tq, nct, q_spec, k_spec, kr_spec = _att_geometry(qn, n_ctx)
    tn_dims = (((0,), (0,)), ((), ()))

    def body(qn_ref, qr_ref, kn_ref, kr_ref, v_ref, do_ref, dqn_ref, dqr_ref, dkn_ref, dkr_ref, dv_ref):
        h, i = pl.program_id(0), pl.program_id(1)
        qn, qr = _att_scaled(qn_ref), _att_scaled(qr_ref)
        kn, kr, vv = kn_ref[...].astype(BF16), kr_ref[...].astype(BF16), v_ref[...].astype(BF16)
        do = do_ref[...].astype(BF16)
        p = _att_probs(qn, qr, kn, kr, i, nct, n_ctx)
        dp = lax.dot_general(do, vv, (((1,), (1,)), ((), ())), preferred_element_type=F32)
        ds = (p * (dp - jnp.sum(p * dp, axis=-1, keepdims=True))).astype(BF16)
        dqn_ref[...] = jnp.dot(ds, kn, preferred_element_type=F32) * ATT_SCALE
        dqr_ref[...] = jnp.dot(ds, kr, preferred_element_type=F32) * ATT_SCALE

        @pl.when(i == 0)
        def _():
            dkn_ref[...] = jnp.zeros_like(dkn_ref)
            dv_ref[...] = jnp.zeros_like(dv_ref)

        @pl.when(jnp.logical_and(i == 0, h == 0))
        def _():
            dkr_ref[...] = jnp.zeros_like(dkr_ref)

        dkn_ref[...] += lax.dot_general(ds, qn, tn_dims, preferred_element_type=F32)
        dkr_ref[...] += lax.dot_general(ds, qr, tn_dims, preferred_element_type=F32)
        dv_ref[...] += lax.dot_general(p.astype(BF16), do, tn_dims, preferred_element_type=F32)

    return _pcall(
        body, name="att_bwd", grid=(HEADS, L // tq),
        in_specs=[q_spec, q_spec, k_spec, kr_spec, k_spec, q_spec],
        out_specs=[q_spec, q_spec, k_spec, kr_spec, k_spec],
        out_shape=[jax.ShapeDtypeStruct((L, MIX_W), F32)] * 3 + [jax.ShapeDtypeStruct((L, HEAD_W), F32),
                                                                 jax.ShapeDtypeStruct((L, MIX_W), F32)],
        compiler_params=_params(("arbitrary", "arbitrary")),
    )(qn, qr, kn, kr, v, do)


def _make_attention(n_ctx):
    @jax.custom_vjp
    def att(qn, qr, kn, kr, v):
        return _att_fwd_call(qn, qr, kn, kr, v, n_ctx)

    def att_fwd(qn, qr, kn, kr, v):
        return _att_fwd_call(qn, qr, kn, kr, v, n_ctx), (qn, qr, kn, kr, v)

    def att_bwd(res, do):
        return tuple(_att_bwd_call(*res, do, n_ctx))

    att.defvjp(att_fwd, att_bwd)
    return att


def _loss_call(x, g, target):
    L, D = x.shape
    tl = _pick(L, 256, 8)

    def f(xv, gv, tv):
        err = _rms(xv, gv) - tv
        return 0.5 * jnp.sum(err * err, axis=0, keepdims=True) / D

    def body(x_ref, g_ref, t_ref, loss_ref, dx_ref, dg_ref):
        i = pl.program_id(0)
        loss, vjp = jax.vjp(lambda xv, gv: f(xv, gv, t_ref[...]), x_ref[...], g_ref[...])
        dx, dg = vjp(jnp.ones_like(loss))
        dx_ref[...] = dx

        @pl.when(i == 0)
        def _():
            loss_ref[...] = loss
            dg_ref[...] = dg

        @pl.when(i != 0)
        def _():
            loss_ref[...] += loss
            dg_ref[...] += dg

    row = pl.BlockSpec((tl, D), lambda i: (i, 0))
    one = pl.BlockSpec((1, D), lambda i: (0, 0))
    return _pcall(
        body, name="loss", grid=(L // tl,), in_specs=[row, one, row], out_specs=[one, row, one],
        out_shape=[jax.ShapeDtypeStruct((1, D), F32), jax.ShapeDtypeStruct((L, D), F32),
                   jax.ShapeDtypeStruct((1, D), F32)],
        compiler_params=_params(("arbitrary",)),
    )(x, g, target)


def _sum_leading(x, name):
    n, R, W = x.shape
    tr = _pick(R, 512, 8)

    def body(x_ref, o_ref):
        acc = x_ref[0]
        for d in range(1, n):
            acc = acc + x_ref[d]
        o_ref[...] = acc

    return _pcall(
        body, name=name, grid=(R // tr,), in_specs=[pl.BlockSpec((n, tr, W), lambda i: (0, i, 0))],
        out_specs=pl.BlockSpec((tr, W), lambda i: (i, 0)), out_shape=jax.ShapeDtypeStruct((R, W), F32),
        compiler_params=_params(("parallel",)),
    )(x)


def _adamw(w, g, m, v):
    shape = w.shape
    W = shape[-1]
    as2d = lambda t: t.reshape(-1, W)
    R = as2d(w).shape[0]
    tr = _pick(R, max(8, (2 ** 17 // W) // 8 * 8), 8)

    def body(w_ref, g_ref, m_ref, v_ref, d_ref, nm_ref, nv_ref):
        gv = g_ref[...]
        m_new = ADAM_B1 * m_ref[...] + (1.0 - ADAM_B1) * gv
        v_new = ADAM_B2 * v_ref[...] + (1.0 - ADAM_B2) * (gv * gv)
        m_hat = m_new / (1.0 - ADAM_B1 ** ADAM_STEP)
        v_hat = v_new / (1.0 - ADAM_B2 ** ADAM_STEP)
        d_ref[...] = -ADAM_LR * (m_hat / (jnp.sqrt(v_hat) + ADAM_EPS) + ADAM_WD * w_ref[...])
        nm_ref[...] = m_new
        nv_ref[...] = v_new

    spec = pl.BlockSpec((tr, W), lambda i: (i, 0))
    outs = _pcall(
        body, name="adamw", grid=(R // tr,), in_specs=[spec] * 4, out_specs=[spec] * 3,
        out_shape=[jax.ShapeDtypeStruct((R, W), F32)] * 3, compiler_params=_params(("parallel",)),
    )(as2d(w), as2d(g), as2d(m), as2d(v))
    return tuple(o.reshape(shape) for o in outs)


HBM = pl.BlockSpec(memory_space=pltpu.HBM)


def _place():
    x, y, c = lax.axis_index("x"), lax.axis_index("y"), lax.axis_index("c")
    return x, y, c, [(1 - x, y), (x, 1 - y), (1 - x, 1 - y)]


class _Gather:
    def __init__(self, x_refs, out_refs, send_sems, recv_sems, local_sems):
        self.x_refs, self.out_refs, self.n = x_refs, out_refs, len(x_refs)
        self.sems = send_sems, recv_sems, local_sems
        self.x, self.y, self.c, self.chips = _place()
        self.me, self.sibling = (self.x, self.y, self.c), (self.x, self.y, 1 - self.c)

    @staticmethod
    def scratch(n):
        return [pltpu.SemaphoreType.DMA((7 * n,)), pltpu.SemaphoreType.DMA((7 * n,)), pltpu.SemaphoreType.DMA((n,))]

    def slot(self, a, px, py, pc):
        return self.out_refs[a].at[4 * px + 2 * py + pc]

    def copy(self, a, k, blk, to, src=None):
        return pltpu.make_async_remote_copy(
            src_ref=self.slot(a, *blk) if src is None else src, dst_ref=self.slot(a, *blk),
            send_sem=self.sems[0].at[7 * a + k], recv_sem=self.sems[1].at[7 * a + k], device_id=to,
            device_id_type=MESH)

    def own(self):
        first = []
        for a in range(self.n):
            first.append(self.copy(a, 0, self.me, self.sibling, src=self.x_refs[a]))
            first += [self.copy(a, 1 + j, self.me, (*chip, self.c), src=self.x_refs[a])
                      for j, chip in enumerate(self.chips)]
        return [pltpu.make_async_copy(self.x_refs[a], self.slot(a, *self.me), self.sems[2].at[a])
                for a in range(self.n)], first

    def start(self):
        mine, first = self.own()
        for cp in mine + first:
            cp.start()

    def finish(self):
        c, chips = self.c, self.chips
        mine, first = self.own()
        passed = []
        for j, chip in enumerate(chips):
            for a in range(self.n):
                self.copy(a, 1 + j, (*chip, c), self.me).wait_recv()
                passed.append(self.copy(a, 4 + j, (*chip, c), self.sibling))
                passed[-1].start()
        for a in range(self.n):
            self.copy(a, 0, self.sibling, self.me).wait_recv()
        for j, chip in enumerate(chips):
            for a in range(self.n):
                self.copy(a, 4 + j, (*chip, 1 - c), self.me).wait_recv()
        for cp in first + passed:
            cp.wait_send()
        for cp in mine:
            cp.wait()


def _all_gather(blocks, name):
    n = len(blocks)

    def body(*refs):
        g = _Gather(refs[:n], refs[n:2 * n], *refs[2 * n:])
        g.start()
        g.finish()

    return _pcall(
        body, name=name, out_shape=[jax.ShapeDtypeStruct((N_DEV,) + b.shape, b.dtype) for b in blocks],
        in_specs=[HBM] * n, out_specs=[HBM] * n, scratch_shapes=_Gather.scratch(n),
    )(*blocks)


def _send_to_sibling(gs, name):
    n = len(gs)

    def body(*refs):
        g_refs, out_refs, (send_sems, recv_sems) = refs[:n], refs[n:2 * n], refs[2 * n:]
        x, y, c, _ = _place()
        copies = [pltpu.make_async_remote_copy(
            src_ref=g_refs[a].at[2 * q + 1 - c], dst_ref=out_refs[a].at[q], send_sem=send_sems.at[4 * a + q],
            recv_sem=recv_sems.at[4 * a + q], device_id=(x, y, 1 - c), device_id_type=MESH)
            for a in range(n) for q in range(4)]
        for cp in copies:
            cp.start()
        for cp in copies:
            cp.wait()

    return _pcall(
        body, name=name, out_shape=[jax.ShapeDtypeStruct((4,) + g.shape[1:], g.dtype) for g in gs],
        in_specs=[HBM] * n, out_specs=[HBM] * n,
        scratch_shapes=[pltpu.SemaphoreType.DMA((4 * n,)), pltpu.SemaphoreType.DMA((4 * n,))],
    )(*gs)


def _send_to_chips(ps, name):
    n = len(ps)

    def body(*refs):
        p_refs, out_refs, (send_sems, recv_sems) = refs[:n], refs[n:2 * n], refs[2 * n:]
        x, y, c, chips = _place()
        copies = [pltpu.make_async_remote_copy(
            src_ref=p_refs[a].at[2 * cx + cy], dst_ref=out_refs[a].at[j], send_sem=send_sems.at[3 * a + j],
            recv_sem=recv_sems.at[3 * a + j], device_id=(cx, cy, c), device_id_type=MESH)
            for a in range(n) for j, (cx, cy) in enumerate(chips)]
        for cp in copies:
            cp.start()
        for cp in copies:
            cp.wait()

    return _pcall(
        body, name=name, out_shape=[jax.ShapeDtypeStruct((3,) + p.shape[1:], p.dtype) for p in ps],
        in_specs=[HBM] * n, out_specs=[HBM] * n,
        scratch_shapes=[pltpu.SemaphoreType.DMA((3 * n,)), pltpu.SemaphoreType.DMA((3 * n,))],
    )(*ps)


def _add_rows(R, W):
    return _pick(R, max(16, 2 ** 19 // W // 16 * 16), 16)


def _add_sibling(g, recv, core):
    _, R, W = g.shape
    tr = _add_rows(R, W)

    def body(core_ref, g_ref, r_ref, o_ref):
        o_ref[...] = (g_ref[...].astype(F32) + r_ref[...].astype(F32)).astype(BF16)

    return _pcall(
        body, name="rs_add_sibling",
        grid_spec=pltpu.PrefetchScalarGridSpec(
            num_scalar_prefetch=1, grid=(4, R // tr),
            in_specs=[pl.BlockSpec((None, tr, W), lambda q, i, core_ref: (2 * q + core_ref[0], i, 0)),
                      pl.BlockSpec((None, tr, W), lambda q, i, core_ref: (q, i, 0))],
            out_specs=pl.BlockSpec((None, tr, W), lambda q, i, core_ref: (q, i, 0))),
        out_shape=jax.ShapeDtypeStruct((4, R, W), BF16), compiler_params=_params(("parallel", "parallel")),
    )(core, g, recv)


def _add_chips(p, recv, chip):
    _, R, W = p.shape
    tr = _add_rows(R, W)

    def body(chip_ref, p_ref, r_ref, o_ref):
        up = lambda t: t.astype(F32)
        o_ref[...] = ((up(p_ref[...]) + up(r_ref[0])) + up(r_ref[1])) + up(r_ref[2])

    return _pcall(
        body, name="rs_add_chips",
        grid_spec=pltpu.PrefetchScalarGridSpec(
            num_scalar_prefetch=1, grid=(R // tr,),
            in_specs=[pl.BlockSpec((None, tr, W), lambda i, chip_ref: (chip_ref[0], i, 0)),
                      pl.BlockSpec((3, tr, W), lambda i, chip_ref: (0, i, 0))],
            out_specs=pl.BlockSpec((tr, W), lambda i, chip_ref: (i, 0))),
        out_shape=jax.ShapeDtypeStruct((R, W), F32), compiler_params=_params(("parallel",)),
    )(chip, p, recv)


def _reduce_scatter(gs):
    x, y, c = lax.axis_index("x"), lax.axis_index("y"), lax.axis_index("c")
    core = jnp.reshape(c, (1,)).astype(jnp.int32)
    chip = jnp.reshape(2 * x + y, (1,)).astype(jnp.int32)
    from_sibling = _send_to_sibling(gs, "rs_sibling")
    chip_sums = [_add_sibling(g, r, core) for g, r in zip(gs, from_sibling)]
    from_chips = _send_to_chips(chip_sums, "rs_chips")
    return [_add_chips(p, r, chip) for p, r in zip(chip_sums, from_chips)]


def _rope_tables(seq, n_ctx):
    rows = seq // GRID_W
    row = jnp.repeat(jnp.arange(rows, dtype=F32), GRID_W)
    col = jnp.tile(jnp.arange(GRID_W, dtype=F32), rows)
    inv = ROPE_BASE ** (-jnp.arange(ROPE_FREQS, dtype=F32) * 2.0 / (ROPE // 2))
    ang_r, ang_c = row[:, None] * inv, col[:, None] * inv
    one, zero = jnp.ones((seq, ROPE), F32), jnp.zeros((seq, ROPE), F32)
    cos = jnp.concatenate([jnp.cos(ang_r), jnp.cos(ang_r), jnp.cos(ang_c), jnp.cos(ang_c), one], axis=1)
    sin = jnp.concatenate([-jnp.sin(ang_r), jnp.sin(ang_r), -jnp.sin(ang_c), jnp.sin(ang_c), zero], axis=1)
    cos = jnp.concatenate([jnp.ones((n_ctx, LANES), F32), cos], axis=0)
    sin = jnp.concatenate([jnp.zeros((n_ctx, LANES), F32), sin], axis=0)
    return cos, sin


def _shared(v):
    return v.reshape((1, 1, -1)) if v.ndim == 1 else v.reshape((1,) + v.shape)


Z_BOUNDS = (Z_Q, Z_K, Z_V, Z_G, Z_LR, Z_A, Z_GATE, Z_BG, Z_CG, Z_H, Z_CQ, Z_CKV, Z_KR, Z_END)
PLAIN = ("w_in", "w_qn", "w_qr", "w_kn", "w_v")
COLS = ("ffn_w1", "ffn_w3")
ROWS = ("w_out", "ffn_w2")


def _layer(t, p, car, G, nxt, mod_l, mod_c, n_ctx, tables):
    L, D = t.shape
    seg = lambda i: jnp.stack([mod_c[i * D:(i + 1) * D], mod_l[i * D:(i + 1) * D]]).reshape(2, 1, D)
    sh1, sc1, g1, sh2, sc2, g2 = (seg(i) for i in range(6))
    plain = _plain_weights(G, D)
    dense = lambda a, n: mm(a, plain[n], car[n], ())[0]
    take = lambda *names: tuple(nxt[n] for n in names) if nxt else ()

    (h,) = _rowop(_f_norm_mod, "norm_mod", n_ctx, out_dtypes=(BF16,))((t,), (), (_shared(p["norm1_g"]), sc1, sh1))
    z, got_in = mm(h, plain["w_in"], car["w_in"], take("w_in"))
    q, k, v, gate, lr, conf_a, conf_gate, sc_b, sc_c, sc_h, cq, ckv, kr = _split_cols(z, Z_BOUNDS)

    up = p["gla_fg_up"]
    up_f = jnp.pad(up[0], ((0, LANES - GATE_RANK), (0, 0)))
    up_b = jnp.pad(up[1], ((GATE_RANK, LANES - 2 * GATE_RANK), (0, 0)))
    logd_f, logd_b = _rowop(_f_gla_gate, "gla_gate")(
        (lr,), (), (_shared(up_f), _shared(up_b), _shared(p["gla_fg_b"][0]), _shared(p["gla_fg_b"][1])))
    o_f = _make_gla(n_ctx, False)(q, k, v, logd_f)
    o_b = _make_gla(n_ctx, True)(q, k, v, logd_b)
    (gla,) = _rowop(_f_gla_finish, "gla_finish", lane_block=HEAD_W, out_dtypes=(BF16,))(
        (o_f, o_b, gate), (), (_shared(p["gla_onorm_g"]),))

    conv = _make_conv(n_ctx)
    (u,) = _rowop(_f_glu, "glu")((conf_a, conf_gate), (), ())
    (conf,) = _rowop(_f_ln_silu, "ln_silu", out_dtypes=(BF16,))(
        (conv(u, p["conf_dw"]),), (), (_shared(p["conf_dw_b"]), _shared(p["conf_ln_g"]), _shared(p["conf_ln_b"])))

    (ch,) = _rowop(_f_mul, "mul")((sc_c, sc_h), (), ())
    (sconv,) = _rowop(_f_mul, "mul_out", out_dtypes=(BF16,))((sc_b, conv(ch, p["sc_dw"])), (), ())

    (cq,) = _rowop(_f_rms, "rms")((cq,), (), (_shared(p["mla_q_norm_g"]),))
    (ckv,) = _rowop(_f_rms, "rms")((ckv,), (), (_shared(p["mla_kv_norm_g"]),))
    rope = _rowop(_f_rope, "rope", lane_block=LANES)
    (qr,) = rope((dense(cq, "w_qr"),), tables, ())
    (kr,) = rope((kr,), tables, ())
    mla = _make_attention(n_ctx)(dense(cq, "w_qn"), qr, dense(ckv, "w_kn"), kr, dense(ckv, "w_v"))

    o, got_out = _make_mm_rows(0)(jnp.concatenate([gla, conf, sconv, mla], axis=1), G["w_out"], car["w_out"],
                                  take("w_out", "mla_w_uq", "mla_w_ukv"))
    t1, h2 = _rowop(_f_resid_norm_mod, "resid_norm_mod", n_ctx, out_dtypes=(F32, BF16))(
        (t, o), (), (g1, _shared(p["norm2_g"]), sc2, sh2))
    a1, got_w1 = _make_mm_cols(0)(h2, G["ffn_w1"], car["ffn_w1"], take("ffn_w1"))
    a3, got_w3 = _make_mm_cols(0)(h2, G["ffn_w3"], car["ffn_w3"], take("ffn_w3"))
    n_ff = a1.shape[2]
    (act,) = _rowop(_f_swiglu, "swiglu", tile=1024, out_dtypes=(BF16,))(
        (a1.reshape(N_DEV * L, n_ff), a3.reshape(N_DEV * L, n_ff)), (), ())
    f, got_w2 = _make_mm_rows(0)(act.reshape(N_DEV, L, n_ff), G["ffn_w2"], car["ffn_w2"], take("ffn_w2"))
    (t2,) = _rowop(_f_resid, "resid", n_ctx)((t1, f), (), (g2,))
    got = got_in + got_out + got_w1 + got_w3 + got_w2
    return t2, dict(zip(("w_in", "w_out", "mla_w_uq", "mla_w_ukv", "ffn_w1", "ffn_w3", "ffn_w2"), got)) if nxt else None


def _trunk(t, smalls, cars, G, shards, mods_l, mods_c, n_ctx, seq):
    tables = _rope_tables(seq, n_ctx)
    depth = len(smalls)
    for i in range(depth):
        t, G = _layer(t, smalls[i], cars[i], G, shards[i + 1] if i + 1 < depth else None, mods_l[i], mods_c[i],
                      n_ctx, tables)
    return t


def _plain_weights(G, D):
    full = lambda n: jnp.concatenate([G[n][d, 0] for d in range(N_DEV)], axis=1)
    w_in = full("w_in")
    w_in = jnp.concatenate([w_in[:, :Z_LR + 2 * GATE_RANK], jnp.zeros((D, Z_A - Z_LR - 2 * GATE_RANK), BF16),
                            w_in[:, Z_LR + 2 * GATE_RANK:], jnp.zeros((D, Z_END - Z_KR - ROPE), BF16)], axis=1)
    w_uq = full("mla_w_uq").reshape(Q_RANK, HEADS, HEAD_W + ROPE)
    w_ukv = full("mla_w_ukv").reshape(KV_RANK, HEADS, 2 * HEAD_W)
    return {"w_in": w_in,
            "w_qn": w_uq[:, :, :HEAD_W].reshape(Q_RANK, MIX_W),
            "w_qr": jnp.pad(w_uq[:, :, HEAD_W:], ((0, 0), (0, 0), (0, LANES - ROPE))).reshape(Q_RANK, HEADS * LANES),
            "w_kn": w_ukv[:, :, :HEAD_W].reshape(KV_RANK, MIX_W),
            "w_v": w_ukv[:, :, HEAD_W:].reshape(KV_RANK, MIX_W)}


def _col_slabs(full):
    n = full.shape[1] // N_DEV
    return jnp.stack([full[:, d * n:(d + 1) * n] for d in range(N_DEV)])


def _shard_grads(d_car):
    d_in = d_car["w_in"]
    d_in = jnp.concatenate([d_in[:, :Z_LR + 2 * GATE_RANK], d_in[:, Z_A:Z_KR + ROPE]], axis=1)
    by_head = lambda g: g.reshape(g.shape[0], HEADS, -1)
    d_uq = jnp.concatenate([by_head(d_car["w_qn"]), by_head(d_car["w_qr"])[:, :, :ROPE]], axis=2)
    d_ukv = jnp.concatenate([by_head(d_car["w_kn"]), by_head(d_car["w_v"])], axis=2)
    out = {"w_in": _col_slabs(d_in), "mla_w_uq": _col_slabs(d_uq.reshape(Q_RANK, -1)),
           "mla_w_ukv": _col_slabs(d_ukv.reshape(KV_RANK, -1))}
    out.update({n: d_car[n] for n in COLS + ROWS})
    return out


BIG = ("w_in", "w_out", "ffn_w1", "ffn_w3", "ffn_w2", "mla_w_uq", "mla_w_ukv")
SMALL_SHARED = ("norm1_g", "gla_onorm_g", "conf_dw_b", "conf_ln_g", "conf_ln_b", "mla_q_norm_g", "mla_kv_norm_g",
                "norm2_g")
SMALL_SHARDED = ("gla_fg_up", "gla_fg_b", "conf_dw", "sc_dw")
WEIGHTS = ("c_ctx", "norm1_g", "w_mod", "b_mod", "w_in", "gla_fg_up", "gla_fg_b", "gla_onorm_g", "conf_dw",
           "conf_dw_b", "conf_ln_g", "conf_ln_b", "sc_dw", "mla_q_norm_g", "mla_kv_norm_g", "mla_w_uq", "mla_w_ukv",
           "w_out", "norm2_g", "ffn_w1", "ffn_w3", "ffn_w2", "final_norm_g")


def _gather_last(pieces8):
    moved = jnp.moveaxis(pieces8, 0, -2)
    return moved.reshape(moved.shape[:-2] + (-1,))


def _sum_devices(x8, name):
    shape = x8.shape[1:]
    return _sum_leading(x8.reshape(N_DEV, -1, shape[-1]), name).reshape(shape)


def _step(w, m, v, x, c, ctx, loss_target):
    depth = w["norm1_g"].shape[0]
    seq, D = x.shape[1], x.shape[2]
    n_ctx = ctx.shape[1]
    me = 4 * lax.axis_index("x") + 2 * lax.axis_index("y") + lax.axis_index("c")

    shards = [{n: w[n][i:i + 1].astype(BF16) for n in BIG} for i in range(depth)]
    G = dict(zip(BIG, _all_gather([shards[0][n] for n in BIG], "gather_weights")))
    small8 = _all_gather([c] + [w[n] for n in SMALL_SHARDED], "gather_small")
    c_all = small8[0].reshape(N_DEV, D)
    small_full = {n: _gather_last(g) for n, g in zip(SMALL_SHARDED, small8[1:])}

    rows = jnp.concatenate([c_all, w["c_ctx"][None], jnp.zeros((16 - N_DEV - 1, D), F32)])
    (act,) = _rowop(_f_silu, "silu")((rows,), (), ())
    n_mod = w["w_mod"].shape[2]
    b_mine = lax.dynamic_slice_in_dim(w["b_mod"], me * n_mod, n_mod, axis=1)
    mod_cols = [_rowop(_f_add_bias, "add_bias")((_matmul(act, w["w_mod"][i], "nn", "mod_fwd"),), (),
                                                (_shared(b_mine[i]),))[0] for i in range(depth)]
    (mods8,) = _all_gather([jnp.stack(mod_cols)], "gather_mod")
    mods = jnp.moveaxis(mods8, 0, 2).reshape(depth, 16, N_DEV * n_mod)
    mods_l = [lax.dynamic_index_in_dim(mods[i], me, 0, keepdims=False) for i in range(depth)]
    mods_c = [mods[i, N_DEV] for i in range(depth)]

    smalls, cars = [], []
    plain_shapes = {"w_in": (D, Z_END), "w_qn": (Q_RANK, MIX_W), "w_qr": (Q_RANK, HEADS * LANES),
                    "w_kn": (KV_RANK, MIX_W), "w_v": (KV_RANK, MIX_W)}
    for i in range(depth):
        p = {n: w[n][i] for n in SMALL_SHARED}
        p.update({n: small_full[n][i] for n in SMALL_SHARDED})
        smalls.append(p)
        car = {n: lax.empty(s, BF16) for n, s in plain_shapes.items()}
        car.update({n: lax.empty((N_DEV,) + w[n].shape[1:], BF16) for n in COLS + ROWS})
        cars.append(car)
    t0 = jnp.concatenate([ctx[0], x[0]], axis=0)
    t_out, vjp = jax.vjp(lambda t, ps, cs, ml, mc: _trunk(t, ps, cs, G, shards, ml, mc, n_ctx, seq),
                         t0, smalls, cars, mods_l, mods_c)
    loss_lanes, d_out, d_final_g = _loss_call(t_out[n_ctx:], w["final_norm_g"][None], loss_target[0])
    d_t0, d_smalls, d_cars, d_mods_l, d_mods_c = vjp(jnp.concatenate([jnp.zeros((n_ctx, D), F32), d_out], axis=0))
    grad_x = d_t0[n_ctx:][None]

    grads = {n: [] for n in BIG}
    for i in range(depth):
        sharded = _shard_grads(d_cars[i])
        for n, g in zip(BIG, _reduce_scatter([sharded[n] for n in BIG])):
            grads[n].append(g)
    grads = {n: jnp.stack(g) for n, g in grads.items()}

    names = SMALL_SHARED + SMALL_SHARDED
    d_mod = jnp.stack([jnp.stack([d_mods_l[i], d_mods_c[i]]) for i in range(depth)])
    parts = [loss_lanes, d_final_g, d_mod] + [jnp.stack([d_smalls[i][n] for i in range(depth)]) for n in names]
    parts8 = _all_gather(parts, "gather_partials")
    summed = [_sum_devices(p8, "sum_partials") for p8 in parts8]
    (loss_row,) = _rowop(lambda a: (jnp.sum(a, axis=-1, keepdims=True) + jnp.zeros_like(a),), "loss_sum")(
        (summed[0],), (), ())
    loss = loss_row[0, 0]
    grads["final_norm_g"] = summed[1].reshape(D)
    for n, g in zip(names, summed[3:]):
        if n in SMALL_SHARDED:
            g = lax.dynamic_slice_in_dim(g, me * w[n].shape[-1], w[n].shape[-1], axis=g.ndim - 1)
        grads[n] = g

    d_mod8, d_mod_c = parts8[2], summed[2][:, 1]
    grads["b_mod"] = _rowop(lambda a, b: (a + b,), "add")((summed[2][:, 0], d_mod_c), (), ())[0]
    d_rows = jnp.concatenate([jnp.moveaxis(d_mod8[:, :, 0], 0, 1), d_mod_c[:, None],
                              jnp.zeros((depth, 16 - N_DEV - 1, 6 * D), F32)], axis=1)
    d_rows = lax.dynamic_slice_in_dim(d_rows, me * n_mod, n_mod, axis=2)
    grads["w_mod"] = jnp.stack([_matmul(act, d_rows[i], "tn", "mod_dw") for i in range(depth)])
    d_act = _matmul(d_rows.transpose(1, 0, 2).reshape(16, depth * n_mod),
                    w["w_mod"].transpose(1, 0, 2).reshape(D, depth * n_mod), "nt", "mod_dact")
    (d_act8,) = _all_gather([d_act], "gather_dact")
    (d_rows_in,) = _rowop(_f_mul_silu_grad, "silu_grad")((_sum_devices(d_act8, "sum_dact"), rows), (), ())
    grads["c_ctx"] = d_rows_in[N_DEV]

    outs = {n: _adamw(w[n], grads[n], m[n], v[n]) for n in WEIGHTS}
    return (loss, grad_x, *[grads[n] for n in WEIGHTS], *[outs[n][0] for n in WEIGHTS],
            *[outs[n][1] for n in WEIGHTS], *[outs[n][2] for n in WEIGHTS])


def kernel(x, c, ctx, c_ctx, norm1_g, w_mod, b_mod, w_in, gla_fg_up, gla_fg_b, gla_onorm_g, conf_dw, conf_dw_b, conf_ln_g, conf_ln_b, sc_dw, mla_q_norm_g, mla_kv_norm_g, mla_w_uq, mla_w_ukv, w_out, norm2_g, ffn_w1, ffn_w3, ffn_w2, final_norm_g, loss_target, m_c_ctx, m_norm1_g, m_w_mod, m_b_mod, m_w_in, m_gla_fg_up, m_gla_fg_b, m_gla_onorm_g, m_conf_dw, m_conf_dw_b, m_conf_ln_g, m_conf_ln_b, m_sc_dw, m_mla_q_norm_g, m_mla_kv_norm_g, m_mla_w_uq, m_mla_w_ukv, m_w_out, m_norm2_g, m_ffn_w1, m_ffn_w3, m_ffn_w2, m_final_norm_g, v_c_ctx, v_norm1_g, v_w_mod, v_b_mod, v_w_in, v_gla_fg_up, v_gla_fg_b, v_gla_onorm_g, v_conf_dw, v_conf_dw_b, v_conf_ln_g, v_conf_ln_b, v_sc_dw, v_mla_q_norm_g, v_mla_kv_norm_g, v_mla_w_uq, v_mla_w_ukv, v_w_out, v_norm2_g, v_ffn_w1, v_ffn_w3, v_ffn_w2, v_final_norm_g):
    w = dict(c_ctx=c_ctx, norm1_g=norm1_g, w_mod=w_mod, b_mod=b_mod, w_in=w_in, gla_fg_up=gla_fg_up, gla_fg_b=gla_fg_b, gla_onorm_g=gla_onorm_g, conf_dw=conf_dw, conf_dw_b=conf_dw_b, conf_ln_g=conf_ln_g, conf_ln_b=conf_ln_b, sc_dw=sc_dw, mla_q_norm_g=mla_q_norm_g, mla_kv_norm_g=mla_kv_norm_g, mla_w_uq=mla_w_uq, mla_w_ukv=mla_w_ukv, w_out=w_out, norm2_g=norm2_g, ffn_w1=ffn_w1, ffn_w3=ffn_w3, ffn_w2=ffn_w2, final_norm_g=final_norm_g)
    m = dict(c_ctx=m_c_ctx, norm1_g=m_norm1_g, w_mod=m_w_mod, b_mod=m_b_mod, w_in=m_w_in, gla_fg_up=m_gla_fg_up, gla_fg_b=m_gla_fg_b, gla_onorm_g=m_gla_onorm_g, conf_dw=m_conf_dw, conf_dw_b=m_conf_dw_b, conf_ln_g=m_conf_ln_g, conf_ln_b=m_conf_ln_b, sc_dw=m_sc_dw, mla_q_norm_g=m_mla_q_norm_g, mla_kv_norm_g=m_mla_kv_norm_g, mla_w_uq=m_mla_w_uq, mla_w_ukv=m_mla_w_ukv, w_out=m_w_out, norm2_g=m_norm2_g, ffn_w1=m_ffn_w1, ffn_w3=m_ffn_w3, ffn_w2=m_ffn_w2, final_norm_g=m_final_norm_g)
    v = dict(c_ctx=v_c_ctx, norm1_g=v_norm1_g, w_mod=v_w_mod, b_mod=v_b_mod, w_in=v_w_in, gla_fg_up=v_gla_fg_up, gla_fg_b=v_gla_fg_b, gla_onorm_g=v_gla_onorm_g, conf_dw=v_conf_dw, conf_dw_b=v_conf_dw_b, conf_ln_g=v_conf_ln_g, conf_ln_b=v_conf_ln_b, sc_dw=v_sc_dw, mla_q_norm_g=v_mla_q_norm_g, mla_kv_norm_g=v_mla_kv_norm_g, mla_w_uq=v_mla_w_uq, mla_w_ukv=v_mla_w_ukv, w_out=v_w_out, norm2_g=v_norm2_g, ffn_w1=v_ffn_w1, ffn_w3=v_ffn_w3, ffn_w2=v_ffn_w2, final_norm_g=v_final_norm_g)
    return _step(w, m, v, x, c, ctx, loss_target)
```

```python
import functools
import math

import jax
import jax.numpy as jnp
from jax import lax
from jax.experimental import pallas as pl
from jax.experimental.pallas import tpu as pltpu

F32 = jnp.float32
BF16 = jnp.bfloat16
MESH = pl.DeviceIdType.MESH
N_DEV = 8

EPS = 1e-6
GRID_W = 64
HEADS = 4
HEAD_W = 128
MIX_W = HEADS * HEAD_W
GATE_RANK = 16
GATE_NORM = 16.0
GLA_CHUNK = 128
CONF_K = 31
SC_K = 3
Q_RANK = 384
KV_RANK = 128
ROPE = 64
ROPE_FREQS = 16
ROPE_BASE = 10000.0
ATT_SCALE = (HEAD_W + ROPE) ** -0.5
CONV_HALO = 16

ADAM_LR = 0.001
ADAM_B1 = 0.9
ADAM_B2 = 0.999
ADAM_EPS = 1e-08
ADAM_WD = 0.01
ADAM_STEP = 10

LANES = 128
VMEM_LIMIT = 56 * 2 ** 20
ROW_BLOCK_BYTES = 10 * 2 ** 20

Z_Q, Z_K, Z_V, Z_G, Z_LR, Z_A, Z_GATE, Z_BG, Z_CG, Z_H, Z_CQ, Z_CKV, Z_KR, Z_END = (
    0, 512, 1024, 1536, 2048, 2176, 2688, 3200, 3712, 4224, 4736, 5120, 5248, 5376)
IN_W = 5216


def _pcall(body, **kw):
    return pl.pallas_call(body, **kw)


def _params(sem=None):
    return pltpu.CompilerParams(dimension_semantics=sem, vmem_limit_bytes=VMEM_LIMIT)


def _pick(dim, cap, mult):
    d = (min(cap, dim) // mult) * mult
    while d >= mult:
        if dim % d == 0:
            return d
        d -= mult
    return dim


def _mm_call(name, a, b, out_shape, grid, a_spec, b_spec, o_spec, dims, k_axis=None, once_axis=None, out_dtype=F32,
             gather=()):
    a_blk = tuple(d for d in a_spec.block_shape if d is not None)
    o_blk = tuple(d for d in o_spec.block_shape if d is not None)
    if a.dtype == BF16:
        once_axis = None
    scratch = ([pltpu.VMEM((math.prod(o_blk[:-1]), o_blk[-1]), F32)] if k_axis is not None else []) + (
        [pltpu.VMEM(a_blk, BF16)] if once_axis is not None else [])
    nk = grid[k_axis] if k_axis is not None else 1
    ng = len(gather)

    def body(*refs):
        a_ref, b_ref, o_ref, scr = refs[0], refs[1], refs[2 + ng], refs[3 + 2 * ng:]
        if ng:
            exchange = _Gather(refs[2:2 + ng], refs[3 + ng:3 + 2 * ng], *scr[-3:])
            scr = scr[:-3]
            steps = [pl.program_id(ax) for ax in range(len(grid))]
            at_first = functools.reduce(jnp.logical_and, [s == 0 for s in steps])
            at_last = functools.reduce(jnp.logical_and, [s == g - 1 for s, g in zip(steps, grid)])
            pl.when(at_first)(exchange.start)
        product(a_ref, b_ref, o_ref, scr)
        if ng:
            pl.when(at_last)(exchange.finish)

    def product(a_ref, b_ref, o_ref, scr):
        if once_axis is not None:
            a_bf = scr[-1]

            @pl.when(pl.program_id(once_axis) == 0)
            def _():
                a_bf[...] = a_ref[...].astype(BF16)

            av = a_bf[...]
        else:
            av = a_ref[...].astype(BF16)
        bv = b_ref[...].astype(BF16)
        if bv.ndim == 3:
            bv = bv.reshape(-1, bv.shape[-1])
        prod = lax.dot_general(av, bv, dims, preferred_element_type=F32)
        if k_axis is None:
            o_ref[...] = prod.astype(o_ref.dtype).reshape(o_ref.shape)
        else:
            acc, k = scr[0], pl.program_id(k_axis)

            @pl.when(k == 0)
            def _():
                acc[...] = prod

            @pl.when(k != 0)
            def _():
                acc[...] += prod

            @pl.when(k == nk - 1)
            def _():
                o_ref[...] = acc[...].astype(o_ref.dtype).reshape(o_ref.shape)

    outs = _pcall(
        body, name=name, grid=grid, in_specs=[a_spec, b_spec] + [HBM] * ng, out_specs=[o_spec] + [HBM] * ng,
        out_shape=[jax.ShapeDtypeStruct(out_shape, out_dtype)]
        + [jax.ShapeDtypeStruct((N_DEV,) + g.shape, g.dtype) for g in gather],
        scratch_shapes=scratch + (_Gather.scratch(ng) if ng else []),
        compiler_params=_params(("arbitrary",) * len(grid)),
    )(a, b, *gather)
    return (outs[0], tuple(outs[1:])) if ng else outs[0]


NN = (((1,), (0,)), ((), ()))
NT = (((1,), (1,)), ((), ()))
TN = (((0,), (0,)), ((), ()))


def _matmul(a, b, mode, name, out_dtype=F32, gather=()):
    if mode == "nn":
        (M, K), (_, N) = a.shape, b.shape
    elif mode == "nt":
        (M, K), (N, _) = a.shape, b.shape
    else:
        (K, M), (_, N) = a.shape, b.shape
    if mode == "nn":
        tm, tn, tk = _pick(M, 1088, 16), _pick(N, 768, LANES), _pick(K, 2048, LANES)
    elif mode == "nt":
        tm, tn, tk = _pick(M, 1088, 16), _pick(N, 2048, LANES), _pick(K, 1024, LANES)
    else:
        tm, tn, tk = _pick(M, 2048, LANES), _pick(N, 768, LANES), _pick(K, 1088, 16)
    if mode == "nn":
        a_spec = pl.BlockSpec((tm, tk), lambda i, j, k: (i, k))
        b_spec = pl.BlockSpec((tk, tn), lambda i, j, k: (k, j))
    elif mode == "nt":
        a_spec = pl.BlockSpec((tm, tk), lambda i, j, k: (i, k))
        b_spec = pl.BlockSpec((tn, tk), lambda i, j, k: (j, k))
    else:
        a_spec = pl.BlockSpec((tk, tm), lambda i, j, k: (k, i))
        b_spec = pl.BlockSpec((tk, tn), lambda i, j, k: (k, j))
    return _mm_call(name, a, b, (M, N), (M // tm, N // tn, K // tk), a_spec, b_spec,
                    pl.BlockSpec((tm, tn), lambda i, j, k: (i, j)), {"nn": NN, "nt": NT, "tn": TN}[mode], k_axis=2,
                    out_dtype=out_dtype, gather=gather)


def _with_gathered(result, nxt):
    return result if nxt else (result, ())


def _no_grads(res, nxt):
    return jnp.zeros_like(res), tuple(jnp.zeros_like(t) for t in nxt)


@jax.custom_vjp
def mm(a, w, carrier, nxt):
    return _with_gathered(_matmul(a, w, "nn", "mm_fwd", gather=nxt), nxt)


def _mm_fwd(a, w, carrier, nxt):
    return mm(a, w, carrier, nxt), (a, w, nxt)


def _mm_bwd(res, cts):
    a, w, nxt = res
    dc = cts[0]
    no_w, no_nxt = _no_grads(w, nxt)
    return _matmul(dc, w, "nt", "mm_da", a.dtype), no_w, _matmul(a, dc, "tn", "mm_dw", BF16), no_nxt


mm.defvjp(_mm_fwd, _mm_bwd)


def _make_mm_cols(layer):
    def forward(a, G, nxt):
        (M, K), n = a.shape, G.shape[3]
        tm = _pick(M, 1088, 16)
        return _with_gathered(_mm_call(
            "mmc_fwd", a, G, (N_DEV, M, n), (M // tm, N_DEV),
            pl.BlockSpec((tm, K), lambda i, d: (i, 0)),
            pl.BlockSpec((None, None, K, n), lambda i, d: (d, layer, 0, 0)),
            pl.BlockSpec((None, tm, n), lambda i, d: (d, i, 0)), NN, once_axis=1, out_dtype=BF16, gather=nxt), nxt)

    def grad_a(do, G, dtype):
        (_, M, n), K = do.shape, G.shape[2]
        tm, tn = _pick(M, 1088, 16), _pick(K, 1024, LANES)
        return _mm_call("mmc_da", do, G, (M, K), (M // tm, K // tn, N_DEV),
                        pl.BlockSpec((None, tm, n), lambda i, j, d: (d, i, 0)),
                        pl.BlockSpec((None, None, tn, n), lambda i, j, d: (d, layer, j, 0)),
                        pl.BlockSpec((tm, tn), lambda i, j, d: (i, j)), NT, k_axis=2, out_dtype=dtype)

    def grad_w(a, do):
        (M, K), n = a.shape, do.shape[2]
        tm, tk = _pick(K, 2048, LANES), _pick(M, 1088, 16)
        return _mm_call("mmc_dw", a, do, (N_DEV, K, n), (N_DEV, K // tm, M // tk),
                        pl.BlockSpec((tk, tm), lambda d, i, k: (k, i)),
                        pl.BlockSpec((None, tk, n), lambda d, i, k: (d, k, 0)),
                        pl.BlockSpec((None, tm, n), lambda d, i, k: (d, i, 0)), TN, k_axis=2, out_dtype=BF16)

    @jax.custom_vjp
    def f(a, G, carrier, nxt):
        return forward(a, G, nxt)

    def f_fwd(a, G, carrier, nxt):
        return forward(a, G, nxt), (a, G, nxt)

    def f_bwd(res, cts):
        a, G, nxt = res
        no_g, no_nxt = _no_grads(G, nxt)
        return grad_a(cts[0], G, a.dtype), no_g, grad_w(a, cts[0]), no_nxt

    f.defvjp(f_fwd, f_bwd)
    return f


def _make_mm_rows(layer):
    def forward(a, G, nxt):
        M, (r, N) = a.shape[-2], G.shape[2:]
        tm, tn = _pick(M, 1088, 16), _pick(N, 1024, LANES)
        if a.ndim == 2:
            return _with_gathered(_mm_call(
                "mmr2_fwd", a, G, (M, N), (M // tm, N // tn),
                pl.BlockSpec((tm, N_DEV * r), lambda i, j: (i, 0)),
                pl.BlockSpec((N_DEV, None, r, tn), lambda i, j: (0, layer, 0, j)),
                pl.BlockSpec((tm, tn), lambda i, j: (i, j)), NN, gather=nxt), nxt)
        return _with_gathered(_mm_call(
            "mmr_fwd", a, G, (M, N), (M // tm, N // tn, N_DEV),
            pl.BlockSpec((None, tm, r), lambda i, j, d: (d, i, 0)),
            pl.BlockSpec((None, None, r, tn), lambda i, j, d: (d, layer, 0, j)),
            pl.BlockSpec((tm, tn), lambda i, j, d: (i, j)), NN, k_axis=2, gather=nxt), nxt)

    def grad_a(dc, G, like):
        (M, N), r = dc.shape, G.shape[2]
        tm = _pick(M, 1088, 16)
        if like.ndim == 2:
            return _mm_call("mmr2_da", dc, G, like.shape, (M // tm,),
                            pl.BlockSpec((tm, N), lambda i: (i, 0)),
                            pl.BlockSpec((N_DEV, None, r, N), lambda i: (0, layer, 0, 0)),
                            pl.BlockSpec((tm, N_DEV * r), lambda i: (i, 0)), NT, out_dtype=like.dtype)
        return _mm_call("mmr_da", dc, G, like.shape, (M // tm, N_DEV),
                        pl.BlockSpec((tm, N), lambda i, d: (i, 0)),
                        pl.BlockSpec((None, None, r, N), lambda i, d: (d, layer, 0, 0)),
                        pl.BlockSpec((None, tm, r), lambda i, d: (d, i, 0)), NT, once_axis=1, out_dtype=like.dtype)

    def grad_w(a, dc):
        (M, N), tk = dc.shape, _pick(dc.shape[0], 1088, 16)
        if a.ndim == 2:
            r, tn = a.shape[1] // N_DEV, _pick(N, 1024, LANES)
            return _mm_call("mmr2_dw", a, dc, (N_DEV, r, N), (N // tn, M // tk),
                            pl.BlockSpec((tk, N_DEV * r), lambda j, k: (k, 0)),
                            pl.BlockSpec((tk, tn), lambda j, k: (k, j)),
                            pl.BlockSpec((N_DEV, r, tn), lambda j, k: (0, 0, j)), TN, k_axis=1, out_dtype=BF16)
        r, tn = a.shape[2], _pick(N, 2048, LANES)
        return _mm_call("mmr_dw", a, dc, (N_DEV, r, N), (N_DEV, N // tn, M // tk),
                        pl.BlockSpec((None, tk, r), lambda d, j, k: (d, k, 0)),
                        pl.BlockSpec((tk, tn), lambda d, j, k: (k, j)),
                        pl.BlockSpec((None, r, tn), lambda d, j, k: (d, 0, j)), TN, k_axis=2, out_dtype=BF16)

    @jax.custom_vjp
    def f(a, G, carrier, nxt):
        return forward(a, G, nxt)

    def f_fwd(a, G, carrier, nxt):
        return forward(a, G, nxt), (a, G, nxt)

    def f_bwd(res, cts):
        a, G, nxt = res
        dc = cts[0].astype(BF16)
        no_g, no_nxt = _no_grads(G, nxt)
        return grad_a(dc, G, a), no_g, grad_w(a, dc), no_nxt

    f.defvjp(f_fwd, f_bwd)
    return f


@functools.partial(jax.custom_vjp, nondiff_argnums=(1,))
def _split_cols(z, bounds):
    return tuple(z[:, a:b] for a, b in zip(bounds[:-1], bounds[1:]))


def _split_cols_fwd(z, bounds):
    return _split_cols(z, bounds), None


def _split_cols_bwd(bounds, _, cts):
    return (jnp.concatenate(cts, axis=1),)


_split_cols.defvjp(_split_cols_fwd, _split_cols_bwd)


def _rowop(fn, name, n_ctx=0, tile=256, lane_block=None, out_dtypes=None):
    def geometry(rows):
        L, w0 = rows[0].shape
        nj = w0 // lane_block if lane_block else 1
        width = 3 * sum(lane_block or r.shape[1] for r in rows)
        cap = max(16, ROW_BLOCK_BYTES // (4 * width) // 16 * 16)
        tl = _pick(math.gcd(L, n_ctx) if n_ctx else L, min(tile, cap), 16)
        return L, tl, n_ctx // tl, nj

    def block_w(x):
        return lane_block or x.shape[1]

    def row_spec(tl, x):
        if lane_block and x.shape[1] != lane_block:
            return pl.BlockSpec((tl, lane_block), lambda i, j: (i, j))
        return pl.BlockSpec((tl, block_w(x)), lambda i, j: (i, 0))

    def param_spec(p, nct):
        s, r, w = p.shape
        if s == 1:
            return pl.BlockSpec((1, r, w), lambda i, j: (0, 0, 0))
        return pl.BlockSpec((1, r, w), lambda i, j: (jnp.where(i < nct, 0, 1), 0, 0))

    def forward(rows, consts, params):
        L, tl, nct, nj = geometry(rows)
        nr, nc, npar = len(rows), len(consts), len(params)
        outs = jax.eval_shape(
            lambda: fn(*[jnp.zeros((tl, block_w(r)), F32) for r in rows + consts],
                       *[jnp.zeros(p.shape[1:], F32) for p in params]))

        def body(*refs):
            ins = [r[...].astype(F32) for r in refs[:nr + nc]] + [r[0] for r in refs[nr + nc:nr + nc + npar]]
            for o_ref, o in zip(refs[nr + nc + npar:], fn(*ins)):
                o_ref[...] = o.astype(o_ref.dtype)

        return _pcall(
            body, name=name + "_fwd", grid=(L // tl, nj),
            in_specs=[row_spec(tl, r) for r in rows + consts] + [param_spec(p, nct) for p in params],
            out_specs=[pl.BlockSpec((tl, o.shape[1]), lambda i, j: (i, j)) for o in outs],
            out_shape=[jax.ShapeDtypeStruct((L, o.shape[1] * nj), dt)
                       for o, dt in zip(outs, out_dtypes or (F32,) * len(outs))],
            compiler_params=_params(("parallel", "parallel")),
        )(*rows, *consts, *params)

    def backward(rows, consts, params, cts):
        L, tl, nct, nj = geometry(rows)
        nr, nc, npar, no = len(rows), len(consts), len(params), len(cts)

        def body(*refs):
            i, j = pl.program_id(0), pl.program_id(1)
            rv = [r[...].astype(F32) for r in refs[:nr]]
            cv = [r[...] for r in refs[nr:nr + nc]]
            pv = [r[0] for r in refs[nr + nc:nr + nc + npar]]
            ct = tuple(r[...].astype(F32) for r in refs[nr + nc + npar:nr + nc + npar + no])
            out_refs = refs[nr + nc + npar + no:]
            _, vjp = jax.vjp(lambda *d: tuple(fn(*d[:nr], *cv, *d[nr:])), *rv, *pv)
            grads = vjp(ct)
            for ref, g in zip(out_refs[:nr], grads[:nr]):
                ref[...] = g.astype(ref.dtype)
            for ref, g, p in zip(out_refs[nr:], grads[nr:], params):
                first_row = (i == 0) if (p.shape[0] == 1 or nct == 0) else ((i == 0) | (i == nct))
                first = jnp.logical_and(first_row, j == 0)

                @pl.when(first)
                def _():
                    ref[0] = g

                @pl.when(jnp.logical_not(first))
                def _():
                    ref[0] += g

        outs = _pcall(
            body, name=name + "_bwd", grid=(L // tl, nj),
            in_specs=[row_spec(tl, r) for r in rows + consts] + [param_spec(p, nct) for p in params]
            + [pl.BlockSpec((tl, c.shape[1] // nj), lambda i, j: (i, j)) for c in cts],
            out_specs=[row_spec(tl, r) for r in rows] + [param_spec(p, nct) for p in params],
            out_shape=[jax.ShapeDtypeStruct(r.shape, r.dtype) for r in rows]
            + [jax.ShapeDtypeStruct(p.shape, F32) for p in params],
            compiler_params=_params(("arbitrary", "arbitrary")),
        )(*rows, *consts, *params, *cts)
        return tuple(outs[:nr]), tuple(outs[nr:])

    @jax.custom_vjp
    def op(rows, consts, params):
        return tuple(forward(rows, consts, params))

    def op_fwd(rows, consts, params):
        return tuple(forward(rows, consts, params)), (rows, consts, params)

    def op_bwd(res, cts):
        rows, consts, params = res
        d_rows, d_params = backward(rows, consts, params, tuple(cts))
        return d_rows, tuple(jnp.zeros_like(c) for c in consts), d_params

    op.defvjp(op_fwd, op_bwd)
    return op


def _sigmoid(x):
    return 1.0 / (1.0 + jnp.exp(-x))


def _silu(x):
    return x * _sigmoid(x)


def _log_sigmoid(x):
    return jnp.minimum(x, 0.0) - jnp.log(1.0 + jnp.exp(-jnp.abs(x)))


def _rms(x, g):
    return x * lax.rsqrt(jnp.mean(x * x, axis=-1, keepdims=True) + EPS) * g


def _f_norm_mod(x, g, sc, sh):
    return (_rms(x, g) * (1.0 + sc) + sh,)


def _f_resid_norm_mod(x, o, gate, g, sc, sh):
    x1 = x + gate * o
    return x1, _rms(x1, g) * (1.0 + sc) + sh


def _f_resid(x, o, gate):
    return (x + gate * o,)


def _f_swiglu(a1, a3):
    return (_silu(a1) * a3,)


def _f_gla_gate(lr, up_f, up_b, b_f, b_b):
    dot = functools.partial(jnp.dot, preferred_element_type=F32)
    return (_log_sigmoid(dot(lr, up_f) + b_f) / GATE_NORM, _log_sigmoid(dot(lr, up_b) + b_b) / GATE_NORM)


def _f_gla_finish(o_f, o_b, gate, g):
    return (_rms(o_f + o_b, g) * _silu(gate),)


def _f_glu(a, gate):
    return (a * _sigmoid(gate),)


def _f_ln_silu(u, dw_b, g, b):
    u = u + dw_b
    xc = u - jnp.mean(u, axis=-1, keepdims=True)
    y = xc * lax.rsqrt(jnp.mean(xc * xc, axis=-1, keepdims=True) + EPS)
    return (_silu(y * g + b),)


def _f_mul(a, b):
    return (a * b,)


def _f_rms(x, g):
    return (_rms(x, g),)


def _f_rope(t, cos, sin):
    w = t.shape[1]
    r = lax.broadcasted_iota(jnp.int32, (w, w), 0)
    c = lax.broadcasted_iota(jnp.int32, (w, w), 1)
    perm = (jnp.bitwise_xor(r, ROPE_FREQS) == c).astype(F32)
    partner = jnp.dot(t, perm, precision=lax.Precision.HIGHEST, preferred_element_type=F32)
    return (t * cos + partner * sin,)


def _f_silu(x):
    return (_silu(x),)


def _f_add_bias(x, b):
    return (x + b,)


def _f_mul_silu_grad(d, x):
    _, vjp = jax.vjp(_silu, x)
    return (vjp(d)[0],)


def _conv_geometry(u, n_ctx):
    L, C = u.shape
    tl = _pick(math.gcd(L, n_ctx), 256, 8)
    return L, C, tl, n_ctx // tl, L // tl


def _conv_specs(tl, nt):
    prev = pl.BlockSpec((tl, LANES), lambda c, i: (jnp.maximum(i - 1, 0), c))
    cur = pl.BlockSpec((tl, LANES), lambda c, i: (i, c))
    nxt = pl.BlockSpec((tl, LANES), lambda c, i: (jnp.minimum(i + 1, nt - 1), c))
    return prev, cur, nxt


def _conv_window(prev_ref, cur_ref, next_ref, tl, nct, nt):
    i = pl.program_id(1)
    has_prev = jnp.logical_and(i != 0, i != nct)
    has_next = jnp.logical_and(i != nct - 1, i != nt - 1)
    prev = jnp.where(has_prev, prev_ref[tl - CONV_HALO:tl, :], 0.0)
    nxt = jnp.where(has_next, next_ref[0:CONV_HALO, :], 0.0)
    return jnp.concatenate([prev, cur_ref[...], nxt], axis=0)


def _shifted(window, off, tl):
    n = window.shape[0]
    if off == 0:
        return window[0:tl]
    return pltpu.roll(window, n - off, 0)[0:tl]


def _conv_apply(u, w, n_ctx, flip, name):
    L, C, tl, nct, nt = _conv_geometry(u, n_ctx)
    K = w.shape[0]
    pad = (K - 1) // 2
    prev, cur, nxt = _conv_specs(tl, nt)

    def body(p_ref, c_ref, n_ref, w_ref, o_ref):
        win = _conv_window(p_ref, c_ref, n_ref, tl, nct, nt)
        acc = jnp.zeros((tl, LANES), F32)
        for k in range(K):
            kk = K - 1 - k if flip else k
            acc = acc + _shifted(win, CONV_HALO - pad + k, tl) * w_ref[kk:kk + 1, :]
        o_ref[...] = acc

    return _pcall(
        body, name=name, grid=(C // LANES, nt),
        in_specs=[prev, cur, nxt, pl.BlockSpec((K, LANES), lambda c, i: (0, c))],
        out_specs=cur, out_shape=jax.ShapeDtypeStruct((L, C), F32),
        compiler_params=_params(("parallel", "parallel")),
    )(u, u, u, w)


def _conv_dw(u, dy, K, n_ctx, name):
    L, C, tl, nct, nt = _conv_geometry(u, n_ctx)
    pad = (K - 1) // 2
    prev, cur, nxt = _conv_specs(tl, nt)

    def body(p_ref, c_ref, n_ref, dy_ref, dw_ref):
        i = pl.program_id(1)

        @pl.when(i == 0)
        def _():
            dw_ref[...] = jnp.zeros_like(dw_ref)

        win = _conv_window(p_ref, c_ref, n_ref, tl, nct, nt)
        dy_t = dy_ref[...]
        for k in range(K):
            dw_ref[k:k + 1, :] += jnp.sum(_shifted(win, CONV_HALO - pad + k, tl) * dy_t, axis=0, keepdims=True)

    return _pcall(
        body, name=name, grid=(C // LANES, nt),
        in_specs=[prev, cur, nxt, cur],
        out_specs=pl.BlockSpec((K, LANES), lambda c, i: (0, c)),
        out_shape=jax.ShapeDtypeStruct((K, C), F32),
        compiler_params=_params(("parallel", "arbitrary")),
    )(u, u, u, dy)


def _make_conv(n_ctx):
    @jax.custom_vjp
    def conv(u, w):
        return _conv_apply(u, w, n_ctx, False, "conv_fwd")

    def conv_fwd(u, w):
        return _conv_apply(u, w, n_ctx, False, "conv_fwd"), (u, w)

    def conv_bwd(res, dy):
        u, w = res
        return _conv_apply(dy, w, n_ctx, True, "conv_du"), _conv_dw(u, dy, w.shape[0], n_ctx, "conv_dw")

    conv.defvjp(conv_fwd, conv_bwd)
    return conv


def _gla_chunk(q, k, v, g, st, reverse):
    C = q.shape[0]
    r = lax.broadcasted_iota(jnp.int32, (C, C), 0)
    c = lax.broadcasted_iota(jnp.int32, (C, C), 1)
    seen = (r <= c) if reverse else (r >= c)
    dot = functools.partial(lax.dot_general, preferred_element_type=F32, precision=lax.Precision.DEFAULT)
    bcum = lax.dot_general(seen.astype(F32), g, (((1,), (0,)), ((), ())), preferred_element_type=F32,
                           precision=lax.Precision.HIGHEST)
    total = jnp.sum(g, axis=0, keepdims=True)
    a = q * (HEAD_W ** -0.5) * jnp.exp(bcum)
    scores = jnp.where(seen, dot(a, k * jnp.exp(-bcum), (((1,), (1,)), ((), ()))), 0.0)
    o = dot(a, st, (((1,), (1,)), ((), ()))) + dot(scores, v, (((1,), (0,)), ((), ())))
    st_new = st * jnp.exp(total) + dot(v, k * jnp.exp(total - bcum), (((0,), (0,)), ((), ())))
    return o, st_new


def _gla_order(t, nc, ncc, reverse):
    if not reverse:
        return t
    return jnp.where(t < ncc, ncc - 1 - t, ncc + nc - 1 - t)


def _gla_fwd_call(q, k, v, g, n_ctx, reverse):
    L = q.shape[0]
    C = GLA_CHUNK
    nc, ncc = L // C, n_ctx // C
    spec = pl.BlockSpec((C, MIX_W), lambda t: (_gla_order(t, nc, ncc, reverse), 0))

    def body(q_ref, k_ref, v_ref, g_ref, o_ref, s_ref, st):
        @pl.when(pl.program_id(0) == 0)
        def _():
            st[...] = jnp.zeros_like(st)

        for h in range(HEADS):
            hs = slice(h * HEAD_W, (h + 1) * HEAD_W)
            s_ref[h, 0] = st[h]
            o, st_new = _gla_chunk(q_ref[:, hs], k_ref[:, hs], v_ref[:, hs], g_ref[:, hs], st[h], reverse)
            o_ref[:, hs] = o
            st[h] = st_new

    return _pcall(
        body, name="gla_fwd", grid=(nc,), in_specs=[spec] * 4,
        out_specs=[spec, pl.BlockSpec((HEADS, 1, HEAD_W, HEAD_W), lambda t: (0, t, 0, 0))],
        out_shape=[jax.ShapeDtypeStruct((L, MIX_W), F32), jax.ShapeDtypeStruct((HEADS, nc, HEAD_W, HEAD_W), F32)],
        scratch_shapes=[pltpu.VMEM((HEADS, HEAD_W, HEAD_W), F32)],
        compiler_params=_params(("arbitrary",)),
    )(q, k, v, g)


def _gla_bwd_call(q, k, v, g, states, do, n_ctx, reverse):
    L = q.shape[0]
    C = GLA_CHUNK
    nc, ncc = L // C, n_ctx // C
    spec = pl.BlockSpec((C, MIX_W), lambda t: (_gla_order(nc - 1 - t, nc, ncc, reverse), 0))

    def body(q_ref, k_ref, v_ref, g_ref, s_ref, do_ref, dq_ref, dk_ref, dv_ref, dg_ref, dst):
        @pl.when(pl.program_id(0) == 0)
        def _():
            dst[...] = jnp.zeros_like(dst)

        for h in range(HEADS):
            hs = slice(h * HEAD_W, (h + 1) * HEAD_W)
            _, vjp = jax.vjp(functools.partial(_gla_chunk, reverse=reverse),
                             q_ref[:, hs], k_ref[:, hs], v_ref[:, hs], g_ref[:, hs], s_ref[h, 0])
            dq, dk, dv, dg, dst_prev = vjp((do_ref[:, hs], dst[h]))
            dq_ref[:, hs] = dq
            dk_ref[:, hs] = dk
            dv_ref[:, hs] = dv
            dg_ref[:, hs] = dg
            dst[h] = dst_prev

    return _pcall(
        body, name="gla_bwd", grid=(nc,),
        in_specs=[spec] * 4 + [pl.BlockSpec((HEADS, 1, HEAD_W, HEAD_W), lambda t: (0, nc - 1 - t, 0, 0)), spec],
        out_specs=[spec] * 4, out_shape=[jax.ShapeDtypeStruct((L, MIX_W), F32)] * 4,
        scratch_shapes=[pltpu.VMEM((HEADS, HEAD_W, HEAD_W), F32)],
        compiler_params=_params(("arbitrary",)),
    )(q, k, v, g, states, do)


def _make_gla(n_ctx, reverse):
    @jax.custom_vjp
    def gla(q, k, v, g):
        return _gla_fwd_call(q, k, v, g, n_ctx, reverse)[0]

    def gla_fwd(q, k, v, g):
        o, states = _gla_fwd_call(q, k, v, g, n_ctx, reverse)
        return o, (q, k, v, g, states)

    def gla_bwd(res, do):
        q, k, v, g, states = res
        return tuple(_gla_bwd_call(q, k, v, g, states, do, n_ctx, reverse))

    gla.defvjp(gla_fwd, gla_bwd)
    return gla


def _att_scaled(q_ref):
    return (q_ref[...] * ATT_SCALE).astype(BF16)


def _att_probs(qn, qr, kn, kr, i, nct, n_ctx):
    nt_dims = (((1,), (1,)), ((), ()))
    s = lax.dot_general(qn, kn, nt_dims, preferred_element_type=F32)
    s = s + lax.dot_general(qr, kr, nt_dims, preferred_element_type=F32)
    col = lax.broadcasted_iota(jnp.int32, (1, s.shape[1]), 1)
    s = s + jnp.where(col < jnp.where(i < nct, n_ctx, s.shape[1]), 0.0, -1e30)
    p = jnp.exp(s - jnp.max(s, axis=-1, keepdims=True))
    return p * (1.0 / jnp.sum(p, axis=-1, keepdims=True))


def _att_geometry(qn, n_ctx):
    L = qn.shape[0]
    tq = _pick(math.gcd(L, n_ctx), 256, 8)
    q_spec = pl.BlockSpec((tq, HEAD_W), lambda h, i: (i, h))
    k_spec = pl.BlockSpec((L, HEAD_W), lambda h, i: (0, h))
    kr_spec = pl.BlockSpec((L, HEAD_W), lambda h, i: (0, 0))
    return L, tq, n_ctx // tq, q_spec, k_spec, kr_spec


def _att_fwd_call(qn, qr, kn, kr, v, n_ctx):
    L, tq, nct, q_spec, k_spec, kr_spec = _att_geometry(qn, n_ctx)

    def body(qn_ref, qr_ref, kn_ref, kr_ref, v_ref, o_ref):
        p = _att_probs(_att_scaled(qn_ref), _att_scaled(qr_ref), kn_ref[...].astype(BF16),
                       kr_ref[...].astype(BF16), pl.program_id(1), nct, n_ctx)
        o_ref[...] = jnp.dot(p.astype(BF16), v_ref[...].astype(BF16), preferred_element_type=F32).astype(BF16)

    return _pcall(
        body, name="att_fwd", grid=(HEADS, L // tq),
        in_specs=[q_spec, q_spec, k_spec, kr_spec, k_spec], out_specs=q_spec,
        out_shape=jax.ShapeDtypeStruct((L, MIX_W), BF16),
        compiler_params=_params(("parallel", "parallel")),
    )(qn, qr, kn, kr, v)


def _att_bwd_call(qn, qr, kn, kr, v, do, n_ctx, payload=()):
    L, tq, nct, q_spec, k_spec, kr_spec = _att_geometry(qn, n_ctx)
    tn_dims = (((0,), (0,)), ((), ()))
    nq, npay = L // tq, len(payload)

    def body(*refs):
        (qn_ref, qr_ref, kn_ref, kr_ref, v_ref, do_ref), refs = refs[:6], refs[6:]
        p_refs, (dqn_ref, dqr_ref, dkn_ref, dkr_ref, dv_ref) = refs[:npay], refs[npay:npay + 5]
        h, i = pl.program_id(0), pl.program_id(1)
        if npay:
            exchange = _ChipExchange(p_refs, refs[npay + 5:2 * npay + 5], *refs[2 * npay + 5:])
            pl.when(jnp.logical_and(h == 0, i == 0))(exchange.start)
        qn, qr = _att_scaled(qn_ref), _att_scaled(qr_ref)
        kn, kr, vv = kn_ref[...].astype(BF16), kr_ref[...].astype(BF16), v_ref[...].astype(BF16)
        do = do_ref[...].astype(BF16)
        p = _att_probs(qn, qr, kn, kr, i, nct, n_ctx)
        dp = lax.dot_general(do, vv, (((1,), (1,)), ((), ())), preferred_element_type=F32)
        ds = (p * (dp - jnp.sum(p * dp, axis=-1, keepdims=True))).astype(BF16)
        dqn_ref[...] = jnp.dot(ds, kn, preferred_element_type=F32) * ATT_SCALE
        dqr_ref[...] = jnp.dot(ds, kr, preferred_element_type=F32) * ATT_SCALE

        @pl.when(i == 0)
        def _():
            dkn_ref[...] = jnp.zeros_like(dkn_ref)
            dv_ref[...] = jnp.zeros_like(dv_ref)

        @pl.when(jnp.logical_and(i == 0, h == 0))
        def _():
            dkr_ref[...] = jnp.zeros_like(dkr_ref)

        dkn_ref[...] += lax.dot_general(ds, qn, tn_dims, preferred_element_type=F32)
        dkr_ref[...] += lax.dot_general(ds, qr, tn_dims, preferred_element_type=F32)
        dv_ref[...] += lax.dot_general(p.astype(BF16), do, tn_dims, preferred_element_type=F32)
        if npay:
            pl.when(jnp.logical_and(h == HEADS - 1, i == nq - 1))(exchange.finish)

    outs = _pcall(
        body, name="att_bwd", grid=(HEADS, nq),
        in_specs=[q_spec, q_spec, k_spec, kr_spec, k_spec, q_spec] + [HBM] * npay,
        out_specs=[q_spec, q_spec, k_spec, kr_spec, k_spec] + [HBM] * npay,
        out_shape=[jax.ShapeDtypeStruct((L, MIX_W), F32)] * 3
        + [jax.ShapeDtypeStruct((L, HEAD_W), F32), jax.ShapeDtypeStruct((L, MIX_W), F32)]
        + [jax.ShapeDtypeStruct((3,) + p.shape[1:], p.dtype) for p in payload],
        scratch_shapes=_ChipExchange.scratch(npay) if npay else [],
        compiler_params=_params(("arbitrary", "arbitrary")),
    )(qn, qr, kn, kr, v, do, *payload)
    return tuple(outs[:5]), tuple(outs[5:])


def _make_attention(n_ctx):
    @jax.custom_vjp
    def att(qn, qr, kn, kr, v, slots):
        return _att_fwd_call(qn, qr, kn, kr, v, n_ctx), tuple(lax.empty((4,) + s.shape[1:], s.dtype) for s in slots)

    def att_fwd(qn, qr, kn, kr, v, slots):
        return att(qn, qr, kn, kr, v, slots), (qn, qr, kn, kr, v)

    def att_bwd(res, cts):
        grads, arrived = _att_bwd_call(*res, cts[0], n_ctx, tuple(cts[1]))
        return (*grads, arrived)

    att.defvjp(att_fwd, att_bwd)
    return att


def _loss_call(x, g, target):
    L, D = x.shape
    tl = _pick(L, 256, 8)

    def f(xv, gv, tv):
        err = _rms(xv, gv) - tv
        return 0.5 * jnp.sum(err * err, axis=0, keepdims=True) / D

    def body(x_ref, g_ref, t_ref, loss_ref, dx_ref, dg_ref):
        i = pl.program_id(0)
        loss, vjp = jax.vjp(lambda xv, gv: f(xv, gv, t_ref[...]), x_ref[...], g_ref[...])
        dx, dg = vjp(jnp.ones_like(loss))
        dx_ref[...] = dx

        @pl.when(i == 0)
        def _():
            loss_ref[...] = loss
            dg_ref[...] = dg

        @pl.when(i != 0)
        def _():
            loss_ref[...] += loss
            dg_ref[...] += dg

    row = pl.BlockSpec((tl, D), lambda i: (i, 0))
    one = pl.BlockSpec((1, D), lambda i: (0, 0))
    return _pcall(
        body, name="loss", grid=(L // tl,), in_specs=[row, one, row], out_specs=[one, row, one],
        out_shape=[jax.ShapeDtypeStruct((1, D), F32), jax.ShapeDtypeStruct((L, D), F32),
                   jax.ShapeDtypeStruct((1, D), F32)],
        compiler_params=_params(("arbitrary",)),
    )(x, g, target)


def _sum_leading(x, name):
    n, R, W = x.shape
    tr = _pick(R, 512, 8)

    def body(x_ref, o_ref):
        acc = x_ref[0]
        for d in range(1, n):
            acc = acc + x_ref[d]
        o_ref[...] = acc

    return _pcall(
        body, name=name, grid=(R // tr,), in_specs=[pl.BlockSpec((n, tr, W), lambda i: (0, i, 0))],
        out_specs=pl.BlockSpec((tr, W), lambda i: (i, 0)), out_shape=jax.ShapeDtypeStruct((R, W), F32),
        compiler_params=_params(("parallel",)),
    )(x)


def _adamw(w, g, m, v):
    shape = w.shape
    W = shape[-1]
    as2d = lambda t: t.reshape(-1, W)
    R = as2d(w).shape[0]
    tr = _pick(R, max(8, (2 ** 17 // W) // 8 * 8), 8)

    def body(w_ref, g_ref, m_ref, v_ref, d_ref, nm_ref, nv_ref):
        gv = g_ref[...]
        m_new = ADAM_B1 * m_ref[...] + (1.0 - ADAM_B1) * gv
        v_new = ADAM_B2 * v_ref[...] + (1.0 - ADAM_B2) * (gv * gv)
        m_hat = m_new / (1.0 - ADAM_B1 ** ADAM_STEP)
        v_hat = v_new / (1.0 - ADAM_B2 ** ADAM_STEP)
        d_ref[...] = -ADAM_LR * (m_hat / (jnp.sqrt(v_hat) + ADAM_EPS) + ADAM_WD * w_ref[...])
        nm_ref[...] = m_new
        nv_ref[...] = v_new

    spec = pl.BlockSpec((tr, W), lambda i: (i, 0))
    outs = _pcall(
        body, name="adamw", grid=(R // tr,), in_specs=[spec] * 4, out_specs=[spec] * 3,
        out_shape=[jax.ShapeDtypeStruct((R, W), F32)] * 3, compiler_params=_params(("parallel",)),
    )(as2d(w), as2d(g), as2d(m), as2d(v))
    return tuple(o.reshape(shape) for o in outs)


HBM = pl.BlockSpec(memory_space=pltpu.HBM)


def _place():
    x, y, c = lax.axis_index("x"), lax.axis_index("y"), lax.axis_index("c")
    return x, y, c, [(1 - x, y), (x, 1 - y), (1 - x, 1 - y)]


class _Gather:
    def __init__(self, x_refs, out_refs, send_sems, recv_sems, local_sems):
        self.x_refs, self.out_refs, self.n = x_refs, out_refs, len(x_refs)
        self.sems = send_sems, recv_sems, local_sems
        self.x, self.y, self.c, self.chips = _place()
        self.me, self.sibling = (self.x, self.y, self.c), (self.x, self.y, 1 - self.c)

    @staticmethod
    def scratch(n):
        return [pltpu.SemaphoreType.DMA((7 * n,)), pltpu.SemaphoreType.DMA((7 * n,)), pltpu.SemaphoreType.DMA((n,))]

    def slot(self, a, px, py, pc):
        return self.out_refs[a].at[4 * px + 2 * py + pc]

    def copy(self, a, k, blk, to, src=None):
        return pltpu.make_async_remote_copy(
            src_ref=self.slot(a, *blk) if src is None else src, dst_ref=self.slot(a, *blk),
            send_sem=self.sems[0].at[7 * a + k], recv_sem=self.sems[1].at[7 * a + k], device_id=to,
            device_id_type=MESH)

    def own(self):
        first = []
        for a in range(self.n):
            first.append(self.copy(a, 0, self.me, self.sibling, src=self.x_refs[a]))
            first += [self.copy(a, 1 + j, self.me, (*chip, self.c), src=self.x_refs[a])
                      for j, chip in enumerate(self.chips)]
        return [pltpu.make_async_copy(self.x_refs[a], self.slot(a, *self.me), self.sems[2].at[a])
                for a in range(self.n)], first

    def start(self):
        mine, first = self.own()
        for cp in mine + first:
            cp.start()

    def finish(self):
        c, chips = self.c, self.chips
        mine, first = self.own()
        passed = []
        for j, chip in enumerate(chips):
            for a in range(self.n):
                self.copy(a, 1 + j, (*chip, c), self.me).wait_recv()
                passed.append(self.copy(a, 4 + j, (*chip, c), self.sibling))
                passed[-1].start()
        for a in range(self.n):
            self.copy(a, 0, self.sibling, self.me).wait_recv()
        for j, chip in enumerate(chips):
            for a in range(self.n):
                self.copy(a, 4 + j, (*chip, 1 - c), self.me).wait_recv()
        for cp in first + passed:
            cp.wait_send()
        for cp in mine:
            cp.wait()


def _all_gather(blocks, name):
    n = len(blocks)

    def body(*refs):
        g = _Gather(refs[:n], refs[n:2 * n], *refs[2 * n:])
        g.start()
        g.finish()

    return _pcall(
        body, name=name, out_shape=[jax.ShapeDtypeStruct((N_DEV,) + b.shape, b.dtype) for b in blocks],
        in_specs=[HBM] * n, out_specs=[HBM] * n, scratch_shapes=_Gather.scratch(n),
    )(*blocks)


def _send_to_sibling(gs, name):
    n = len(gs)

    def body(*refs):
        g_refs, out_refs, (send_sems, recv_sems) = refs[:n], refs[n:2 * n], refs[2 * n:]
        x, y, c, _ = _place()
        copies = [pltpu.make_async_remote_copy(
            src_ref=g_refs[a].at[2 * q + 1 - c], dst_ref=out_refs[a].at[q], send_sem=send_sems.at[4 * a + q],
            recv_sem=recv_sems.at[4 * a + q], device_id=(x, y, 1 - c), device_id_type=MESH)
            for a in range(n) for q in range(4)]
        for cp in copies:
            cp.start()
        for cp in copies:
            cp.wait()

    return _pcall(
        body, name=name, out_shape=[jax.ShapeDtypeStruct((4,) + g.shape[1:], g.dtype) for g in gs],
        in_specs=[HBM] * n, out_specs=[HBM] * n,
        scratch_shapes=[pltpu.SemaphoreType.DMA((4 * n,)), pltpu.SemaphoreType.DMA((4 * n,))],
    )(*gs)


class _ChipExchange:
    def __init__(self, p_refs, out_refs, send_sems, recv_sems):
        self.p_refs, self.out_refs, self.sems = p_refs, out_refs, (send_sems, recv_sems)

    @staticmethod
    def scratch(n):
        return [pltpu.SemaphoreType.DMA((3 * n,)), pltpu.SemaphoreType.DMA((3 * n,))]

    def copies(self):
        x, y, c, chips = _place()
        return [pltpu.make_async_remote_copy(
            src_ref=self.p_refs[a].at[2 * cx + cy], dst_ref=self.out_refs[a].at[j], send_sem=self.sems[0].at[3 * a + j],
            recv_sem=self.sems[1].at[3 * a + j], device_id=(cx, cy, c), device_id_type=MESH)
            for a in range(len(self.p_refs)) for j, (cx, cy) in enumerate(chips)]

    def start(self):
        for cp in self.copies():
            cp.start()

    def finish(self):
        for cp in self.copies():
            cp.wait()


def _send_to_chips(ps, name):
    n = len(ps)

    def body(*refs):
        exchange = _ChipExchange(refs[:n], refs[n:2 * n], *refs[2 * n:])
        exchange.start()
        exchange.finish()

    return _pcall(
        body, name=name, out_shape=[jax.ShapeDtypeStruct((3,) + p.shape[1:], p.dtype) for p in ps],
        in_specs=[HBM] * n, out_specs=[HBM] * n, scratch_shapes=_ChipExchange.scratch(n),
    )(*ps)


def _add_rows(R, W):
    return _pick(R, max(16, 2 ** 19 // W // 16 * 16), 16)


def _add_sibling(g, recv, core):
    _, R, W = g.shape
    tr = _add_rows(R, W)

    def body(core_ref, g_ref, r_ref, o_ref):
        o_ref[...] = (g_ref[...].astype(F32) + r_ref[...].astype(F32)).astype(BF16)

    return _pcall(
        body, name="rs_add_sibling",
        grid_spec=pltpu.PrefetchScalarGridSpec(
            num_scalar_prefetch=1, grid=(4, R // tr),
            in_specs=[pl.BlockSpec((None, tr, W), lambda q, i, core_ref: (2 * q + core_ref[0], i, 0)),
                      pl.BlockSpec((None, tr, W), lambda q, i, core_ref: (q, i, 0))],
            out_specs=pl.BlockSpec((None, tr, W), lambda q, i, core_ref: (q, i, 0))),
        out_shape=jax.ShapeDtypeStruct((4, R, W), BF16), compiler_params=_params(("parallel", "parallel")),
    )(core, g, recv)


def _add_chips(p, recv, chip):
    _, R, W = p.shape
    tr = _add_rows(R, W)

    def body(chip_ref, p_ref, r_ref, o_ref):
        up = lambda t: t.astype(F32)
        o_ref[...] = ((up(p_ref[...]) + up(r_ref[0])) + up(r_ref[1])) + up(r_ref[2])

    return _pcall(
        body, name="rs_add_chips",
        grid_spec=pltpu.PrefetchScalarGridSpec(
            num_scalar_prefetch=1, grid=(R // tr,),
            in_specs=[pl.BlockSpec((None, tr, W), lambda i, chip_ref: (chip_ref[0], i, 0)),
                      pl.BlockSpec((3, tr, W), lambda i, chip_ref: (0, i, 0))],
            out_specs=pl.BlockSpec((tr, W), lambda i, chip_ref: (i, 0))),
        out_shape=jax.ShapeDtypeStruct((R, W), F32), compiler_params=_params(("parallel",)),
    )(chip, p, recv)


def _chip_sums(gs):
    core = jnp.reshape(lax.axis_index("c"), (1,)).astype(jnp.int32)
    return [_add_sibling(g, r, core) for g, r in zip(gs, _send_to_sibling(gs, "rs_sibling"))]


def _sum_chips(chip_sums, from_chips):
    chip = jnp.reshape(2 * lax.axis_index("x") + lax.axis_index("y"), (1,)).astype(jnp.int32)
    return [_add_chips(p, r, chip) for p, r in zip(chip_sums, from_chips)]


def _rope_tables(seq, n_ctx):
    rows = seq // GRID_W
    row = jnp.repeat(jnp.arange(rows, dtype=F32), GRID_W)
    col = jnp.tile(jnp.arange(GRID_W, dtype=F32), rows)
    inv = ROPE_BASE ** (-jnp.arange(ROPE_FREQS, dtype=F32) * 2.0 / (ROPE // 2))
    ang_r, ang_c = row[:, None] * inv, col[:, None] * inv
    one, zero = jnp.ones((seq, ROPE), F32), jnp.zeros((seq, ROPE), F32)
    cos = jnp.concatenate([jnp.cos(ang_r), jnp.cos(ang_r), jnp.cos(ang_c), jnp.cos(ang_c), one], axis=1)
    sin = jnp.concatenate([-jnp.sin(ang_r), jnp.sin(ang_r), -jnp.sin(ang_c), jnp.sin(ang_c), zero], axis=1)
    cos = jnp.concatenate([jnp.ones((n_ctx, LANES), F32), cos], axis=0)
    sin = jnp.concatenate([jnp.zeros((n_ctx, LANES), F32), sin], axis=0)
    return cos, sin


def _shared(v):
    return v.reshape((1, 1, -1)) if v.ndim == 1 else v.reshape((1,) + v.shape)


Z_BOUNDS = (Z_Q, Z_K, Z_V, Z_G, Z_LR, Z_A, Z_GATE, Z_BG, Z_CG, Z_H, Z_CQ, Z_CKV, Z_KR, Z_END)
PLAIN = ("w_in", "w_qn", "w_qr", "w_kn", "w_v")
COLS = ("ffn_w1", "ffn_w3")
ROWS = ("w_out", "ffn_w2")


def _layer(t, p, car, mod_l, mod_c, slots, G, nxt, n_ctx, tables):
    L, D = t.shape
    seg = lambda i: jnp.stack([mod_c[i * D:(i + 1) * D], mod_l[i * D:(i + 1) * D]]).reshape(2, 1, D)
    sh1, sc1, g1, sh2, sc2, g2 = (seg(i) for i in range(6))
    plain = _plain_weights(G, D)
    dense = lambda a, n: mm(a, plain[n], car[n], ())[0]
    take = lambda *names: tuple(nxt[n] for n in names) if nxt else ()

    (h,) = _rowop(_f_norm_mod, "norm_mod", n_ctx, out_dtypes=(BF16,))((t,), (), (_shared(p["norm1_g"]), sc1, sh1))
    z, got_in = mm(h, plain["w_in"], car["w_in"], take("w_in"))
    q, k, v, gate, lr, conf_a, conf_gate, sc_b, sc_c, sc_h, cq, ckv, kr = _split_cols(z, Z_BOUNDS)

    up = p["gla_fg_up"]
    up_f = jnp.pad(up[0], ((0, LANES - GATE_RANK), (0, 0)))
    up_b = jnp.pad(up[1], ((GATE_RANK, LANES - 2 * GATE_RANK), (0, 0)))
    logd_f, logd_b = _rowop(_f_gla_gate, "gla_gate")(
        (lr,), (), (_shared(up_f), _shared(up_b), _shared(p["gla_fg_b"][0]), _shared(p["gla_fg_b"][1])))
    o_f = _make_gla(n_ctx, False)(q, k, v, logd_f)
    o_b = _make_gla(n_ctx, True)(q, k, v, logd_b)
    (gla,) = _rowop(_f_gla_finish, "gla_finish", lane_block=HEAD_W, out_dtypes=(BF16,))(
        (o_f, o_b, gate), (), (_shared(p["gla_onorm_g"]),))

    conv = _make_conv(n_ctx)
    (u,) = _rowop(_f_glu, "glu")((conf_a, conf_gate), (), ())
    (conf,) = _rowop(_f_ln_silu, "ln_silu", out_dtypes=(BF16,))(
        (conv(u, p["conf_dw"]),), (), (_shared(p["conf_dw_b"]), _shared(p["conf_ln_g"]), _shared(p["conf_ln_b"])))

    (ch,) = _rowop(_f_mul, "mul")((sc_c, sc_h), (), ())
    (sconv,) = _rowop(_f_mul, "mul_out", out_dtypes=(BF16,))((sc_b, conv(ch, p["sc_dw"])), (), ())

    (cq,) = _rowop(_f_rms, "rms")((cq,), (), (_shared(p["mla_q_norm_g"]),))
    (ckv,) = _rowop(_f_rms, "rms")((ckv,), (), (_shared(p["mla_kv_norm_g"]),))
    rope = _rowop(_f_rope, "rope", lane_block=LANES)
    (qr,) = rope((dense(cq, "w_qr"),), tables, ())
    (kr,) = rope((kr,), tables, ())
    mla, tokens = _make_attention(n_ctx)(dense(cq, "w_qn"), qr, dense(ckv, "w_kn"), kr, dense(ckv, "w_v"), slots)

    o, got_out = _make_mm_rows(0)(jnp.concatenate([gla, conf, sconv, mla], axis=1), G["w_out"], car["w_out"],
                                  take("w_out", "mla_w_uq", "mla_w_ukv"))
    t1, h2 = _rowop(_f_resid_norm_mod, "resid_norm_mod", n_ctx, out_dtypes=(F32, BF16))(
        (t, o), (), (g1, _shared(p["norm2_g"]), sc2, sh2))
    a1, got_w1 = _make_mm_cols(0)(h2, G["ffn_w1"], car["ffn_w1"], take("ffn_w1"))
    a3, got_w3 = _make_mm_cols(0)(h2, G["ffn_w3"], car["ffn_w3"], take("ffn_w3"))
    n_ff = a1.shape[2]
    (act,) = _rowop(_f_swiglu, "swiglu", tile=1024, out_dtypes=(BF16,))(
        (a1.reshape(N_DEV * L, n_ff), a3.reshape(N_DEV * L, n_ff)), (), ())
    f, got_w2 = _make_mm_rows(0)(act.reshape(N_DEV, L, n_ff), G["ffn_w2"], car["ffn_w2"], take("ffn_w2"))
    (t2,) = _rowop(_f_resid, "resid", n_ctx)((t1, f), (), (g2,))
    got = got_in + got_out + got_w1 + got_w3 + got_w2
    gathered = dict(zip(("w_in", "w_out", "mla_w_uq", "mla_w_ukv", "ffn_w1", "ffn_w3", "ffn_w2"), got)) if nxt else None
    return (t2, tokens), gathered


def _plain_weights(G, D):
    full = lambda n: jnp.concatenate([G[n][d, 0] for d in range(N_DEV)], axis=1)
    w_in = full("w_in")
    w_in = jnp.concatenate([w_in[:, :Z_LR + 2 * GATE_RANK], jnp.zeros((D, Z_A - Z_LR - 2 * GATE_RANK), BF16),
                            w_in[:, Z_LR + 2 * GATE_RANK:], jnp.zeros((D, Z_END - Z_KR - ROPE), BF16)], axis=1)
    w_uq = full("mla_w_uq").reshape(Q_RANK, HEADS, HEAD_W + ROPE)
    w_ukv = full("mla_w_ukv").reshape(KV_RANK, HEADS, 2 * HEAD_W)
    return {"w_in": w_in,
            "w_qn": w_uq[:, :, :HEAD_W].reshape(Q_RANK, MIX_W),
            "w_qr": jnp.pad(w_uq[:, :, HEAD_W:], ((0, 0), (0, 0), (0, LANES - ROPE))).reshape(Q_RANK, HEADS * LANES),
            "w_kn": w_ukv[:, :, :HEAD_W].reshape(KV_RANK, MIX_W),
            "w_v": w_ukv[:, :, HEAD_W:].reshape(KV_RANK, MIX_W)}


def _col_slabs(full):
    n = full.shape[1] // N_DEV
    return jnp.stack([full[:, d * n:(d + 1) * n] for d in range(N_DEV)])


def _shard_grads(d_car):
    d_in = d_car["w_in"]
    d_in = jnp.concatenate([d_in[:, :Z_LR + 2 * GATE_RANK], d_in[:, Z_A:Z_KR + ROPE]], axis=1)
    by_head = lambda g: g.reshape(g.shape[0], HEADS, -1)
    d_uq = jnp.concatenate([by_head(d_car["w_qn"]), by_head(d_car["w_qr"])[:, :, :ROPE]], axis=2)
    d_ukv = jnp.concatenate([by_head(d_car["w_kn"]), by_head(d_car["w_v"])], axis=2)
    out = {"w_in": _col_slabs(d_in), "mla_w_uq": _col_slabs(d_uq.reshape(Q_RANK, -1)),
           "mla_w_ukv": _col_slabs(d_ukv.reshape(KV_RANK, -1))}
    out.update({n: d_car[n] for n in COLS + ROWS})
    return out


BIG = ("w_in", "w_out", "ffn_w1", "ffn_w3", "ffn_w2", "mla_w_uq", "mla_w_ukv")
SMALL_SHARED = ("norm1_g", "gla_onorm_g", "conf_dw_b", "conf_ln_g", "conf_ln_b", "mla_q_norm_g", "mla_kv_norm_g",
                "norm2_g")
SMALL_SHARDED = ("gla_fg_up", "gla_fg_b", "conf_dw", "sc_dw")
WEIGHTS = ("c_ctx", "norm1_g", "w_mod", "b_mod", "w_in", "gla_fg_up", "gla_fg_b", "gla_onorm_g", "conf_dw",
           "conf_dw_b", "conf_ln_g", "conf_ln_b", "sc_dw", "mla_q_norm_g", "mla_kv_norm_g", "mla_w_uq", "mla_w_ukv",
           "w_out", "norm2_g", "ffn_w1", "ffn_w3", "ffn_w2", "final_norm_g")


def _gather_last(pieces8):
    moved = jnp.moveaxis(pieces8, 0, -2)
    return moved.reshape(moved.shape[:-2] + (-1,))


def _sum_devices(x8, name):
    shape = x8.shape[1:]
    return _sum_leading(x8.reshape(N_DEV, -1, shape[-1]), name).reshape(shape)


def _step(w, m, v, x, c, ctx, loss_target):
    depth = w["norm1_g"].shape[0]
    seq, D = x.shape[1], x.shape[2]
    n_ctx = ctx.shape[1]
    me = 4 * lax.axis_index("x") + 2 * lax.axis_index("y") + lax.axis_index("c")

    shards = [{n: w[n][i:i + 1].astype(BF16) for n in BIG} for i in range(depth)]
    G = dict(zip(BIG, _all_gather([shards[0][n] for n in BIG], "gather_weights")))
    small8 = _all_gather([c] + [w[n] for n in SMALL_SHARDED], "gather_small")
    c_all = small8[0].reshape(N_DEV, D)
    small_full = {n: _gather_last(g) for n, g in zip(SMALL_SHARDED, small8[1:])}

    rows = jnp.concatenate([c_all, w["c_ctx"][None], jnp.zeros((16 - N_DEV - 1, D), F32)])
    (act,) = _rowop(_f_silu, "silu")((rows,), (), ())
    n_mod = w["w_mod"].shape[2]
    b_mine = lax.dynamic_slice_in_dim(w["b_mod"], me * n_mod, n_mod, axis=1)
    mod_cols = [_rowop(_f_add_bias, "add_bias")((_matmul(act, w["w_mod"][i], "nn", "mod_fwd"),), (),
                                                (_shared(b_mine[i]),))[0] for i in range(depth)]
    (mods8,) = _all_gather([jnp.stack(mod_cols)], "gather_mod")
    mods = jnp.moveaxis(mods8, 0, 2).reshape(depth, 16, N_DEV * n_mod)
    mods_l = [lax.dynamic_index_in_dim(mods[i], me, 0, keepdims=False) for i in range(depth)]
    mods_c = [mods[i, N_DEV] for i in range(depth)]

    smalls, cars = [], []
    plain_shapes = {"w_in": (D, Z_END), "w_qn": (Q_RANK, MIX_W), "w_qr": (Q_RANK, HEADS * LANES),
                    "w_kn": (KV_RANK, MIX_W), "w_v": (KV_RANK, MIX_W)}
    for i in range(depth):
        p = {n: w[n][i] for n in SMALL_SHARED}
        p.update({n: small_full[n][i] for n in SMALL_SHARDED})
        smalls.append(p)
        car = {n: lax.empty(s, BF16) for n, s in plain_shapes.items()}
        car.update({n: lax.empty((N_DEV,) + w[n].shape[1:], BF16) for n in COLS + ROWS})
        cars.append(car)
    t = jnp.concatenate([ctx[0], x[0]], axis=0)
    tables = _rope_tables(seq, n_ctx)
    vjps = []
    for i in range(depth):
        last = i + 1 == depth
        slots = () if last else tuple(lax.empty((3,) + w[n].shape[1:], BF16) for n in BIG)
        (t, _), vjp_i, G = jax.vjp(
            lambda t_, p_, c_, ml_, mc_, s_, G=G, nxt=None if last else shards[i + 1]: _layer(
                t_, p_, c_, ml_, mc_, s_, G, nxt, n_ctx, tables),
            t, smalls[i], cars[i], mods_l[i], mods_c[i], slots, has_aux=True)
        vjps.append(vjp_i)
    loss_lanes, d_out, d_final_g = _loss_call(t[n_ctx:], w["final_norm_g"][None], loss_target[0])
    d_t = jnp.concatenate([jnp.zeros((n_ctx, D), F32), d_out], axis=0)

    grads = {n: [None] * depth for n in BIG}
    d_smalls, d_mods_l, d_mods_c = [None] * depth, [None] * depth, [None] * depth
    pending = ()
    for i in reversed(range(depth)):
        d_t, d_smalls[i], d_car, d_mods_l[i], d_mods_c[i], arrived = vjps[i]((d_t, tuple(pending)))
        for n, g in zip(BIG, _sum_chips(pending, arrived)):
            grads[n][i + 1] = g
        sharded = _shard_grads(d_car)
        pending = _chip_sums([sharded[n] for n in BIG])
    for n, g in zip(BIG, _sum_chips(pending, _send_to_chips(pending, "rs_chips"))):
        grads[n][0] = g
    grads = {n: jnp.stack(g) for n, g in grads.items()}
    grad_x = d_t[n_ctx:][None]

    names = SMALL_SHARED + SMALL_SHARDED
    d_mod = jnp.stack([jnp.stack([d_mods_l[i], d_mods_c[i]]) for i in range(depth)])
    parts = [loss_lanes, d_final_g, d_mod] + [jnp.stack([d_smalls[i][n] for i in range(depth)]) for n in names]
    parts8 = _all_gather(parts, "gather_partials")
    summed = [_sum_devices(p8, "sum_partials") for p8 in parts8]
    (loss_row,) = _rowop(lambda a: (jnp.sum(a, axis=-1, keepdims=True) + jnp.zeros_like(a),), "loss_sum")(
        (summed[0],), (), ())
    loss = loss_row[0, 0]
    grads["final_norm_g"] = summed[1].reshape(D)
    for n, g in zip(names, summed[3:]):
        if n in SMALL_SHARDED:
            g = lax.dynamic_slice_in_dim(g, me * w[n].shape[-1], w[n].shape[-1], axis=g.ndim - 1)
        grads[n] = g

    d_mod8, d_mod_c = parts8[2], summed[2][:, 1]
    grads["b_mod"] = _rowop(lambda a, b: (a + b,), "add")((summed[2][:, 0], d_mod_c), (), ())[0]
    d_rows = jnp.concatenate([jnp.moveaxis(d_mod8[:, :, 0], 0, 1), d_mod_c[:, None],
                              jnp.zeros((depth, 16 - N_DEV - 1, 6 * D), F32)], axis=1)
    d_rows = lax.dynamic_slice_in_dim(d_rows, me * n_mod, n_mod, axis=2)
    grads["w_mod"] = jnp.stack([_matmul(act, d_rows[i], "tn", "mod_dw") for i in range(depth)])
    d_act = _matmul(d_rows.transpose(1, 0, 2).reshape(16, depth * n_mod),
                    w["w_mod"].transpose(1, 0, 2).reshape(D, depth * n_mod), "nt", "mod_dact")
    (d_act8,) = _all_gather([d_act], "gather_dact")
    (d_rows_in,) = _rowop(_f_mul_silu_grad, "silu_grad")((_sum_devices(d_act8, "sum_dact"), rows), (), ())
    grads["c_ctx"] = d_rows_in[N_DEV]

    outs = {n: _adamw(w[n], grads[n], m[n], v[n]) for n in WEIGHTS}
    return (loss, grad_x, *[grads[n] for n in WEIGHTS], *[outs[n][0] for n in WEIGHTS],
            *[outs[n][1] for n in WEIGHTS], *[outs[n][2] for n in WEIGHTS])


def kernel(x, c, ctx, c_ctx, norm1_g, w_mod, b_mod, w_in, gla_fg_up, gla_fg_b, gla_onorm_g, conf_dw, conf_dw_b, conf_ln_g, conf_ln_b, sc_dw, mla_q_norm_g, mla_kv_norm_g, mla_w_uq, mla_w_ukv, w_out, norm2_g, ffn_w1, ffn_w3, ffn_w2, final_norm_g, loss_target, m_c_ctx, m_norm1_g, m_w_mod, m_b_mod, m_w_in, m_gla_fg_up, m_gla_fg_b, m_gla_onorm_g, m_conf_dw, m_conf_dw_b, m_conf_ln_g, m_conf_ln_b, m_sc_dw, m_mla_q_norm_g, m_mla_kv_norm_g, m_mla_w_uq, m_mla_w_ukv, m_w_out, m_norm2_g, m_ffn_w1, m_ffn_w3, m_ffn_w2, m_final_norm_g, v_c_ctx, v_norm1_g, v_w_mod, v_b_mod, v_w_in, v_gla_fg_up, v_gla_fg_b, v_gla_onorm_g, v_conf_dw, v_conf_dw_b, v_conf_ln_g, v_conf_ln_b, v_sc_dw, v_mla_q_norm_g, v_mla_kv_norm_g, v_mla_w_uq, v_mla_w_ukv, v_w_out, v_norm2_g, v_ffn_w1, v_ffn_w3, v_ffn_w2, v_final_norm_g):
    w = dict(c_ctx=c_ctx, norm1_g=norm1_g, w_mod=w_mod, b_mod=b_mod, w_in=w_in, gla_fg_up=gla_fg_up, gla_fg_b=gla_fg_b, gla_onorm_g=gla_onorm_g, conf_dw=conf_dw, conf_dw_b=conf_dw_b, conf_ln_g=conf_ln_g, conf_ln_b=conf_ln_b, sc_dw=sc_dw, mla_q_norm_g=mla_q_norm_g, mla_kv_norm_g=mla_kv_norm_g, mla_w_uq=mla_w_uq, mla_w_ukv=mla_w_ukv, w_out=w_out, norm2_g=norm2_g, ffn_w1=ffn_w1, ffn_w3=ffn_w3, ffn_w2=ffn_w2, final_norm_g=final_norm_g)
    m = dict(c_ctx=m_c_ctx, norm1_g=m_norm1_g, w_mod=m_w_mod, b_mod=m_b_mod, w_in=m_w_in, gla_fg_up=m_gla_fg_up, gla_fg_b=m_gla_fg_b, gla_onorm_g=m_gla_onorm_g, conf_dw=m_conf_dw, conf_dw_b=m_conf_dw_b, conf_ln_g=m_conf_ln_g, conf_ln_b=m_conf_ln_b, sc_dw=m_sc_dw, mla_q_norm_g=m_mla_q_norm_g, mla_kv_norm_g=m_mla_kv_norm_g, mla_w_uq=m_mla_w_uq, mla_w_ukv=m_mla_w_ukv, w_out=m_w_out, norm2_g=m_norm2_g, ffn_w1=m_ffn_w1, ffn_w3=m_ffn_w3, ffn_w2=m_ffn_w2, final_norm_g=m_final_norm_g)
    v = dict(c_ctx=v_c_ctx, norm1_g=v_norm1_g, w_mod=v_w_mod, b_mod=v_b_mod, w_in=v_w_in, gla_fg_up=v_gla_fg_up, gla_fg_b=v_gla_fg_b, gla_onorm_g=v_gla_onorm_g, conf_dw=v_conf_dw, conf_dw_b=v_conf_dw_b, conf_ln_g=v_conf_ln_g, conf_ln_b=v_conf_ln_b, sc_dw=v_sc_dw, mla_q_norm_g=v_mla_q_norm_g, mla_kv_norm_g=v_mla_kv_norm_g, mla_w_uq=v_mla_w_uq, mla_w_ukv=v_mla_w_ukv, w_out=v_w_out, norm2_g=v_norm2_g, ffn_w1=v_ffn_w1, ffn_w3=v_ffn_w3, ffn_w2=v_ffn_w2, final_norm_g=v_final_norm_g)
    return _step(w, m, v, x, c, ctx, loss_target)
```

```python
import functools
import math

import jax
import jax.numpy as jnp
from jax import lax
from jax.experimental import pallas as pl
from jax.experimental.pallas import tpu as pltpu

F32 = jnp.float32
BF16 = jnp.bfloat16
MESH = pl.DeviceIdType.MESH
N_DEV = 8

EPS = 1e-6
GRID_W = 64
HEADS = 4
HEAD_W = 128
MIX_W = HEADS * HEAD_W
GATE_RANK = 16
GATE_NORM = 16.0
GLA_CHUNK = 128
CONF_K = 31
SC_K = 3
Q_RANK = 384
KV_RANK = 128
ROPE = 64
ROPE_FREQS = 16
ROPE_BASE = 10000.0
ATT_SCALE = (HEAD_W + ROPE) ** -0.5
CONV_HALO = 16

ADAM_LR = 0.001
ADAM_B1 = 0.9
ADAM_B2 = 0.999
ADAM_EPS = 1e-08
ADAM_WD = 0.01
ADAM_STEP = 10

LANES = 128
VMEM_LIMIT = 56 * 2 ** 20
ROW_BLOCK_BYTES = 10 * 2 ** 20

Z_Q, Z_K, Z_V, Z_G, Z_LR, Z_A, Z_GATE, Z_BG, Z_CG, Z_H, Z_CQ, Z_CKV, Z_KR, Z_END = (
    0, 512, 1024, 1536, 2048, 2176, 2688, 3200, 3712, 4224, 4736, 5120, 5248, 5376)
IN_W = 5216


def _pcall(body, **kw):
    return pl.pallas_call(body, **kw)


def _params(sem=None):
    return pltpu.CompilerParams(dimension_semantics=sem, vmem_limit_bytes=VMEM_LIMIT)


def _pick(dim, cap, mult):
    d = (min(cap, dim) // mult) * mult
    while d >= mult:
        if dim % d == 0:
            return d
        d -= mult
    return dim


def _mm_call(name, a, b, out_shape, grid, a_spec, b_spec, o_spec, dims, k_axis=None, once_axis=None, out_dtype=F32,
             gather=()):
    a_blk = tuple(d for d in a_spec.block_shape if d is not None)
    o_blk = tuple(d for d in o_spec.block_shape if d is not None)
    if a.dtype == BF16:
        once_axis = None
    scratch = ([pltpu.VMEM((math.prod(o_blk[:-1]), o_blk[-1]), F32)] if k_axis is not None else []) + (
        [pltpu.VMEM(a_blk, BF16)] if once_axis is not None else [])
    nk = grid[k_axis] if k_axis is not None else 1
    ng = len(gather)

    def body(*refs):
        a_ref, b_ref, o_ref, scr = refs[0], refs[1], refs[2 + ng], refs[3 + 2 * ng:]
        if ng:
            exchange = _Gather(refs[2:2 + ng], refs[3 + ng:3 + 2 * ng], *scr[-3:])
            scr = scr[:-3]
            steps = [pl.program_id(ax) for ax in range(len(grid))]
            at_first = functools.reduce(jnp.logical_and, [s == 0 for s in steps])
            at_last = functools.reduce(jnp.logical_and, [s == g - 1 for s, g in zip(steps, grid)])
            pl.when(at_first)(exchange.start)
        product(a_ref, b_ref, o_ref, scr)
        if ng:
            pl.when(at_last)(exchange.finish)

    def product(a_ref, b_ref, o_ref, scr):
        if once_axis is not None:
            a_bf = scr[-1]

            @pl.when(pl.program_id(once_axis) == 0)
            def _():
                a_bf[...] = a_ref[...].astype(BF16)

            av = a_bf[...]
        else:
            av = a_ref[...].astype(BF16)
        bv = b_ref[...].astype(BF16)
        if bv.ndim == 3:
            bv = bv.reshape(-1, bv.shape[-1])
        prod = lax.dot_general(av, bv, dims, preferred_element_type=F32)
        if k_axis is None:
            o_ref[...] = prod.astype(o_ref.dtype).reshape(o_ref.shape)
        else:
            acc, k = scr[0], pl.program_id(k_axis)

            @pl.when(k == 0)
            def _():
                acc[...] = prod

            @pl.when(k != 0)
            def _():
                acc[...] += prod

            @pl.when(k == nk - 1)
            def _():
                o_ref[...] = acc[...].astype(o_ref.dtype).reshape(o_ref.shape)

    outs = _pcall(
        body, name=name, grid=grid, in_specs=[a_spec, b_spec] + [HBM] * ng, out_specs=[o_spec] + [HBM] * ng,
        out_shape=[jax.ShapeDtypeStruct(out_shape, out_dtype)]
        + [jax.ShapeDtypeStruct((N_DEV,) + g.shape, g.dtype) for g in gather],
        scratch_shapes=scratch + (_Gather.scratch(ng) if ng else []),
        compiler_params=_params(("arbitrary",) * len(grid)),
    )(a, b, *gather)
    return (outs[0], tuple(outs[1:])) if ng else outs[0]


NN = (((1,), (0,)), ((), ()))
NT = (((1,), (1,)), ((), ()))
TN = (((0,), (0,)), ((), ()))


def _matmul(a, b, mode, name, out_dtype=F32, gather=()):
    if mode == "nn":
        (M, K), (_, N) = a.shape, b.shape
    elif mode == "nt":
        (M, K), (N, _) = a.shape, b.shape
    else:
        (K, M), (_, N) = a.shape, b.shape
    if mode == "nn":
        tm, tn, tk = _pick(M, 1088, 16), _pick(N, 768, LANES), _pick(K, 2048, LANES)
    elif mode == "nt":
        tm, tn, tk = _pick(M, 1088, 16), _pick(N, 2048, LANES), _pick(K, 1024, LANES)
    else:
        tm, tn, tk = _pick(M, 2048, LANES), _pick(N, 768, LANES), _pick(K, 1088, 16)
    if mode == "nn":
        a_spec = pl.BlockSpec((tm, tk), lambda i, j, k: (i, k))
        b_spec = pl.BlockSpec((tk, tn), lambda i, j, k: (k, j))
    elif mode == "nt":
        a_spec = pl.BlockSpec((tm, tk), lambda i, j, k: (i, k))
        b_spec = pl.BlockSpec((tn, tk), lambda i, j, k: (j, k))
    else:
        a_spec = pl.BlockSpec((tk, tm), lambda i, j, k: (k, i))
        b_spec = pl.BlockSpec((tk, tn), lambda i, j, k: (k, j))
    return _mm_call(name, a, b, (M, N), (M // tm, N // tn, K // tk), a_spec, b_spec,
                    pl.BlockSpec((tm, tn), lambda i, j, k: (i, j)), {"nn": NN, "nt": NT, "tn": TN}[mode], k_axis=2,
                    out_dtype=out_dtype, gather=gather)


def _with_gathered(result, nxt):
    return result if nxt else (result, ())


def _no_grads(res, nxt):
    return jnp.zeros_like(res), tuple(jnp.zeros_like(t) for t in nxt)


@jax.custom_vjp
def mm(a, w, carrier, nxt):
    return _with_gathered(_matmul(a, w, "nn", "mm_fwd", gather=nxt), nxt)


def _mm_fwd(a, w, carrier, nxt):
    return mm(a, w, carrier, nxt), (a, w, nxt)


def _mm_bwd(res, cts):
    a, w, nxt = res
    dc = cts[0]
    no_w, no_nxt = _no_grads(w, nxt)
    return _matmul(dc, w, "nt", "mm_da", a.dtype), no_w, _matmul(a, dc, "tn", "mm_dw", BF16), no_nxt


mm.defvjp(_mm_fwd, _mm_bwd)


def _make_mm_cols(layer):
    def forward(a, G, nxt):
        (M, K), n = a.shape, G.shape[3]
        tm = _pick(M, 1088, 16)
        return _with_gathered(_mm_call(
            "mmc_fwd", a, G, (N_DEV, M, n), (M // tm, N_DEV),
            pl.BlockSpec((tm, K), lambda i, d: (i, 0)),
            pl.BlockSpec((None, None, K, n), lambda i, d: (d, layer, 0, 0)),
            pl.BlockSpec((None, tm, n), lambda i, d: (d, i, 0)), NN, once_axis=1, out_dtype=BF16, gather=nxt), nxt)

    def grad_a(do, G, dtype):
        (_, M, n), K = do.shape, G.shape[2]
        tm, tn = _pick(M, 1088, 16), _pick(K, 1024, LANES)
        return _mm_call("mmc_da", do, G, (M, K), (M // tm, K // tn, N_DEV),
                        pl.BlockSpec((None, tm, n), lambda i, j, d: (d, i, 0)),
                        pl.BlockSpec((None, None, tn, n), lambda i, j, d: (d, layer, j, 0)),
                        pl.BlockSpec((tm, tn), lambda i, j, d: (i, j)), NT, k_axis=2, out_dtype=dtype)

    def grad_w(a, do):
        (M, K), n = a.shape, do.shape[2]
        tm, tk = _pick(K, 2048, LANES), _pick(M, 1088, 16)
        return _mm_call("mmc_dw", a, do, (N_DEV, K, n), (N_DEV, K // tm, M // tk),
                        pl.BlockSpec((tk, tm), lambda d, i, k: (k, i)),
                        pl.BlockSpec((None, tk, n), lambda d, i, k: (d, k, 0)),
                        pl.BlockSpec((None, tm, n), lambda d, i, k: (d, i, 0)), TN, k_axis=2, out_dtype=BF16)

    @jax.custom_vjp
    def f(a, G, carrier, nxt):
        return forward(a, G, nxt)

    def f_fwd(a, G, carrier, nxt):
        return forward(a, G, nxt), (a, G, nxt)

    def f_bwd(res, cts):
        a, G, nxt = res
        no_g, no_nxt = _no_grads(G, nxt)
        return grad_a(cts[0], G, a.dtype), no_g, grad_w(a, cts[0]), no_nxt

    f.defvjp(f_fwd, f_bwd)
    return f


def _make_mm_rows(layer):
    def forward(a, G, nxt):
        M, (r, N) = a.shape[-2], G.shape[2:]
        tm, tn = _pick(M, 1088, 16), _pick(N, 1024, LANES)
        if a.ndim == 2:
            return _with_gathered(_mm_call(
                "mmr2_fwd", a, G, (M, N), (M // tm, N // tn),
                pl.BlockSpec((tm, N_DEV * r), lambda i, j: (i, 0)),
                pl.BlockSpec((N_DEV, None, r, tn), lambda i, j: (0, layer, 0, j)),
                pl.BlockSpec((tm, tn), lambda i, j: (i, j)), NN, gather=nxt), nxt)
        return _with_gathered(_mm_call(
            "mmr_fwd", a, G, (M, N), (M // tm, N // tn, N_DEV),
            pl.BlockSpec((None, tm, r), lambda i, j, d: (d, i, 0)),
            pl.BlockSpec((None, None, r, tn), lambda i, j, d: (d, layer, 0, j)),
            pl.BlockSpec((tm, tn), lambda i, j, d: (i, j)), NN, k_axis=2, gather=nxt), nxt)

    def grad_a(dc, G, like):
        (M, N), r = dc.shape, G.shape[2]
        tm = _pick(M, 1088, 16)
        if like.ndim == 2:
            return _mm_call("mmr2_da", dc, G, like.shape, (M // tm,),
                            pl.BlockSpec((tm, N), lambda i: (i, 0)),
                            pl.BlockSpec((N_DEV, None, r, N), lambda i: (0, layer, 0, 0)),
                            pl.BlockSpec((tm, N_DEV * r), lambda i: (i, 0)), NT, out_dtype=like.dtype)
        return _mm_call("mmr_da", dc, G, like.shape, (M // tm, N_DEV),
                        pl.BlockSpec((tm, N), lambda i, d: (i, 0)),
                        pl.BlockSpec((None, None, r, N), lambda i, d: (d, layer, 0, 0)),
                        pl.BlockSpec((None, tm, r), lambda i, d: (d, i, 0)), NT, once_axis=1, out_dtype=like.dtype)

    def grad_w(a, dc):
        (M, N), tk = dc.shape, _pick(dc.shape[0], 1088, 16)
        if a.ndim == 2:
            r, tn = a.shape[1] // N_DEV, _pick(N, 1024, LANES)
            return _mm_call("mmr2_dw", a, dc, (N_DEV, r, N), (N // tn, M // tk),
                            pl.BlockSpec((tk, N_DEV * r), lambda j, k: (k, 0)),
                            pl.BlockSpec((tk, tn), lambda j, k: (k, j)),
                            pl.BlockSpec((N_DEV, r, tn), lambda j, k: (0, 0, j)), TN, k_axis=1, out_dtype=BF16)
        r, tn = a.shape[2], _pick(N, 2048, LANES)
        return _mm_call("mmr_dw", a, dc, (N_DEV, r, N), (N_DEV, N // tn, M // tk),
                        pl.BlockSpec((None, tk, r), lambda d, j, k: (d, k, 0)),
                        pl.BlockSpec((tk, tn), lambda d, j, k: (k, j)),
                        pl.BlockSpec((None, r, tn), lambda d, j, k: (d, 0, j)), TN, k_axis=2, out_dtype=BF16)

    @jax.custom_vjp
    def f(a, G, carrier, nxt):
        return forward(a, G, nxt)

    def f_fwd(a, G, carrier, nxt):
        return forward(a, G, nxt), (a, G, nxt)

    def f_bwd(res, cts):
        a, G, nxt = res
        dc = cts[0].astype(BF16)
        no_g, no_nxt = _no_grads(G, nxt)
        return grad_a(dc, G, a), no_g, grad_w(a, dc), no_nxt

    f.defvjp(f_fwd, f_bwd)
    return f


@functools.partial(jax.custom_vjp, nondiff_argnums=(1,))
def _split_cols(z, bounds):
    return tuple(z[:, a:b] for a, b in zip(bounds[:-1], bounds[1:]))


def _split_cols_fwd(z, bounds):
    return _split_cols(z, bounds), None


def _split_cols_bwd(bounds, _, cts):
    return (jnp.concatenate(cts, axis=1),)


_split_cols.defvjp(_split_cols_fwd, _split_cols_bwd)


def _rowop(fn, name, n_ctx=0, tile=256, lane_block=None, out_dtypes=None):
    def geometry(rows):
        L, w0 = rows[0].shape
        nj = w0 // lane_block if lane_block else 1
        width = 3 * sum(lane_block or r.shape[1] for r in rows)
        cap = max(16, ROW_BLOCK_BYTES // (4 * width) // 16 * 16)
        tl = _pick(math.gcd(L, n_ctx) if n_ctx else L, min(tile, cap), 16)
        return L, tl, n_ctx // tl, nj

    def block_w(x):
        return lane_block or x.shape[1]

    def row_spec(tl, x):
        if lane_block and x.shape[1] != lane_block:
            return pl.BlockSpec((tl, lane_block), lambda i, j: (i, j))
        return pl.BlockSpec((tl, block_w(x)), lambda i, j: (i, 0))

    def param_spec(p, nct):
        s, r, w = p.shape
        if s == 1:
            return pl.BlockSpec((1, r, w), lambda i, j: (0, 0, 0))
        return pl.BlockSpec((1, r, w), lambda i, j: (jnp.where(i < nct, 0, 1), 0, 0))

    def forward(rows, consts, params):
        L, tl, nct, nj = geometry(rows)
        nr, nc, npar = len(rows), len(consts), len(params)
        outs = jax.eval_shape(
            lambda: fn(*[jnp.zeros((tl, block_w(r)), F32) for r in rows + consts],
                       *[jnp.zeros(p.shape[1:], F32) for p in params]))

        def body(*refs):
            ins = [r[...].astype(F32) for r in refs[:nr + nc]] + [r[0] for r in refs[nr + nc:nr + nc + npar]]
            for o_ref, o in zip(refs[nr + nc + npar:], fn(*ins)):
                o_ref[...] = o.astype(o_ref.dtype)

        return _pcall(
            body, name=name + "_fwd", grid=(L // tl, nj),
            in_specs=[row_spec(tl, r) for r in rows + consts] + [param_spec(p, nct) for p in params],
            out_specs=[pl.BlockSpec((tl, o.shape[1]), lambda i, j: (i, j)) for o in outs],
            out_shape=[jax.ShapeDtypeStruct((L, o.shape[1] * nj), dt)
                       for o, dt in zip(outs, out_dtypes or (F32,) * len(outs))],
            compiler_params=_params(("parallel", "parallel")),
        )(*rows, *consts, *params)

    def backward(rows, consts, params, cts):
        L, tl, nct, nj = geometry(rows)
        nr, nc, npar, no = len(rows), len(consts), len(params), len(cts)

        def body(*refs):
            i, j = pl.program_id(0), pl.program_id(1)
            rv = [r[...].astype(F32) for r in refs[:nr]]
            cv = [r[...] for r in refs[nr:nr + nc]]
            pv = [r[0] for r in refs[nr + nc:nr + nc + npar]]
            ct = tuple(r[...].astype(F32) for r in refs[nr + nc + npar:nr + nc + npar + no])
            out_refs = refs[nr + nc + npar + no:]
            _, vjp = jax.vjp(lambda *d: tuple(fn(*d[:nr], *cv, *d[nr:])), *rv, *pv)
            grads = vjp(ct)
            for ref, g in zip(out_refs[:nr], grads[:nr]):
                ref[...] = g.astype(ref.dtype)
            for ref, g, p in zip(out_refs[nr:], grads[nr:], params):
                first_row = (i == 0) if (p.shape[0] == 1 or nct == 0) else ((i == 0) | (i == nct))
                first = jnp.logical_and(first_row, j == 0)

                @pl.when(first)
                def _():
                    ref[0] = g

                @pl.when(jnp.logical_not(first))
                def _():
                    ref[0] += g

        outs = _pcall(
            body, name=name + "_bwd", grid=(L // tl, nj),
            in_specs=[row_spec(tl, r) for r in rows + consts] + [param_spec(p, nct) for p in params]
            + [pl.BlockSpec((tl, c.shape[1] // nj), lambda i, j: (i, j)) for c in cts],
            out_specs=[row_spec(tl, r) for r in rows] + [param_spec(p, nct) for p in params],
            out_shape=[jax.ShapeDtypeStruct(r.shape, r.dtype) for r in rows]
            + [jax.ShapeDtypeStruct(p.shape, F32) for p in params],
            compiler_params=_params(("arbitrary", "arbitrary")),
        )(*rows, *consts, *params, *cts)
        return tuple(outs[:nr]), tuple(outs[nr:])

    @jax.custom_vjp
    def op(rows, consts, params):
        return tuple(forward(rows, consts, params))

    def op_fwd(rows, consts, params):
        return tuple(forward(rows, consts, params)), (rows, consts, params)

    def op_bwd(res, cts):
        rows, consts, params = res
        d_rows, d_params = backward(rows, consts, params, tuple(cts))
        return d_rows, tuple(jnp.zeros_like(c) for c in consts), d_params

    op.defvjp(op_fwd, op_bwd)
    return op


def _sigmoid(x):
    return 1.0 / (1.0 + jnp.exp(-x))


def _silu(x):
    return x * _sigmoid(x)


def _log_sigmoid(x):
    return jnp.minimum(x, 0.0) - jnp.log(1.0 + jnp.exp(-jnp.abs(x)))


def _rms(x, g):
    return x * lax.rsqrt(jnp.mean(x * x, axis=-1, keepdims=True) + EPS) * g


def _f_norm_mod(x, g, sc, sh):
    return (_rms(x, g) * (1.0 + sc) + sh,)


def _f_resid_norm_mod(x, o, gate, g, sc, sh):
    x1 = x + gate * o
    return x1, _rms(x1, g) * (1.0 + sc) + sh


def _f_resid(x, o, gate):
    return (x + gate * o,)


def _f_swiglu(a1, a3):
    return (_silu(a1) * a3,)


def _f_gla_gate(lr, up_f, up_b, b_f, b_b):
    dot = functools.partial(jnp.dot, preferred_element_type=F32)
    return (_log_sigmoid(dot(lr, up_f) + b_f) / GATE_NORM, _log_sigmoid(dot(lr, up_b) + b_b) / GATE_NORM)


def _f_gla_finish(o_f, o_b, gate, g):
    return (_rms(o_f + o_b, g) * _silu(gate),)


def _f_glu(a, gate):
    return (a * _sigmoid(gate),)


def _f_ln_silu(u, dw_b, g, b):
    u = u + dw_b
    xc = u - jnp.mean(u, axis=-1, keepdims=True)
    y = xc * lax.rsqrt(jnp.mean(xc * xc, axis=-1, keepdims=True) + EPS)
    return (_silu(y * g + b),)


def _f_mul(a, b):
    return (a * b,)


def _f_rms(x, g):
    return (_rms(x, g),)


def _f_rope(t, cos, sin):
    w = t.shape[1]
    r = lax.broadcasted_iota(jnp.int32, (w, w), 0)
    c = lax.broadcasted_iota(jnp.int32, (w, w), 1)
    perm = (jnp.bitwise_xor(r, ROPE_FREQS) == c).astype(F32)
    partner = jnp.dot(t, perm, precision=lax.Precision.HIGHEST, preferred_element_type=F32)
    return (t * cos + partner * sin,)


def _f_silu(x):
    return (_silu(x),)


def _f_add_bias(x, b):
    return (x + b,)


def _f_mul_silu_grad(d, x):
    _, vjp = jax.vjp(_silu, x)
    return (vjp(d)[0],)


def _conv_geometry(u, n_ctx):
    L, C = u.shape
    tl = _pick(math.gcd(L, n_ctx), 256, 8)
    return L, C, tl, n_ctx // tl, L // tl


def _conv_specs(tl, nt):
    prev = pl.BlockSpec((tl, LANES), lambda c, i: (jnp.maximum(i - 1, 0), c))
    cur = pl.BlockSpec((tl, LANES), lambda c, i: (i, c))
    nxt = pl.BlockSpec((tl, LANES), lambda c, i: (jnp.minimum(i + 1, nt - 1), c))
    return prev, cur, nxt


def _conv_window(prev_ref, cur_ref, next_ref, tl, nct, nt):
    i = pl.program_id(1)
    has_prev = jnp.logical_and(i != 0, i != nct)
    has_next = jnp.logical_and(i != nct - 1, i != nt - 1)
    prev = jnp.where(has_prev, prev_ref[tl - CONV_HALO:tl, :], 0.0)
    nxt = jnp.where(has_next, next_ref[0:CONV_HALO, :], 0.0)
    return jnp.concatenate([prev, cur_ref[...], nxt], axis=0)


def _shifter(window, tl):
    n, rolled = window.shape[0], {0: window}

    def rows(off):
        within = off % 8
        if within not in rolled:
            rolled[within] = pltpu.roll(window, n - within, 0)
        return rolled[within][off - within:off - within + tl]

    return rows


def _conv_apply(u, w, n_ctx, flip, name):
    L, C, tl, nct, nt = _conv_geometry(u, n_ctx)
    K = w.shape[0]
    pad = (K - 1) // 2
    prev, cur, nxt = _conv_specs(tl, nt)

    def body(p_ref, c_ref, n_ref, w_ref, o_ref):
        rows = _shifter(_conv_window(p_ref, c_ref, n_ref, tl, nct, nt), tl)
        acc = jnp.zeros((tl, LANES), F32)
        for k in range(K):
            kk = K - 1 - k if flip else k
            acc = acc + rows(CONV_HALO - pad + k) * w_ref[kk:kk + 1, :]
        o_ref[...] = acc

    return _pcall(
        body, name=name, grid=(C // LANES, nt),
        in_specs=[prev, cur, nxt, pl.BlockSpec((K, LANES), lambda c, i: (0, c))],
        out_specs=cur, out_shape=jax.ShapeDtypeStruct((L, C), F32),
        compiler_params=_params(("parallel", "parallel")),
    )(u, u, u, w)


def _conv_dw(u, dy, K, n_ctx, name):
    L, C, tl, nct, nt = _conv_geometry(u, n_ctx)
    pad = (K - 1) // 2
    prev, cur, nxt = _conv_specs(tl, nt)

    def body(p_ref, c_ref, n_ref, dy_ref, dw_ref):
        i = pl.program_id(1)

        @pl.when(i == 0)
        def _():
            dw_ref[...] = jnp.zeros_like(dw_ref)

        rows = _shifter(_conv_window(p_ref, c_ref, n_ref, tl, nct, nt), tl)
        dy_t = dy_ref[...]
        for k in range(K):
            dw_ref[k:k + 1, :] += jnp.sum(rows(CONV_HALO - pad + k) * dy_t, axis=0, keepdims=True)

    return _pcall(
        body, name=name, grid=(C // LANES, nt),
        in_specs=[prev, cur, nxt, cur],
        out_specs=pl.BlockSpec((K, LANES), lambda c, i: (0, c)),
        out_shape=jax.ShapeDtypeStruct((K, C), F32),
        compiler_params=_params(("parallel", "arbitrary")),
    )(u, u, u, dy)


def _make_conv(n_ctx):
    @jax.custom_vjp
    def conv(u, w):
        return _conv_apply(u, w, n_ctx, False, "conv_fwd")

    def conv_fwd(u, w):
        return _conv_apply(u, w, n_ctx, False, "conv_fwd"), (u, w)

    def conv_bwd(res, dy):
        u, w = res
        return _conv_apply(dy, w, n_ctx, True, "conv_du"), _conv_dw(u, dy, w.shape[0], n_ctx, "conv_dw")

    conv.defvjp(conv_fwd, conv_bwd)
    return conv


def _gla_chunk(q, k, v, g, st, reverse):
    C = q.shape[0]
    r = lax.broadcasted_iota(jnp.int32, (C, C), 0)
    c = lax.broadcasted_iota(jnp.int32, (C, C), 1)
    seen = (r <= c) if reverse else (r >= c)
    dot = functools.partial(lax.dot_general, preferred_element_type=F32, precision=lax.Precision.DEFAULT)
    bcum = lax.dot_general(seen.astype(F32), g, (((1,), (0,)), ((), ())), preferred_element_type=F32,
                           precision=lax.Precision.HIGHEST)
    total = jnp.sum(g, axis=0, keepdims=True)
    a = q * (HEAD_W ** -0.5) * jnp.exp(bcum)
    scores = jnp.where(seen, dot(a, k * jnp.exp(-bcum), (((1,), (1,)), ((), ()))), 0.0)
    o = dot(a, st, (((1,), (1,)), ((), ()))) + dot(scores, v, (((1,), (0,)), ((), ())))
    st_new = st * jnp.exp(total) + dot(v, k * jnp.exp(total - bcum), (((0,), (0,)), ((), ())))
    return o, st_new


def _gla_order(t, nc, ncc, reverse):
    if not reverse:
        return t
    return jnp.where(t < ncc, ncc - 1 - t, ncc + nc - 1 - t)


def _gla_fwd_call(q, k, v, g, n_ctx, reverse):
    L = q.shape[0]
    C = GLA_CHUNK
    nc, ncc = L // C, n_ctx // C
    spec = pl.BlockSpec((C, MIX_W), lambda t: (_gla_order(t, nc, ncc, reverse), 0))

    def body(q_ref, k_ref, v_ref, g_ref, o_ref, s_ref, st):
        @pl.when(pl.program_id(0) == 0)
        def _():
            st[...] = jnp.zeros_like(st)

        for h in range(HEADS):
            hs = slice(h * HEAD_W, (h + 1) * HEAD_W)
            s_ref[h, 0] = st[h]
            o, st_new = _gla_chunk(q_ref[:, hs], k_ref[:, hs], v_ref[:, hs], g_ref[:, hs], st[h], reverse)
            o_ref[:, hs] = o
            st[h] = st_new

    return _pcall(
        body, name="gla_fwd", grid=(nc,), in_specs=[spec] * 4,
        out_specs=[spec, pl.BlockSpec((HEADS, 1, HEAD_W, HEAD_W), lambda t: (0, t, 0, 0))],
        out_shape=[jax.ShapeDtypeStruct((L, MIX_W), F32), jax.ShapeDtypeStruct((HEADS, nc, HEAD_W, HEAD_W), F32)],
        scratch_shapes=[pltpu.VMEM((HEADS, HEAD_W, HEAD_W), F32)],
        compiler_params=_params(("arbitrary",)),
    )(q, k, v, g)


def _gla_bwd_call(q, k, v, g, states, do, n_ctx, reverse):
    L = q.shape[0]
    C = GLA_CHUNK
    nc, ncc = L // C, n_ctx // C
    spec = pl.BlockSpec((C, MIX_W), lambda t: (_gla_order(nc - 1 - t, nc, ncc, reverse), 0))

    def body(q_ref, k_ref, v_ref, g_ref, s_ref, do_ref, dq_ref, dk_ref, dv_ref, dg_ref, dst):
        @pl.when(pl.program_id(0) == 0)
        def _():
            dst[...] = jnp.zeros_like(dst)

        for h in range(HEADS):
            hs = slice(h * HEAD_W, (h + 1) * HEAD_W)
            _, vjp = jax.vjp(functools.partial(_gla_chunk, reverse=reverse),
                             q_ref[:, hs], k_ref[:, hs], v_ref[:, hs], g_ref[:, hs], s_ref[h, 0])
            dq, dk, dv, dg, dst_prev = vjp((do_ref[:, hs], dst[h]))
            dq_ref[:, hs] = dq
            dk_ref[:, hs] = dk
            dv_ref[:, hs] = dv
            dg_ref[:, hs] = dg
            dst[h] = dst_prev

    return _pcall(
        body, name="gla_bwd", grid=(nc,),
        in_specs=[spec] * 4 + [pl.BlockSpec((HEADS, 1, HEAD_W, HEAD_W), lambda t: (0, nc - 1 - t, 0, 0)), spec],
        out_specs=[spec] * 4, out_shape=[jax.ShapeDtypeStruct((L, MIX_W), F32)] * 4,
        scratch_shapes=[pltpu.VMEM((HEADS, HEAD_W, HEAD_W), F32)],
        compiler_params=_params(("arbitrary",)),
    )(q, k, v, g, states, do)


def _make_gla(n_ctx, reverse):
    @jax.custom_vjp
    def gla(q, k, v, g):
        return _gla_fwd_call(q, k, v, g, n_ctx, reverse)[0]

    def gla_fwd(q, k, v, g):
        o, states = _gla_fwd_call(q, k, v, g, n_ctx, reverse)
        return o, (q, k, v, g, states)

    def gla_bwd(res, do):
        q, k, v, g, states = res
        return tuple(_gla_bwd_call(q, k, v, g, states, do, n_ctx, reverse))

    gla.defvjp(gla_fwd, gla_bwd)
    return gla


def _att_scaled(q_ref):
    return (q_ref[...] * ATT_SCALE).astype(BF16)


def _att_probs(qn, qr, kn, kr, i, nct, n_ctx):
    nt_dims = (((1,), (1,)), ((), ()))
    s = lax.dot_general(qn, kn, nt_dims, preferred_element_type=F32)
    s = s + lax.dot_general(qr, kr, nt_dims, preferred_element_type=F32)
    col = lax.broadcasted_iota(jnp.int32, (1, s.shape[1]), 1)
    s = s + jnp.where(col < jnp.where(i < nct, n_ctx, s.shape[1]), 0.0, -1e30)
    p = jnp.exp(s - jnp.max(s, axis=-1, keepdims=True))
    return p * (1.0 / jnp.sum(p, axis=-1, keepdims=True))


def _att_geometry(qn, n_ctx):
    L = qn.shape[0]
    tq = _pick(math.gcd(L, n_ctx), 256, 8)
    q_spec = pl.BlockSpec((tq, HEAD_W), lambda h, i: (i, h))
    k_spec = pl.BlockSpec((L, HEAD_W), lambda h, i: (0, h))
    kr_spec = pl.BlockSpec((L, HEAD_W), lambda h, i: (0, 0))
    return L, tq, n_ctx // tq, q_spec, k_spec, kr_spec


def _att_fwd_call(qn, qr, kn, kr, v, n_ctx):
    L, tq, nct, q_spec, k_spec, kr_spec = _att_geometry(qn, n_ctx)

    def body(qn_ref, qr_ref, kn_ref, kr_ref, v_ref, o_ref):
        p = _att_probs(_att_scaled(qn_ref), _att_scaled(qr_ref), kn_ref[...].astype(BF16),
                       kr_ref[...].astype(BF16), pl.program_id(1), nct, n_ctx)
        o_ref[...] = jnp.dot(p.astype(BF16), v_ref[...].astype(BF16), preferred_element_type=F32).astype(BF16)

    return _pcall(
        body, name="att_fwd", grid=(HEADS, L // tq),
        in_specs=[q_spec, q_spec, k_spec, kr_spec, k_spec], out_specs=q_spec,
        out_shape=jax.ShapeDtypeStruct((L, MIX_W), BF16),
        compiler_params=_params(("parallel", "parallel")),
    )(qn, qr, kn, kr, v)


def _att_bwd_call(qn, qr, kn, kr, v, do, n_ctx, payload=()):
    L, tq, nct, q_spec, k_spec, kr_spec = _att_geometry(qn, n_ctx)
    tn_dims = (((0,), (0,)), ((), ()))
    nq, npay = L // tq, len(payload)

    def body(*refs):
        (qn_ref, qr_ref, kn_ref, kr_ref, v_ref, do_ref), refs = refs[:6], refs[6:]
        p_refs, (dqn_ref, dqr_ref, dkn_ref, dkr_ref, dv_ref) = refs[:npay], refs[npay:npay + 5]
        h, i = pl.program_id(0), pl.program_id(1)
        if npay:
            exchange = _ChipExchange(p_refs, refs[npay + 5:2 * npay + 5], *refs[2 * npay + 5:])
            pl.when(jnp.logical_and(h == 0, i == 0))(exchange.start)
        qn, qr = _att_scaled(qn_ref), _att_scaled(qr_ref)
        kn, kr, vv = kn_ref[...].astype(BF16), kr_ref[...].astype(BF16), v_ref[...].astype(BF16)
        do = do_ref[...].astype(BF16)
        p = _att_probs(qn, qr, kn, kr, i, nct, n_ctx)
        dp = lax.dot_general(do, vv, (((1,), (1,)), ((), ())), preferred_element_type=F32)
        ds = (p * (dp - jnp.sum(p * dp, axis=-1, keepdims=True))).astype(BF16)
        dqn_ref[...] = jnp.dot(ds, kn, preferred_element_type=F32) * ATT_SCALE
        dqr_ref[...] = jnp.dot(ds, kr, preferred_element_type=F32) * ATT_SCALE

        @pl.when(i == 0)
        def _():
            dkn_ref[...] = jnp.zeros_like(dkn_ref)
            dv_ref[...] = jnp.zeros_like(dv_ref)

        @pl.when(jnp.logical_and(i == 0, h == 0))
        def _():
            dkr_ref[...] = jnp.zeros_like(dkr_ref)

        dkn_ref[...] += lax.dot_general(ds, qn, tn_dims, preferred_element_type=F32)
        dkr_ref[...] += lax.dot_general(ds, qr, tn_dims, preferred_element_type=F32)
        dv_ref[...] += lax.dot_general(p.astype(BF16), do, tn_dims, preferred_element_type=F32)
        if npay:
            pl.when(jnp.logical_and(h == HEADS - 1, i == nq - 1))(exchange.finish)

    outs = _pcall(
        body, name="att_bwd", grid=(HEADS, nq),
        in_specs=[q_spec, q_spec, k_spec, kr_spec, k_spec, q_spec] + [HBM] * npay,
        out_specs=[q_spec, q_spec, k_spec, kr_spec, k_spec] + [HBM] * npay,
        out_shape=[jax.ShapeDtypeStruct((L, MIX_W), F32)] * 3
        + [jax.ShapeDtypeStruct((L, HEAD_W), F32), jax.ShapeDtypeStruct((L, MIX_W), F32)]
        + [jax.ShapeDtypeStruct((3,) + p.shape[1:], p.dtype) for p in payload],
        scratch_shapes=_ChipExchange.scratch(npay) if npay else [],
        compiler_params=_params(("arbitrary", "arbitrary")),
    )(qn, qr, kn, kr, v, do, *payload)
    return tuple(outs[:5]), tuple(outs[5:])


def _make_attention(n_ctx):
    @jax.custom_vjp
    def att(qn, qr, kn, kr, v, slots):
        return _att_fwd_call(qn, qr, kn, kr, v, n_ctx), tuple(lax.empty((4,) + s.shape[1:], s.dtype) for s in slots)

    def att_fwd(qn, qr, kn, kr, v, slots):
        return att(qn, qr, kn, kr, v, slots), (qn, qr, kn, kr, v)

    def att_bwd(res, cts):
        grads, arrived = _att_bwd_call(*res, cts[0], n_ctx, tuple(cts[1]))
        return (*grads, arrived)

    att.defvjp(att_fwd, att_bwd)
    return att


def _loss_call(x, g, target):
    L, D = x.shape
    tl = _pick(L, 256, 8)

    def f(xv, gv, tv):
        err = _rms(xv, gv) - tv
        return 0.5 * jnp.sum(err * err, axis=0, keepdims=True) / D

    def body(x_ref, g_ref, t_ref, loss_ref, dx_ref, dg_ref):
        i = pl.program_id(0)
        loss, vjp = jax.vjp(lambda xv, gv: f(xv, gv, t_ref[...]), x_ref[...], g_ref[...])
        dx, dg = vjp(jnp.ones_like(loss))
        dx_ref[...] = dx

        @pl.when(i == 0)
        def _():
            loss_ref[...] = loss
            dg_ref[...] = dg

        @pl.when(i != 0)
        def _():
            loss_ref[...] += loss
            dg_ref[...] += dg

    row = pl.BlockSpec((tl, D), lambda i: (i, 0))
    one = pl.BlockSpec((1, D), lambda i: (0, 0))
    return _pcall(
        body, name="loss", grid=(L // tl,), in_specs=[row, one, row], out_specs=[one, row, one],
        out_shape=[jax.ShapeDtypeStruct((1, D), F32), jax.ShapeDtypeStruct((L, D), F32),
                   jax.ShapeDtypeStruct((1, D), F32)],
        compiler_params=_params(("arbitrary",)),
    )(x, g, target)


def _sum_leading(x, name):
    n, R, W = x.shape
    tr = _pick(R, 512, 8)

    def body(x_ref, o_ref):
        acc = x_ref[0]
        for d in range(1, n):
            acc = acc + x_ref[d]
        o_ref[...] = acc

    return _pcall(
        body, name=name, grid=(R // tr,), in_specs=[pl.BlockSpec((n, tr, W), lambda i: (0, i, 0))],
        out_specs=pl.BlockSpec((tr, W), lambda i: (i, 0)), out_shape=jax.ShapeDtypeStruct((R, W), F32),
        compiler_params=_params(("parallel",)),
    )(x)


def _adamw(w, g, m, v):
    shape = w.shape
    W = shape[-1]
    as2d = lambda t: t.reshape(-1, W)
    R = as2d(w).shape[0]
    tr = _pick(R, max(8, (2 ** 17 // W) // 8 * 8), 8)

    def body(w_ref, g_ref, m_ref, v_ref, d_ref, nm_ref, nv_ref):
        gv = g_ref[...]
        m_new = ADAM_B1 * m_ref[...] + (1.0 - ADAM_B1) * gv
        v_new = ADAM_B2 * v_ref[...] + (1.0 - ADAM_B2) * (gv * gv)
        m_hat = m_new / (1.0 - ADAM_B1 ** ADAM_STEP)
        v_hat = v_new / (1.0 - ADAM_B2 ** ADAM_STEP)
        d_ref[...] = -ADAM_LR * (m_hat / (jnp.sqrt(v_hat) + ADAM_EPS) + ADAM_WD * w_ref[...])
        nm_ref[...] = m_new
        nv_ref[...] = v_new

    spec = pl.BlockSpec((tr, W), lambda i: (i, 0))
    outs = _pcall(
        body, name="adamw", grid=(R // tr,), in_specs=[spec] * 4, out_specs=[spec] * 3,
        out_shape=[jax.ShapeDtypeStruct((R, W), F32)] * 3, compiler_params=_params(("parallel",)),
    )(as2d(w), as2d(g), as2d(m), as2d(v))
    return tuple(o.reshape(shape) for o in outs)


HBM = pl.BlockSpec(memory_space=pltpu.HBM)


def _place():
    x, y, c = lax.axis_index("x"), lax.axis_index("y"), lax.axis_index("c")
    return x, y, c, [(1 - x, y), (x, 1 - y), (1 - x, 1 - y)]


class _Gather:
    def __init__(self, x_refs, out_refs, send_sems, recv_sems, local_sems):
        self.x_refs, self.out_refs, self.n = x_refs, out_refs, len(x_refs)
        self.sems = send_sems, recv_sems, local_sems
        self.x, self.y, self.c, self.chips = _place()
        self.me, self.sibling = (self.x, self.y, self.c), (self.x, self.y, 1 - self.c)

    @staticmethod
    def scratch(n):
        return [pltpu.SemaphoreType.DMA((7 * n,)), pltpu.SemaphoreType.DMA((7 * n,)), pltpu.SemaphoreType.DMA((n,))]

    def slot(self, a, px, py, pc):
        return self.out_refs[a].at[4 * px + 2 * py + pc]

    def copy(self, a, k, blk, to, src=None):
        return pltpu.make_async_remote_copy(
            src_ref=self.slot(a, *blk) if src is None else src, dst_ref=self.slot(a, *blk),
            send_sem=self.sems[0].at[7 * a + k], recv_sem=self.sems[1].at[7 * a + k], device_id=to,
            device_id_type=MESH)

    def own(self):
        first = []
        for a in range(self.n):
            first.append(self.copy(a, 0, self.me, self.sibling, src=self.x_refs[a]))
            first += [self.copy(a, 1 + j, self.me, (*chip, self.c), src=self.x_refs[a])
                      for j, chip in enumerate(self.chips)]
        return [pltpu.make_async_copy(self.x_refs[a], self.slot(a, *self.me), self.sems[2].at[a])
                for a in range(self.n)], first

    def start(self):
        mine, first = self.own()
        for cp in mine + first:
            cp.start()

    def finish(self):
        c, chips = self.c, self.chips
        mine, first = self.own()
        passed = []
        for j, chip in enumerate(chips):
            for a in range(self.n):
                self.copy(a, 1 + j, (*chip, c), self.me).wait_recv()
                passed.append(self.copy(a, 4 + j, (*chip, c), self.sibling))
                passed[-1].start()
        for a in range(self.n):
            self.copy(a, 0, self.sibling, self.me).wait_recv()
        for j, chip in enumerate(chips):
            for a in range(self.n):
                self.copy(a, 4 + j, (*chip, 1 - c), self.me).wait_recv()
        for cp in first + passed:
            cp.wait_send()
        for cp in mine:
            cp.wait()


def _all_gather(blocks, name):
    n = len(blocks)

    def body(*refs):
        g = _Gather(refs[:n], refs[n:2 * n], *refs[2 * n:])
        g.start()
        g.finish()

    return _pcall(
        body, name=name, out_shape=[jax.ShapeDtypeStruct((N_DEV,) + b.shape, b.dtype) for b in blocks],
        in_specs=[HBM] * n, out_specs=[HBM] * n, scratch_shapes=_Gather.scratch(n),
    )(*blocks)


def _send_to_sibling(gs, name):
    n = len(gs)

    def body(*refs):
        g_refs, out_refs, (send_sems, recv_sems) = refs[:n], refs[n:2 * n], refs[2 * n:]
        x, y, c, _ = _place()
        copies = [pltpu.make_async_remote_copy(
            src_ref=g_refs[a].at[2 * q + 1 - c], dst_ref=out_refs[a].at[q], send_sem=send_sems.at[4 * a + q],
            recv_sem=recv_sems.at[4 * a + q], device_id=(x, y, 1 - c), device_id_type=MESH)
            for a in range(n) for q in range(4)]
        for cp in copies:
            cp.start()
        for cp in copies:
            cp.wait()

    return _pcall(
        body, name=name, out_shape=[jax.ShapeDtypeStruct((4,) + g.shape[1:], g.dtype) for g in gs],
        in_specs=[HBM] * n, out_specs=[HBM] * n,
        scratch_shapes=[pltpu.SemaphoreType.DMA((4 * n,)), pltpu.SemaphoreType.DMA((4 * n,))],
    )(*gs)


class _ChipExchange:
    def __init__(self, p_refs, out_refs, send_sems, recv_sems):
        self.p_refs, self.out_refs, self.sems = p_refs, out_refs, (send_sems, recv_sems)

    @staticmethod
    def scratch(n):
        return [pltpu.SemaphoreType.DMA((3 * n,)), pltpu.SemaphoreType.DMA((3 * n,))]

    def copies(self):
        x, y, c, chips = _place()
        return [pltpu.make_async_remote_copy(
            src_ref=self.p_refs[a].at[2 * cx + cy], dst_ref=self.out_refs[a].at[j], send_sem=self.sems[0].at[3 * a + j],
            recv_sem=self.sems[1].at[3 * a + j], device_id=(cx, cy, c), device_id_type=MESH)
            for a in range(len(self.p_refs)) for j, (cx, cy) in enumerate(chips)]

    def start(self):
        for cp in self.copies():
            cp.start()

    def finish(self):
        for cp in self.copies():
            cp.wait()


def _send_to_chips(ps, name):
    n = len(ps)

    def body(*refs):
        exchange = _ChipExchange(refs[:n], refs[n:2 * n], *refs[2 * n:])
        exchange.start()
        exchange.finish()

    return _pcall(
        body, name=name, out_shape=[jax.ShapeDtypeStruct((3,) + p.shape[1:], p.dtype) for p in ps],
        in_specs=[HBM] * n, out_specs=[HBM] * n, scratch_shapes=_ChipExchange.scratch(n),
    )(*ps)


def _add_rows(R, W):
    return _pick(R, max(16, 2 ** 19 // W // 16 * 16), 16)


def _add_sibling(g, recv, core):
    _, R, W = g.shape
    tr = _add_rows(R, W)

    def body(core_ref, g_ref, r_ref, o_ref):
        o_ref[...] = (g_ref[...].astype(F32) + r_ref[...].astype(F32)).astype(BF16)

    return _pcall(
        body, name="rs_add_sibling",
        grid_spec=pltpu.PrefetchScalarGridSpec(
            num_scalar_prefetch=1, grid=(4, R // tr),
            in_specs=[pl.BlockSpec((None, tr, W), lambda q, i, core_ref: (2 * q + core_ref[0], i, 0)),
                      pl.BlockSpec((None, tr, W), lambda q, i, core_ref: (q, i, 0))],
            out_specs=pl.BlockSpec((None, tr, W), lambda q, i, core_ref: (q, i, 0))),
        out_shape=jax.ShapeDtypeStruct((4, R, W), BF16), compiler_params=_params(("parallel", "parallel")),
    )(core, g, recv)


def _add_chips(p, recv, chip):
    _, R, W = p.shape
    tr = _add_rows(R, W)

    def body(chip_ref, p_ref, r_ref, o_ref):
        up = lambda t: t.astype(F32)
        o_ref[...] = ((up(p_ref[...]) + up(r_ref[0])) + up(r_ref[1])) + up(r_ref[2])

    return _pcall(
        body, name="rs_add_chips",
        grid_spec=pltpu.PrefetchScalarGridSpec(
            num_scalar_prefetch=1, grid=(R // tr,),
            in_specs=[pl.BlockSpec((None, tr, W), lambda i, chip_ref: (chip_ref[0], i, 0)),
                      pl.BlockSpec((3, tr, W), lambda i, chip_ref: (0, i, 0))],
            out_specs=pl.BlockSpec((tr, W), lambda i, chip_ref: (i, 0))),
        out_shape=jax.ShapeDtypeStruct((R, W), F32), compiler_params=_params(("parallel",)),
    )(chip, p, recv)


def _chip_sums(gs):
    core = jnp.reshape(lax.axis_index("c"), (1,)).astype(jnp.int32)
    return [_add_sibling(g, r, core) for g, r in zip(gs, _send_to_sibling(gs, "rs_sibling"))]


def _sum_chips(chip_sums, from_chips):
    chip = jnp.reshape(2 * lax.axis_index("x") + lax.axis_index("y"), (1,)).astype(jnp.int32)
    return [_add_chips(p, r, chip) for p, r in zip(chip_sums, from_chips)]


def _rope_tables(seq, n_ctx):
    rows = seq // GRID_W
    row = jnp.repeat(jnp.arange(rows, dtype=F32), GRID_W)
    col = jnp.tile(jnp.arange(GRID_W, dtype=F32), rows)
    inv = ROPE_BASE ** (-jnp.arange(ROPE_FREQS, dtype=F32) * 2.0 / (ROPE // 2))
    ang_r, ang_c = row[:, None] * inv, col[:, None] * inv
    one, zero = jnp.ones((seq, ROPE), F32), jnp.zeros((seq, ROPE), F32)
    cos = jnp.concatenate([jnp.cos(ang_r), jnp.cos(ang_r), jnp.cos(ang_c), jnp.cos(ang_c), one], axis=1)
    sin = jnp.concatenate([-jnp.sin(ang_r), jnp.sin(ang_r), -jnp.sin(ang_c), jnp.sin(ang_c), zero], axis=1)
    cos = jnp.concatenate([jnp.ones((n_ctx, LANES), F32), cos], axis=0)
    sin = jnp.concatenate([jnp.zeros((n_ctx, LANES), F32), sin], axis=0)
    return cos, sin


def _shared(v):
    return v.reshape((1, 1, -1)) if v.ndim == 1 else v.reshape((1,) + v.shape)


Z_BOUNDS = (Z_Q, Z_K, Z_V, Z_G, Z_LR, Z_A, Z_GATE, Z_BG, Z_CG, Z_H, Z_CQ, Z_CKV, Z_KR, Z_END)
PLAIN = ("w_in", "w_qn", "w_qr", "w_kn", "w_v")
COLS = ("ffn_w1", "ffn_w3")
ROWS = ("w_out", "ffn_w2")


def _layer(t, p, car, mod_l, mod_c, slots, G, nxt, n_ctx, tables):
    L, D = t.shape
    seg = lambda i: jnp.stack([mod_c[i * D:(i + 1) * D], mod_l[i * D:(i + 1) * D]]).reshape(2, 1, D)
    sh1, sc1, g1, sh2, sc2, g2 = (seg(i) for i in range(6))
    plain = _plain_weights(G, D)
    dense = lambda a, n: mm(a, plain[n], car[n], ())[0]
    take = lambda *names: tuple(nxt[n] for n in names) if nxt else ()

    (h,) = _rowop(_f_norm_mod, "norm_mod", n_ctx, out_dtypes=(BF16,))((t,), (), (_shared(p["norm1_g"]), sc1, sh1))
    z, got_in = mm(h, plain["w_in"], car["w_in"], take("w_in"))
    q, k, v, gate, lr, conf_a, conf_gate, sc_b, sc_c, sc_h, cq, ckv, kr = _split_cols(z, Z_BOUNDS)

    up = p["gla_fg_up"]
    up_f = jnp.pad(up[0], ((0, LANES - GATE_RANK), (0, 0)))
    up_b = jnp.pad(up[1], ((GATE_RANK, LANES - 2 * GATE_RANK), (0, 0)))
    logd_f, logd_b = _rowop(_f_gla_gate, "gla_gate")(
        (lr,), (), (_shared(up_f), _shared(up_b), _shared(p["gla_fg_b"][0]), _shared(p["gla_fg_b"][1])))
    o_f = _make_gla(n_ctx, False)(q, k, v, logd_f)
    o_b = _make_gla(n_ctx, True)(q, k, v, logd_b)
    (gla,) = _rowop(_f_gla_finish, "gla_finish", lane_block=HEAD_W, out_dtypes=(BF16,))(
        (o_f, o_b, gate), (), (_shared(p["gla_onorm_g"]),))

    conv = _make_conv(n_ctx)
    (u,) = _rowop(_f_glu, "glu")((conf_a, conf_gate), (), ())
    (conf,) = _rowop(_f_ln_silu, "ln_silu", out_dtypes=(BF16,))(
        (conv(u, p["conf_dw"]),), (), (_shared(p["conf_dw_b"]), _shared(p["conf_ln_g"]), _shared(p["conf_ln_b"])))

    (ch,) = _rowop(_f_mul, "mul")((sc_c, sc_h), (), ())
    (sconv,) = _rowop(_f_mul, "mul_out", out_dtypes=(BF16,))((sc_b, conv(ch, p["sc_dw"])), (), ())

    (cq,) = _rowop(_f_rms, "rms")((cq,), (), (_shared(p["mla_q_norm_g"]),))
    (ckv,) = _rowop(_f_rms, "rms")((ckv,), (), (_shared(p["mla_kv_norm_g"]),))
    rope = _rowop(_f_rope, "rope", lane_block=LANES)
    (qr,) = rope((dense(cq, "w_qr"),), tables, ())
    (kr,) = rope((kr,), tables, ())
    mla, tokens = _make_attention(n_ctx)(dense(cq, "w_qn"), qr, dense(ckv, "w_kn"), kr, dense(ckv, "w_v"), slots)

    o, got_out = _make_mm_rows(0)(jnp.concatenate([gla, conf, sconv, mla], axis=1), G["w_out"], car["w_out"],
                                  take("w_out", "mla_w_uq", "mla_w_ukv"))
    t1, h2 = _rowop(_f_resid_norm_mod, "resid_norm_mod", n_ctx, out_dtypes=(F32, BF16))(
        (t, o), (), (g1, _shared(p["norm2_g"]), sc2, sh2))
    a1, got_w1 = _make_mm_cols(0)(h2, G["ffn_w1"], car["ffn_w1"], take("ffn_w1"))
    a3, got_w3 = _make_mm_cols(0)(h2, G["ffn_w3"], car["ffn_w3"], take("ffn_w3"))
    n_ff = a1.shape[2]
    (act,) = _rowop(_f_swiglu, "swiglu", tile=1024, out_dtypes=(BF16,))(
        (a1.reshape(N_DEV * L, n_ff), a3.reshape(N_DEV * L, n_ff)), (), ())
    f, got_w2 = _make_mm_rows(0)(act.reshape(N_DEV, L, n_ff), G["ffn_w2"], car["ffn_w2"], take("ffn_w2"))
    (t2,) = _rowop(_f_resid, "resid", n_ctx)((t1, f), (), (g2,))
    got = got_in + got_out + got_w1 + got_w3 + got_w2
    gathered = dict(zip(("w_in", "w_out", "mla_w_uq", "mla_w_ukv", "ffn_w1", "ffn_w3", "ffn_w2"), got)) if nxt else None
    return (t2, tokens), gathered


def _plain_weights(G, D):
    full = lambda n: jnp.concatenate([G[n][d, 0] for d in range(N_DEV)], axis=1)
    w_in = full("w_in")
    w_in = jnp.concatenate([w_in[:, :Z_LR + 2 * GATE_RANK], jnp.zeros((D, Z_A - Z_LR - 2 * GATE_RANK), BF16),
                            w_in[:, Z_LR + 2 * GATE_RANK:], jnp.zeros((D, Z_END - Z_KR - ROPE), BF16)], axis=1)
    w_uq = full("mla_w_uq").reshape(Q_RANK, HEADS, HEAD_W + ROPE)
    w_ukv = full("mla_w_ukv").reshape(KV_RANK, HEADS, 2 * HEAD_W)
    return {"w_in": w_in,
            "w_qn": w_uq[:, :, :HEAD_W].reshape(Q_RANK, MIX_W),
            "w_qr": jnp.pad(w_uq[:, :, HEAD_W:], ((0, 0), (0, 0), (0, LANES - ROPE))).reshape(Q_RANK, HEADS * LANES),
            "w_kn": w_ukv[:, :, :HEAD_W].reshape(KV_RANK, MIX_W),
            "w_v": w_ukv[:, :, HEAD_W:].reshape(KV_RANK, MIX_W)}


def _col_slabs(full):
    n = full.shape[1] // N_DEV
    return jnp.stack([full[:, d * n:(d + 1) * n] for d in range(N_DEV)])


def _shard_grads(d_car):
    d_in = d_car["w_in"]
    d_in = jnp.concatenate([d_in[:, :Z_LR + 2 * GATE_RANK], d_in[:, Z_A:Z_KR + ROPE]], axis=1)
    by_head = lambda g: g.reshape(g.shape[0], HEADS, -1)
    d_uq = jnp.concatenate([by_head(d_car["w_qn"]), by_head(d_car["w_qr"])[:, :, :ROPE]], axis=2)
    d_ukv = jnp.concatenate([by_head(d_car["w_kn"]), by_head(d_car["w_v"])], axis=2)
    out = {"w_in": _col_slabs(d_in), "mla_w_uq": _col_slabs(d_uq.reshape(Q_RANK, -1)),
           "mla_w_ukv": _col_slabs(d_ukv.reshape(KV_RANK, -1))}
    out.update({n: d_car[n] for n in COLS + ROWS})
    return out


BIG = ("w_in", "w_out", "ffn_w1", "ffn_w3", "ffn_w2", "mla_w_uq", "mla_w_ukv")
SMALL_SHARED = ("norm1_g", "gla_onorm_g", "conf_dw_b", "conf_ln_g", "conf_ln_b", "mla_q_norm_g", "mla_kv_norm_g",
                "norm2_g")
SMALL_SHARDED = ("gla_fg_up", "gla_fg_b", "conf_dw", "sc_dw")
WEIGHTS = ("c_ctx", "norm1_g", "w_mod", "b_mod", "w_in", "gla_fg_up", "gla_fg_b", "gla_onorm_g", "conf_dw",
           "conf_dw_b", "conf_ln_g", "conf_ln_b", "sc_dw", "mla_q_norm_g", "mla_kv_norm_g", "mla_w_uq", "mla_w_ukv",
           "w_out", "norm2_g", "ffn_w1", "ffn_w3", "ffn_w2", "final_norm_g")


def _gather_last(pieces8):
    moved = jnp.moveaxis(pieces8, 0, -2)
    return moved.reshape(moved.shape[:-2] + (-1,))


def _sum_devices(x8, name):
    shape = x8.shape[1:]
    return _sum_leading(x8.reshape(N_DEV, -1, shape[-1]), name).reshape(shape)


def _step(w, m, v, x, c, ctx, loss_target):
    depth = w["norm1_g"].shape[0]
    seq, D = x.shape[1], x.shape[2]
    n_ctx = ctx.shape[1]
    me = 4 * lax.axis_index("x") + 2 * lax.axis_index("y") + lax.axis_index("c")

    shards = [{n: w[n][i:i + 1].astype(BF16) for n in BIG} for i in range(depth)]
    G = dict(zip(BIG, _all_gather([shards[0][n] for n in BIG], "gather_weights")))
    small8 = _all_gather([c] + [w[n] for n in SMALL_SHARDED], "gather_small")
    c_all = small8[0].reshape(N_DEV, D)
    small_full = {n: _gather_last(g) for n, g in zip(SMALL_SHARDED, small8[1:])}

    rows = jnp.concatenate([c_all, w["c_ctx"][None], jnp.zeros((16 - N_DEV - 1, D), F32)])
    (act,) = _rowop(_f_silu, "silu")((rows,), (), ())
    n_mod = w["w_mod"].shape[2]
    b_mine = lax.dynamic_slice_in_dim(w["b_mod"], me * n_mod, n_mod, axis=1)
    mod_cols = [_rowop(_f_add_bias, "add_bias")((_matmul(act, w["w_mod"][i], "nn", "mod_fwd"),), (),
                                                (_shared(b_mine[i]),))[0] for i in range(depth)]
    (mods8,) = _all_gather([jnp.stack(mod_cols)], "gather_mod")
    mods = jnp.moveaxis(mods8, 0, 2).reshape(depth, 16, N_DEV * n_mod)
    mods_l = [lax.dynamic_index_in_dim(mods[i], me, 0, keepdims=False) for i in range(depth)]
    mods_c = [mods[i, N_DEV] for i in range(depth)]

    smalls, cars = [], []
    plain_shapes = {"w_in": (D, Z_END), "w_qn": (Q_RANK, MIX_W), "w_qr": (Q_RANK, HEADS * LANES),
                    "w_kn": (KV_RANK, MIX_W), "w_v": (KV_RANK, MIX_W)}
    for i in range(depth):
        p = {n: w[n][i] for n in SMALL_SHARED}
        p.update({n: small_full[n][i] for n in SMALL_SHARDED})
        smalls.append(p)
        car = {n: lax.empty(s, BF16) for n, s in plain_shapes.items()}
        car.update({n: lax.empty((N_DEV,) + w[n].shape[1:], BF16) for n in COLS + ROWS})
        cars.append(car)
    t = jnp.concatenate([ctx[0], x[0]], axis=0)
    tables = _rope_tables(seq, n_ctx)
    vjps = []
    for i in range(depth):
        last = i + 1 == depth
        slots = () if last else tuple(lax.empty((3,) + w[n].shape[1:], BF16) for n in BIG)
        (t, _), vjp_i, G = jax.vjp(
            lambda t_, p_, c_, ml_, mc_, s_, G=G, nxt=None if last else shards[i + 1]: _layer(
                t_, p_, c_, ml_, mc_, s_, G, nxt, n_ctx, tables),
            t, smalls[i], cars[i], mods_l[i], mods_c[i], slots, has_aux=True)
        vjps.append(vjp_i)
    loss_lanes, d_out, d_final_g = _loss_call(t[n_ctx:], w["final_norm_g"][None], loss_target[0])
    d_t = jnp.concatenate([jnp.zeros((n_ctx, D), F32), d_out], axis=0)

    grads = {n: [None] * depth for n in BIG}
    d_smalls, d_mods_l, d_mods_c = [None] * depth, [None] * depth, [None] * depth
    pending = ()
    for i in reversed(range(depth)):
        d_t, d_smalls[i], d_car, d_mods_l[i], d_mods_c[i], arrived = vjps[i]((d_t, tuple(pending)))
        for n, g in zip(BIG, _sum_chips(pending, arrived)):
            grads[n][i + 1] = g
        sharded = _shard_grads(d_car)
        pending = _chip_sums([sharded[n] for n in BIG])
    for n, g in zip(BIG, _sum_chips(pending, _send_to_chips(pending, "rs_chips"))):
        grads[n][0] = g
    grads = {n: jnp.stack(g) for n, g in grads.items()}
    grad_x = d_t[n_ctx:][None]

    names = SMALL_SHARED + SMALL_SHARDED
    d_mod = jnp.stack([jnp.stack([d_mods_l[i], d_mods_c[i]]) for i in range(depth)])
    parts = [loss_lanes, d_final_g, d_mod] + [jnp.stack([d_smalls[i][n] for i in range(depth)]) for n in names]
    parts8 = _all_gather(parts, "gather_partials")
    summed = [_sum_devices(p8, "sum_partials") for p8 in parts8]
    (loss_row,) = _rowop(lambda a: (jnp.sum(a, axis=-1, keepdims=True) + jnp.zeros_like(a),), "loss_sum")(
        (summed[0],), (), ())
    loss = loss_row[0, 0]
    grads["final_norm_g"] = summed[1].reshape(D)
    for n, g in zip(names, summed[3:]):
        if n in SMALL_SHARDED:
            g = lax.dynamic_slice_in_dim(g, me * w[n].shape[-1], w[n].shape[-1], axis=g.ndim - 1)
        grads[n] = g

    d_mod8, d_mod_c = parts8[2], summed[2][:, 1]
    grads["b_mod"] = _rowop(lambda a, b: (a + b,), "add")((summed[2][:, 0], d_mod_c), (), ())[0]
    d_rows = jnp.concatenate([jnp.moveaxis(d_mod8[:, :, 0], 0, 1), d_mod_c[:, None],
                              jnp.zeros((depth, 16 - N_DEV - 1, 6 * D), F32)], axis=1)
    d_rows = lax.dynamic_slice_in_dim(d_rows, me * n_mod, n_mod, axis=2)
    grads["w_mod"] = jnp.stack([_matmul(act, d_rows[i], "tn", "mod_dw") for i in range(depth)])
    d_act = _matmul(d_rows.transpose(1, 0, 2).reshape(16, depth * n_mod),
                    w["w_mod"].transpose(1, 0, 2).reshape(D, depth * n_mod), "nt", "mod_dact")
    (d_act8,) = _all_gather([d_act], "gather_dact")
    (d_rows_in,) = _rowop(_f_mul_silu_grad, "silu_grad")((_sum_devices(d_act8, "sum_dact"), rows), (), ())
    grads["c_ctx"] = d_rows_in[N_DEV]

    outs = {n: _adamw(w[n], grads[n], m[n], v[n]) for n in WEIGHTS}
    return (loss, grad_x, *[grads[n] for n in WEIGHTS], *[outs[n][0] for n in WEIGHTS],
            *[outs[n][1] for n in WEIGHTS], *[outs[n][2] for n in WEIGHTS])


def kernel(x, c, ctx, c_ctx, norm1_g, w_mod, b_mod, w_in, gla_fg_up, gla_fg_b, gla_onorm_g, conf_dw, conf_dw_b, conf_ln_g, conf_ln_b, sc_dw, mla_q_norm_g, mla_kv_norm_g, mla_w_uq, mla_w_ukv, w_out, norm2_g, ffn_w1, ffn_w3, ffn_w2, final_norm_g, loss_target, m_c_ctx, m_norm1_g, m_w_mod, m_b_mod, m_w_in, m_gla_fg_up, m_gla_fg_b, m_gla_onorm_g, m_conf_dw, m_conf_dw_b, m_conf_ln_g, m_conf_ln_b, m_sc_dw, m_mla_q_norm_g, m_mla_kv_norm_g, m_mla_w_uq, m_mla_w_ukv, m_w_out, m_norm2_g, m_ffn_w1, m_ffn_w3, m_ffn_w2, m_final_norm_g, v_c_ctx, v_norm1_g, v_w_mod, v_b_mod, v_w_in, v_gla_fg_up, v_gla_fg_b, v_gla_onorm_g, v_conf_dw, v_conf_dw_b, v_conf_ln_g, v_conf_ln_b, v_sc_dw, v_mla_q_norm_g, v_mla_kv_norm_g, v_mla_w_uq, v_mla_w_ukv, v_w_out, v_norm2_g, v_ffn_w1, v_ffn_w3, v_ffn_w2, v_final_norm_g):
    w = dict(c_ctx=c_ctx, norm1_g=norm1_g, w_mod=w_mod, b_mod=b_mod, w_in=w_in, gla_fg_up=gla_fg_up, gla_fg_b=gla_fg_b, gla_onorm_g=gla_onorm_g, conf_dw=conf_dw, conf_dw_b=conf_dw_b, conf_ln_g=conf_ln_g, conf_ln_b=conf_ln_b, sc_dw=sc_dw, mla_q_norm_g=mla_q_norm_g, mla_kv_norm_g=mla_kv_norm_g, mla_w_uq=mla_w_uq, mla_w_ukv=mla_w_ukv, w_out=w_out, norm2_g=norm2_g, ffn_w1=ffn_w1, ffn_w3=ffn_w3, ffn_w2=ffn_w2, final_norm_g=final_norm_g)
    m = dict(c_ctx=m_c_ctx, norm1_g=m_norm1_g, w_mod=m_w_mod, b_mod=m_b_mod, w_in=m_w_in, gla_fg_up=m_gla_fg_up, gla_fg_b=m_gla_fg_b, gla_onorm_g=m_gla_onorm_g, conf_dw=m_conf_dw, conf_dw_b=m_conf_dw_b, conf_ln_g=m_conf_ln_g, conf_ln_b=m_conf_ln_b, sc_dw=m_sc_dw, mla_q_norm_g=m_mla_q_norm_g, mla_kv_norm_g=m_mla_kv_norm_g, mla_w_uq=m_mla_w_uq, mla_w_ukv=m_mla_w_ukv, w_out=m_w_out, norm2_g=m_norm2_g, ffn_w1=m_ffn_w1, ffn_w3=m_ffn_w3, ffn_w2=m_ffn_w2, final_norm_g=m_final_norm_g)
    v = dict(c_ctx=v_c_ctx, norm1_g=v_norm1_g, w_mod=v_w_mod, b_mod=v_b_mod, w_in=v_w_in, gla_fg_up=v_gla_fg_up, gla_fg_b=v_gla_fg_b, gla_onorm_g=v_gla_onorm_g, conf_dw=v_conf_dw, conf_dw_b=v_conf_dw_b, conf_ln_g=v_conf_ln_g, conf_ln_b=v_conf_ln_b, sc_dw=v_sc_dw, mla_q_norm_g=v_mla_q_norm_g, mla_kv_norm_g=v_mla_kv_norm_g, mla_w_uq=v_mla_w_uq, mla_w_ukv=v_mla_w_ukv, w_out=v_w_out, norm2_g=v_norm2_g, ffn_w1=v_ffn_w1, ffn_w3=v_ffn_w3, ffn_w2=v_ffn_w2, final_norm_g=v_final_norm_g)
    return _step(w, m, v, x, c, ctx, loss_target)
```

```python
import functools
import math

import jax
import jax.numpy as jnp
from jax import lax
from jax.experimental import pallas as pl
from jax.experimental.pallas import tpu as pltpu

F32 = jnp.float32
BF16 = jnp.bfloat16
MESH = pl.DeviceIdType.MESH
N_DEV = 8

EPS = 1e-6
GRID_W = 64
HEADS = 4
HEAD_W = 128
MIX_W = HEADS * HEAD_W
GATE_RANK = 16
GATE_NORM = 16.0
GLA_CHUNK = 128
CONF_K = 31
SC_K = 3
Q_RANK = 384
KV_RANK = 128
ROPE = 64
ROPE_FREQS = 16
ROPE_BASE = 10000.0
ATT_SCALE = (HEAD_W + ROPE) ** -0.5
CONV_HALO = 16

ADAM_LR = 0.001
ADAM_B1 = 0.9
ADAM_B2 = 0.999
ADAM_EPS = 1e-08
ADAM_WD = 0.01
ADAM_STEP = 10

LANES = 128
VMEM_LIMIT = 56 * 2 ** 20
ROW_BLOCK_BYTES = 10 * 2 ** 20

Z_Q, Z_K, Z_V, Z_G, Z_LR, Z_A, Z_GATE, Z_BG, Z_CG, Z_H, Z_CQ, Z_CKV, Z_KR, Z_END = (
    0, 512, 1024, 1536, 2048, 2176, 2688, 3200, 3712, 4224, 4736, 5120, 5248, 5376)
IN_W = 5216


def _pcall(body, **kw):
    return pl.pallas_call(body, **kw)


def _params(sem=None):
    return pltpu.CompilerParams(dimension_semantics=sem, vmem_limit_bytes=VMEM_LIMIT)


def _pick(dim, cap, mult):
    d = (min(cap, dim) // mult) * mult
    while d >= mult:
        if dim % d == 0:
            return d
        d -= mult
    return dim


def _mm_call(name, a, b, out_shape, grid, a_spec, b_spec, o_spec, dims, k_axis=None, once_axis=None, out_dtype=F32,
             gather=()):
    a_blk = tuple(d for d in a_spec.block_shape if d is not None)
    o_blk = tuple(d for d in o_spec.block_shape if d is not None)
    if a.dtype == BF16:
        once_axis = None
    scratch = ([pltpu.VMEM((math.prod(o_blk[:-1]), o_blk[-1]), F32)] if k_axis is not None else []) + (
        [pltpu.VMEM(a_blk, BF16)] if once_axis is not None else [])
    nk = grid[k_axis] if k_axis is not None else 1
    ng = len(gather)

    def body(*refs):
        a_ref, b_ref, o_ref, scr = refs[0], refs[1], refs[2 + ng], refs[3 + 2 * ng:]
        if ng:
            exchange = _Gather(refs[2:2 + ng], refs[3 + ng:3 + 2 * ng], *scr[-3:])
            scr = scr[:-3]
            steps = [pl.program_id(ax) for ax in range(len(grid))]
            at_first = functools.reduce(jnp.logical_and, [s == 0 for s in steps])
            at_last = functools.reduce(jnp.logical_and, [s == g - 1 for s, g in zip(steps, grid)])
            pl.when(at_first)(exchange.start)
        product(a_ref, b_ref, o_ref, scr)
        if ng:
            pl.when(at_last)(exchange.finish)

    def product(a_ref, b_ref, o_ref, scr):
        if once_axis is not None:
            a_bf = scr[-1]

            @pl.when(pl.program_id(once_axis) == 0)
            def _():
                a_bf[...] = a_ref[...].astype(BF16)

            av = a_bf[...]
        else:
            av = a_ref[...].astype(BF16)
        bv = b_ref[...].astype(BF16)
        if bv.ndim == 3:
            bv = bv.reshape(-1, bv.shape[-1])
        prod = lax.dot_general(av, bv, dims, preferred_element_type=F32)
        if k_axis is None:
            o_ref[...] = prod.astype(o_ref.dtype).reshape(o_ref.shape)
        else:
            acc, k = scr[0], pl.program_id(k_axis)

            @pl.when(k == 0)
            def _():
                acc[...] = prod

            @pl.when(k != 0)
            def _():
                acc[...] += prod

            @pl.when(k == nk - 1)
            def _():
                o_ref[...] = acc[...].astype(o_ref.dtype).reshape(o_ref.shape)

    outs = _pcall(
        body, name=name, grid=grid, in_specs=[a_spec, b_spec] + [HBM] * ng, out_specs=[o_spec] + [HBM] * ng,
        out_shape=[jax.ShapeDtypeStruct(out_shape, out_dtype)]
        + [jax.ShapeDtypeStruct((N_DEV,) + g.shape, g.dtype) for g in gather],
        scratch_shapes=scratch + (_Gather.scratch(ng) if ng else []),
        compiler_params=_params(("arbitrary",) * len(grid)),
    )(a, b, *gather)
    return (outs[0], tuple(outs[1:])) if ng else outs[0]


NN = (((1,), (0,)), ((), ()))
NT = (((1,), (1,)), ((), ()))
TN = (((0,), (0,)), ((), ()))


def _matmul(a, b, mode, name, out_dtype=F32, gather=()):
    if mode == "nn":
        (M, K), (_, N) = a.shape, b.shape
    elif mode == "nt":
        (M, K), (N, _) = a.shape, b.shape
    else:
        (K, M), (_, N) = a.shape, b.shape
    if mode == "nn":
        tm, tn, tk = _pick(M, 1088, 16), _pick(N, 768, LANES), _pick(K, 2048, LANES)
    elif mode == "nt":
        tm, tn, tk = _pick(M, 1088, 16), _pick(N, 2048, LANES), _pick(K, 1024, LANES)
    else:
        tm, tn, tk = _pick(M, 2048, LANES), _pick(N, 768, LANES), _pick(K, 1088, 16)
    if mode == "nn":
        a_spec = pl.BlockSpec((tm, tk), lambda i, j, k: (i, k))
        b_spec = pl.BlockSpec((tk, tn), lambda i, j, k: (k, j))
    elif mode == "nt":
        a_spec = pl.BlockSpec((tm, tk), lambda i, j, k: (i, k))
        b_spec = pl.BlockSpec((tn, tk), lambda i, j, k: (j, k))
    else:
        a_spec = pl.BlockSpec((tk, tm), lambda i, j, k: (k, i))
        b_spec = pl.BlockSpec((tk, tn), lambda i, j, k: (k, j))
    return _mm_call(name, a, b, (M, N), (M // tm, N // tn, K // tk), a_spec, b_spec,
                    pl.BlockSpec((tm, tn), lambda i, j, k: (i, j)), {"nn": NN, "nt": NT, "tn": TN}[mode], k_axis=2,
                    out_dtype=out_dtype, gather=gather)


def _with_gathered(result, nxt):
    return result if nxt else (result, ())


def _no_grads(res, nxt):
    return jnp.zeros_like(res), tuple(jnp.zeros_like(t) for t in nxt)


@jax.custom_vjp
def mm(a, w, carrier, nxt):
    return _with_gathered(_matmul(a, w, "nn", "mm_fwd", gather=nxt), nxt)


def _mm_fwd(a, w, carrier, nxt):
    return mm(a, w, carrier, nxt), (a, w, nxt)


def _mm_bwd(res, cts):
    a, w, nxt = res
    dc = cts[0]
    no_w, no_nxt = _no_grads(w, nxt)
    return _matmul(dc, w, "nt", "mm_da", a.dtype), no_w, _matmul(a, dc, "tn", "mm_dw", BF16), no_nxt


mm.defvjp(_mm_fwd, _mm_bwd)


def _make_mm_cols(layer):
    def forward(a, G, nxt):
        (M, K), n = a.shape, G.shape[3]
        tm = _pick(M, 1088, 16)
        return _with_gathered(_mm_call(
            "mmc_fwd", a, G, (N_DEV, M, n), (M // tm, N_DEV),
            pl.BlockSpec((tm, K), lambda i, d: (i, 0)),
            pl.BlockSpec((None, None, K, n), lambda i, d: (d, layer, 0, 0)),
            pl.BlockSpec((None, tm, n), lambda i, d: (d, i, 0)), NN, once_axis=1, out_dtype=BF16, gather=nxt), nxt)

    def grad_a(do, G, dtype):
        (_, M, n), K = do.shape, G.shape[2]
        tm, tn = _pick(M, 1088, 16), _pick(K, 1024, LANES)
        return _mm_call("mmc_da", do, G, (M, K), (M // tm, K // tn, N_DEV),
                        pl.BlockSpec((None, tm, n), lambda i, j, d: (d, i, 0)),
                        pl.BlockSpec((None, None, tn, n), lambda i, j, d: (d, layer, j, 0)),
                        pl.BlockSpec((tm, tn), lambda i, j, d: (i, j)), NT, k_axis=2, out_dtype=dtype)

    def grad_w(a, do):
        (M, K), n = a.shape, do.shape[2]
        tm, tk = _pick(K, 2048, LANES), _pick(M, 1088, 16)
        return _mm_call("mmc_dw", a, do, (N_DEV, K, n), (N_DEV, K // tm, M // tk),
                        pl.BlockSpec((tk, tm), lambda d, i, k: (k, i)),
                        pl.BlockSpec((None, tk, n), lambda d, i, k: (d, k, 0)),
                        pl.BlockSpec((None, tm, n), lambda d, i, k: (d, i, 0)), TN, k_axis=2, out_dtype=BF16)

    @jax.custom_vjp
    def f(a, G, carrier, nxt):
        return forward(a, G, nxt)

    def f_fwd(a, G, carrier, nxt):
        return forward(a, G, nxt), (a, G, nxt)

    def f_bwd(res, cts):
        a, G, nxt = res
        no_g, no_nxt = _no_grads(G, nxt)
        return grad_a(cts[0], G, a.dtype), no_g, grad_w(a, cts[0]), no_nxt

    f.defvjp(f_fwd, f_bwd)
    return f


def _make_mm_rows(layer):
    def forward(a, G, nxt):
        M, (r, N) = a.shape[-2], G.shape[2:]
        tm, tn = _pick(M, 1088, 16), _pick(N, 1024, LANES)
        if a.ndim == 2:
            return _with_gathered(_mm_call(
                "mmr2_fwd", a, G, (M, N), (M // tm, N // tn),
                pl.BlockSpec((tm, N_DEV * r), lambda i, j: (i, 0)),
                pl.BlockSpec((N_DEV, None, r, tn), lambda i, j: (0, layer, 0, j)),
                pl.BlockSpec((tm, tn), lambda i, j: (i, j)), NN, gather=nxt), nxt)
        return _with_gathered(_mm_call(
            "mmr_fwd", a, G, (M, N), (M // tm, N // tn, N_DEV),
            pl.BlockSpec((None, tm, r), lambda i, j, d: (d, i, 0)),
            pl.BlockSpec((None, None, r, tn), lambda i, j, d: (d, layer, 0, j)),
            pl.BlockSpec((tm, tn), lambda i, j, d: (i, j)), NN, k_axis=2, gather=nxt), nxt)

    def grad_a(dc, G, like):
        (M, N), r = dc.shape, G.shape[2]
        tm = _pick(M, 1088, 16)
        if like.ndim == 2:
            return _mm_call("mmr2_da", dc, G, like.shape, (M // tm,),
                            pl.BlockSpec((tm, N), lambda i: (i, 0)),
                            pl.BlockSpec((N_DEV, None, r, N), lambda i: (0, layer, 0, 0)),
                            pl.BlockSpec((tm, N_DEV * r), lambda i: (i, 0)), NT, out_dtype=like.dtype)
        return _mm_call("mmr_da", dc, G, like.shape, (M // tm, N_DEV),
                        pl.BlockSpec((tm, N), lambda i, d: (i, 0)),
                        pl.BlockSpec((None, None, r, N), lambda i, d: (d, layer, 0, 0)),
                        pl.BlockSpec((None, tm, r), lambda i, d: (d, i, 0)), NT, once_axis=1, out_dtype=like.dtype)

    def grad_w(a, dc):
        (M, N), tk = dc.shape, _pick(dc.shape[0], 1088, 16)
        if a.ndim == 2:
            r, tn = a.shape[1] // N_DEV, _pick(N, 1024, LANES)
            return _mm_call("mmr2_dw", a, dc, (N_DEV, r, N), (N // tn, M // tk),
                            pl.BlockSpec((tk, N_DEV * r), lambda j, k: (k, 0)),
                            pl.BlockSpec((tk, tn), lambda j, k: (k, j)),
                            pl.BlockSpec((N_DEV, r, tn), lambda j, k: (0, 0, j)), TN, k_axis=1, out_dtype=BF16)
        r, tn = a.shape[2], _pick(N, 2048, LANES)
        return _mm_call("mmr_dw", a, dc, (N_DEV, r, N), (N_DEV, N // tn, M // tk),
                        pl.BlockSpec((None, tk, r), lambda d, j, k: (d, k, 0)),
                        pl.BlockSpec((tk, tn), lambda d, j, k: (k, j)),
                        pl.BlockSpec((None, r, tn), lambda d, j, k: (d, 0, j)), TN, k_axis=2, out_dtype=BF16)

    @jax.custom_vjp
    def f(a, G, carrier, nxt):
        return forward(a, G, nxt)

    def f_fwd(a, G, carrier, nxt):
        return forward(a, G, nxt), (a, G, nxt)

    def f_bwd(res, cts):
        a, G, nxt = res
        dc = cts[0].astype(BF16)
        no_g, no_nxt = _no_grads(G, nxt)
        return grad_a(dc, G, a), no_g, grad_w(a, dc), no_nxt

    f.defvjp(f_fwd, f_bwd)
    return f


@functools.partial(jax.custom_vjp, nondiff_argnums=(1,))
def _split_cols(z, bounds):
    return tuple(z[:, a:b] for a, b in zip(bounds[:-1], bounds[1:]))


def _split_cols_fwd(z, bounds):
    return _split_cols(z, bounds), None


def _split_cols_bwd(bounds, _, cts):
    return (jnp.concatenate(cts, axis=1),)


_split_cols.defvjp(_split_cols_fwd, _split_cols_bwd)


def _rowop(fn, name, n_ctx=0, tile=256, lane_block=None, out_dtypes=None):
    def geometry(rows):
        L, w0 = rows[0].shape
        nj = w0 // lane_block if lane_block else 1
        width = 3 * sum(lane_block or r.shape[1] for r in rows)
        cap = max(16, ROW_BLOCK_BYTES // (4 * width) // 16 * 16)
        tl = _pick(math.gcd(L, n_ctx) if n_ctx else L, min(tile, cap), 16)
        return L, tl, n_ctx // tl, nj

    def block_w(x):
        return lane_block or x.shape[1]

    def row_spec(tl, x):
        if lane_block and x.shape[1] != lane_block:
            return pl.BlockSpec((tl, lane_block), lambda i, j: (i, j))
        return pl.BlockSpec((tl, block_w(x)), lambda i, j: (i, 0))

    def param_spec(p, nct):
        s, r, w = p.shape
        if s == 1:
            return pl.BlockSpec((1, r, w), lambda i, j: (0, 0, 0))
        return pl.BlockSpec((1, r, w), lambda i, j: (jnp.where(i < nct, 0, 1), 0, 0))

    def forward(rows, consts, params):
        L, tl, nct, nj = geometry(rows)
        nr, nc, npar = len(rows), len(consts), len(params)
        outs = jax.eval_shape(
            lambda: fn(*[jnp.zeros((tl, block_w(r)), F32) for r in rows + consts],
                       *[jnp.zeros(p.shape[1:], F32) for p in params]))

        def body(*refs):
            ins = [r[...].astype(F32) for r in refs[:nr + nc]] + [r[0] for r in refs[nr + nc:nr + nc + npar]]
            for o_ref, o in zip(refs[nr + nc + npar:], fn(*ins)):
                o_ref[...] = o.astype(o_ref.dtype)

        return _pcall(
            body, name=name + "_fwd", grid=(L // tl, nj),
            in_specs=[row_spec(tl, r) for r in rows + consts] + [param_spec(p, nct) for p in params],
            out_specs=[pl.BlockSpec((tl, o.shape[1]), lambda i, j: (i, j)) for o in outs],
            out_shape=[jax.ShapeDtypeStruct((L, o.shape[1] * nj), dt)
                       for o, dt in zip(outs, out_dtypes or (F32,) * len(outs))],
            compiler_params=_params(("parallel", "parallel")),
        )(*rows, *consts, *params)

    def backward(rows, consts, params, cts):
        L, tl, nct, nj = geometry(rows)
        nr, nc, npar, no = len(rows), len(consts), len(params), len(cts)

        def body(*refs):
            i, j = pl.program_id(0), pl.program_id(1)
            rv = [r[...].astype(F32) for r in refs[:nr]]
            cv = [r[...] for r in refs[nr:nr + nc]]
            pv = [r[0] for r in refs[nr + nc:nr + nc + npar]]
            ct = tuple(r[...].astype(F32) for r in refs[nr + nc + npar:nr + nc + npar + no])
            out_refs = refs[nr + nc + npar + no:]
            _, vjp = jax.vjp(lambda *d: tuple(fn(*d[:nr], *cv, *d[nr:])), *rv, *pv)
            grads = vjp(ct)
            for ref, g in zip(out_refs[:nr], grads[:nr]):
                ref[...] = g.astype(ref.dtype)
            for ref, g, p in zip(out_refs[nr:], grads[nr:], params):
                first_row = (i == 0) if (p.shape[0] == 1 or nct == 0) else ((i == 0) | (i == nct))
                first = jnp.logical_and(first_row, j == 0)

                @pl.when(first)
                def _():
                    ref[0] = g

                @pl.when(jnp.logical_not(first))
                def _():
                    ref[0] += g

        outs = _pcall(
            body, name=name + "_bwd", grid=(L // tl, nj),
            in_specs=[row_spec(tl, r) for r in rows + consts] + [param_spec(p, nct) for p in params]
            + [pl.BlockSpec((tl, c.shape[1] // nj), lambda i, j: (i, j)) for c in cts],
            out_specs=[row_spec(tl, r) for r in rows] + [param_spec(p, nct) for p in params],
            out_shape=[jax.ShapeDtypeStruct(r.shape, r.dtype) for r in rows]
            + [jax.ShapeDtypeStruct(p.shape, F32) for p in params],
            compiler_params=_params(("arbitrary", "arbitrary")),
        )(*rows, *consts, *params, *cts)
        return tuple(outs[:nr]), tuple(outs[nr:])

    @jax.custom_vjp
    def op(rows, consts, params):
        return tuple(forward(rows, consts, params))

    def op_fwd(rows, consts, params):
        return tuple(forward(rows, consts, params)), (rows, consts, params)

    def op_bwd(res, cts):
        rows, consts, params = res
        d_rows, d_params = backward(rows, consts, params, tuple(cts))
        return d_rows, tuple(jnp.zeros_like(c) for c in consts), d_params

    op.defvjp(op_fwd, op_bwd)
    return op


def _sigmoid(x):
    return 1.0 / (1.0 + jnp.exp(-x))


def _silu(x):
    return x * _sigmoid(x)


def _log_sigmoid(x):
    return jnp.minimum(x, 0.0) - jnp.log(1.0 + jnp.exp(-jnp.abs(x)))


def _rms(x, g):
    return x * lax.rsqrt(jnp.mean(x * x, axis=-1, keepdims=True) + EPS) * g


def _f_norm_mod(x, g, sc, sh):
    return (_rms(x, g) * (1.0 + sc) + sh,)


def _f_resid_norm_mod(x, o, gate, g, sc, sh):
    x1 = x + gate * o
    return x1, _rms(x1, g) * (1.0 + sc) + sh


def _f_resid(x, o, gate):
    return (x + gate * o,)


def _f_swiglu(a1, a3):
    return (_silu(a1) * a3,)


def _f_gla_gate(lr, up_f, up_b, b_f, b_b):
    dot = functools.partial(jnp.dot, preferred_element_type=F32)
    return (_log_sigmoid(dot(lr, up_f) + b_f) / GATE_NORM, _log_sigmoid(dot(lr, up_b) + b_b) / GATE_NORM)


def _f_gla_finish(o_f, o_b, gate, g):
    return (_rms(o_f + o_b, g) * _silu(gate),)


def _f_glu(a, gate):
    return (a * _sigmoid(gate),)


def _f_ln_silu(u, dw_b, g, b):
    u = u + dw_b
    xc = u - jnp.mean(u, axis=-1, keepdims=True)
    y = xc * lax.rsqrt(jnp.mean(xc * xc, axis=-1, keepdims=True) + EPS)
    return (_silu(y * g + b),)


def _f_mul(a, b):
    return (a * b,)


def _f_rms(x, g):
    return (_rms(x, g),)


@jax.custom_vjp
def _rope_partner(t):
    w = t.shape[1]
    lane = lax.broadcasted_iota(jnp.int32, t.shape, 1)
    return jnp.where(jnp.bitwise_and(lane, ROPE_FREQS) == 0, pltpu.roll(t, w - ROPE_FREQS, 1), pltpu.roll(t, ROPE_FREQS, 1))


_rope_partner.defvjp(lambda t: (_rope_partner(t), None), lambda _, d: (_rope_partner(d),))


def _f_rope(t, cos, sin):
    return (t * cos + _rope_partner(t) * sin,)


def _f_silu(x):
    return (_silu(x),)


def _f_add_bias(x, b):
    return (x + b,)


def _f_mul_silu_grad(d, x):
    _, vjp = jax.vjp(_silu, x)
    return (vjp(d)[0],)


def _conv_geometry(u, n_ctx):
    L, C = u.shape
    tl = _pick(math.gcd(L, n_ctx), 256, 8)
    return L, C, tl, n_ctx // tl, L // tl


def _conv_specs(tl, nt):
    prev = pl.BlockSpec((tl, LANES), lambda c, i: (jnp.maximum(i - 1, 0), c))
    cur = pl.BlockSpec((tl, LANES), lambda c, i: (i, c))
    nxt = pl.BlockSpec((tl, LANES), lambda c, i: (jnp.minimum(i + 1, nt - 1), c))
    return prev, cur, nxt


def _conv_window(prev_ref, cur_ref, next_ref, tl, nct, nt):
    i = pl.program_id(1)
    has_prev = jnp.logical_and(i != 0, i != nct)
    has_next = jnp.logical_and(i != nct - 1, i != nt - 1)
    prev = jnp.where(has_prev, prev_ref[tl - CONV_HALO:tl, :], 0.0)
    nxt = jnp.where(has_next, next_ref[0:CONV_HALO, :], 0.0)
    return jnp.concatenate([prev, cur_ref[...], nxt], axis=0)


def _shifted(window, off, tl):
    n = window.shape[0]
    if off == 0:
        return window[0:tl]
    return pltpu.roll(window, n - off, 0)[0:tl]


def _conv_apply(u, w, n_ctx, flip, name):
    L, C, tl, nct, nt = _conv_geometry(u, n_ctx)
    K = w.shape[0]
    pad = (K - 1) // 2
    prev, cur, nxt = _conv_specs(tl, nt)

    def body(p_ref, c_ref, n_ref, w_ref, o_ref):
        win = _conv_window(p_ref, c_ref, n_ref, tl, nct, nt)
        acc = jnp.zeros((tl, LANES), F32)
        for k in range(K):
            kk = K - 1 - k if flip else k
            acc = acc + _shifted(win, CONV_HALO - pad + k, tl) * w_ref[kk:kk + 1, :]
        o_ref[...] = acc

    return _pcall(
        body, name=name, grid=(C // LANES, nt),
        in_specs=[prev, cur, nxt, pl.BlockSpec((K, LANES), lambda c, i: (0, c))],
        out_specs=cur, out_shape=jax.ShapeDtypeStruct((L, C), F32),
        compiler_params=_params(("parallel", "parallel")),
    )(u, u, u, w)


def _conv_dw(u, dy, K, n_ctx, name):
    L, C, tl, nct, nt = _conv_geometry(u, n_ctx)
    pad = (K - 1) // 2
    prev, cur, nxt = _conv_specs(tl, nt)

    def body(p_ref, c_ref, n_ref, dy_ref, dw_ref):
        i = pl.program_id(1)

        @pl.when(i == 0)
        def _():
            dw_ref[...] = jnp.zeros_like(dw_ref)

        win = _conv_window(p_ref, c_ref, n_ref, tl, nct, nt)
        dy_t = dy_ref[...]
        for k in range(K):
            dw_ref[k:k + 1, :] += jnp.sum(_shifted(win, CONV_HALO - pad + k, tl) * dy_t, axis=0, keepdims=True)

    return _pcall(
        body, name=name, grid=(C // LANES, nt),
        in_specs=[prev, cur, nxt, cur],
        out_specs=pl.BlockSpec((K, LANES), lambda c, i: (0, c)),
        out_shape=jax.ShapeDtypeStruct((K, C), F32),
        compiler_params=_params(("parallel", "arbitrary")),
    )(u, u, u, dy)


def _make_conv(n_ctx):
    @jax.custom_vjp
    def conv(u, w):
        return _conv_apply(u, w, n_ctx, False, "conv_fwd")

    def conv_fwd(u, w):
        return _conv_apply(u, w, n_ctx, False, "conv_fwd"), (u, w)

    def conv_bwd(res, dy):
        u, w = res
        return _conv_apply(dy, w, n_ctx, True, "conv_du"), _conv_dw(u, dy, w.shape[0], n_ctx, "conv_dw")

    conv.defvjp(conv_fwd, conv_bwd)
    return conv


def _gla_chunk(q, k, v, g, st, reverse):
    C = q.shape[0]
    r = lax.broadcasted_iota(jnp.int32, (C, C), 0)
    c = lax.broadcasted_iota(jnp.int32, (C, C), 1)
    seen = (r <= c) if reverse else (r >= c)
    dot = functools.partial(lax.dot_general, preferred_element_type=F32, precision=lax.Precision.DEFAULT)
    bcum = lax.dot_general(seen.astype(F32), g, (((1,), (0,)), ((), ())), preferred_element_type=F32,
                           precision=lax.Precision.HIGHEST)
    total = jnp.sum(g, axis=0, keepdims=True)
    a = q * (HEAD_W ** -0.5) * jnp.exp(bcum)
    scores = jnp.where(seen, dot(a, k * jnp.exp(-bcum), (((1,), (1,)), ((), ()))), 0.0)
    o = dot(a, st, (((1,), (1,)), ((), ()))) + dot(scores, v, (((1,), (0,)), ((), ())))
    st_new = st * jnp.exp(total) + dot(v, k * jnp.exp(total - bcum), (((0,), (0,)), ((), ())))
    return o, st_new


def _gla_order(t, nc, ncc, reverse):
    if not reverse:
        return t
    return jnp.where(t < ncc, ncc - 1 - t, ncc + nc - 1 - t)


def _gla_fwd_call(q, k, v, g, n_ctx, reverse):
    L = q.shape[0]
    C = GLA_CHUNK
    nc, ncc = L // C, n_ctx // C
    spec = pl.BlockSpec((C, MIX_W), lambda t: (_gla_order(t, nc, ncc, reverse), 0))

    def body(q_ref, k_ref, v_ref, g_ref, o_ref, s_ref, st):
        @pl.when(pl.program_id(0) == 0)
        def _():
            st[...] = jnp.zeros_like(st)

        for h in range(HEADS):
            hs = slice(h * HEAD_W, (h + 1) * HEAD_W)
            s_ref[h, 0] = st[h]
            o, st_new = _gla_chunk(q_ref[:, hs], k_ref[:, hs], v_ref[:, hs], g_ref[:, hs], st[h], reverse)
            o_ref[:, hs] = o
            st[h] = st_new

    return _pcall(
        body, name="gla_fwd", grid=(nc,), in_specs=[spec] * 4,
        out_specs=[spec, pl.BlockSpec((HEADS, 1, HEAD_W, HEAD_W), lambda t: (0, t, 0, 0))],
        out_shape=[jax.ShapeDtypeStruct((L, MIX_W), F32), jax.ShapeDtypeStruct((HEADS, nc, HEAD_W, HEAD_W), F32)],
        scratch_shapes=[pltpu.VMEM((HEADS, HEAD_W, HEAD_W), F32)],
        compiler_params=_params(("arbitrary",)),
    )(q, k, v, g)


def _gla_bwd_call(q, k, v, g, states, do, n_ctx, reverse):
    L = q.shape[0]
    C = GLA_CHUNK
    nc, ncc = L // C, n_ctx // C
    spec = pl.BlockSpec((C, MIX_W), lambda t: (_gla_order(nc - 1 - t, nc, ncc, reverse), 0))

    def body(q_ref, k_ref, v_ref, g_ref, s_ref, do_ref, dq_ref, dk_ref, dv_ref, dg_ref, dst):
        @pl.when(pl.program_id(0) == 0)
        def _():
            dst[...] = jnp.zeros_like(dst)

        for h in range(HEADS):
            hs = slice(h * HEAD_W, (h + 1) * HEAD_W)
            _, vjp = jax.vjp(functools.partial(_gla_chunk, reverse=reverse),
                             q_ref[:, hs], k_ref[:, hs], v_ref[:, hs], g_ref[:, hs], s_ref[h, 0])
            dq, dk, dv, dg, dst_prev = vjp((do_ref[:, hs], dst[h]))
            dq_ref[:, hs] = dq
            dk_ref[:, hs] = dk
            dv_ref[:, hs] = dv
            dg_ref[:, hs] = dg
            dst[h] = dst_prev

    return _pcall(
        body, name="gla_bwd", grid=(nc,),
        in_specs=[spec] * 4 + [pl.BlockSpec((HEADS, 1, HEAD_W, HEAD_W), lambda t: (0, nc - 1 - t, 0, 0)), spec],
        out_specs=[spec] * 4, out_shape=[jax.ShapeDtypeStruct((L, MIX_W), F32)] * 4,
        scratch_shapes=[pltpu.VMEM((HEADS, HEAD_W, HEAD_W), F32)],
        compiler_params=_params(("arbitrary",)),
    )(q, k, v, g, states, do)


def _make_gla(n_ctx, reverse):
    @jax.custom_vjp
    def gla(q, k, v, g):
        return _gla_fwd_call(q, k, v, g, n_ctx, reverse)[0]

    def gla_fwd(q, k, v, g):
        o, states = _gla_fwd_call(q, k, v, g, n_ctx, reverse)
        return o, (q, k, v, g, states)

    def gla_bwd(res, do):
        q, k, v, g, states = res
        return tuple(_gla_bwd_call(q, k, v, g, states, do, n_ctx, reverse))

    gla.defvjp(gla_fwd, gla_bwd)
    return gla


def _att_scaled(q_ref):
    return (q_ref[...] * ATT_SCALE).astype(BF16)


def _att_probs(qn, qr, kn, kr, i, nct, n_ctx):
    nt_dims = (((1,), (1,)), ((), ()))
    s = lax.dot_general(qn, kn, nt_dims, preferred_element_type=F32)
    s = s + lax.dot_general(qr, kr, nt_dims, preferred_element_type=F32)
    col = lax.broadcasted_iota(jnp.int32, (1, s.shape[1]), 1)
    s = s + jnp.where(col < jnp.where(i < nct, n_ctx, s.shape[1]), 0.0, -1e30)
    p = jnp.exp(s - jnp.max(s, axis=-1, keepdims=True))
    return p * (1.0 / jnp.sum(p, axis=-1, keepdims=True))


def _att_geometry(qn, n_ctx):
    L = qn.shape[0]
    tq = _pick(math.gcd(L, n_ctx), 256, 8)
    q_spec = pl.BlockSpec((tq, HEAD_W), lambda h, i: (i, h))
    k_spec = pl.BlockSpec((L, HEAD_W), lambda h, i: (0, h))
    kr_spec = pl.BlockSpec((L, HEAD_W), lambda h, i: (0, 0))
    return L, tq, n_ctx // tq, q_spec, k_spec, kr_spec


def _att_fwd_call(qn, qr, kn, kr, v, n_ctx):
    L, tq, nct, q_spec, k_spec, kr_spec = _att_geometry(qn, n_ctx)

    def body(qn_ref, qr_ref, kn_ref, kr_ref, v_ref, o_ref):
        p = _att_probs(_att_scaled(qn_ref), _att_scaled(qr_ref), kn_ref[...].astype(BF16),
                       kr_ref[...].astype(BF16), pl.program_id(1), nct, n_ctx)
        o_ref[...] = jnp.dot(p.astype(BF16), v_ref[...].astype(BF16), preferred_element_type=F32).astype(BF16)

    return _pcall(
        body, name="att_fwd", grid=(HEADS, L // tq),
        in_specs=[q_spec, q_spec, k_spec, kr_spec, k_spec], out_specs=q_spec,
        out_shape=jax.ShapeDtypeStruct((L, MIX_W), BF16),
        compiler_params=_params(("parallel", "parallel")),
    )(qn, qr, kn, kr, v)


def _att_bwd_call(qn, qr, kn, kr, v, do, n_ctx, payload=()):
    L, tq, nct, q_spec, k_spec, kr_spec = _att_geometry(qn, n_ctx)
    tn_dims = (((0,), (0,)), ((), ()))
    nq, npay = L // tq, len(payload)

    def body(*refs):
        (qn_ref, qr_ref, kn_ref, kr_ref, v_ref, do_ref), refs = refs[:6], refs[6:]
        p_refs, (dqn_ref, dqr_ref, dkn_ref, dkr_ref, dv_ref) = refs[:npay], refs[npay:npay + 5]
        h, i = pl.program_id(0), pl.program_id(1)
        if npay:
            exchange = _ChipExchange(p_refs, refs[npay + 5:2 * npay + 5], *refs[2 * npay + 5:])
            pl.when(jnp.logical_and(h == 0, i == 0))(exchange.start)
        qn, qr = _att_scaled(qn_ref), _att_scaled(qr_ref)
        kn, kr, vv = kn_ref[...].astype(BF16), kr_ref[...].astype(BF16), v_ref[...].astype(BF16)
        do = do_ref[...].astype(BF16)
        p = _att_probs(qn, qr, kn, kr, i, nct, n_ctx)
        dp = lax.dot_general(do, vv, (((1,), (1,)), ((), ())), preferred_element_type=F32)
        ds = (p * (dp - jnp.sum(p * dp, axis=-1, keepdims=True))).astype(BF16)
        dqn_ref[...] = jnp.dot(ds, kn, preferred_element_type=F32) * ATT_SCALE
        dqr_ref[...] = jnp.dot(ds, kr, preferred_element_type=F32) * ATT_SCALE

        @pl.when(i == 0)
        def _():
            dkn_ref[...] = jnp.zeros_like(dkn_ref)
            dv_ref[...] = jnp.zeros_like(dv_ref)

        @pl.when(jnp.logical_and(i == 0, h == 0))
        def _():
            dkr_ref[...] = jnp.zeros_like(dkr_ref)

        dkn_ref[...] += lax.dot_general(ds, qn, tn_dims, preferred_element_type=F32)
        dkr_ref[...] += lax.dot_general(ds, qr, tn_dims, preferred_element_type=F32)
        dv_ref[...] += lax.dot_general(p.astype(BF16), do, tn_dims, preferred_element_type=F32)
        if npay:
            pl.when(jnp.logical_and(h == HEADS - 1, i == nq - 1))(exchange.finish)

    outs = _pcall(
        body, name="att_bwd", grid=(HEADS, nq),
        in_specs=[q_spec, q_spec, k_spec, kr_spec, k_spec, q_spec] + [HBM] * npay,
        out_specs=[q_spec, q_spec, k_spec, kr_spec, k_spec] + [HBM] * npay,
        out_shape=[jax.ShapeDtypeStruct((L, MIX_W), F32)] * 3
        + [jax.ShapeDtypeStruct((L, HEAD_W), F32), jax.ShapeDtypeStruct((L, MIX_W), F32)]
        + [jax.ShapeDtypeStruct((3,) + p.shape[1:], p.dtype) for p in payload],
        scratch_shapes=_ChipExchange.scratch(npay) if npay else [],
        compiler_params=_params(("arbitrary", "arbitrary")),
    )(qn, qr, kn, kr, v, do, *payload)
    return tuple(outs[:5]), tuple(outs[5:])


def _make_attention(n_ctx):
    @jax.custom_vjp
    def att(qn, qr, kn, kr, v, slots):
        return _att_fwd_call(qn, qr, kn, kr, v, n_ctx), tuple(lax.empty((4,) + s.shape[1:], s.dtype) for s in slots)

    def att_fwd(qn, qr, kn, kr, v, slots):
        return att(qn, qr, kn, kr, v, slots), (qn, qr, kn, kr, v)

    def att_bwd(res, cts):
        grads, arrived = _att_bwd_call(*res, cts[0], n_ctx, tuple(cts[1]))
        return (*grads, arrived)

    att.defvjp(att_fwd, att_bwd)
    return att


def _loss_call(x, g, target):
    L, D = x.shape
    tl = _pick(L, 256, 8)

    def f(xv, gv, tv):
        err = _rms(xv, gv) - tv
        return 0.5 * jnp.sum(err * err, axis=0, keepdims=True) / D

    def body(x_ref, g_ref, t_ref, loss_ref, dx_ref, dg_ref):
        i = pl.program_id(0)
        loss, vjp = jax.vjp(lambda xv, gv: f(xv, gv, t_ref[...]), x_ref[...], g_ref[...])
        dx, dg = vjp(jnp.ones_like(loss))
        dx_ref[...] = dx

        @pl.when(i == 0)
        def _():
            loss_ref[...] = loss
            dg_ref[...] = dg

        @pl.when(i != 0)
        def _():
            loss_ref[...] += loss
            dg_ref[...] += dg

    row = pl.BlockSpec((tl, D), lambda i: (i, 0))
    one = pl.BlockSpec((1, D), lambda i: (0, 0))
    return _pcall(
        body, name="loss", grid=(L // tl,), in_specs=[row, one, row], out_specs=[one, row, one],
        out_shape=[jax.ShapeDtypeStruct((1, D), F32), jax.ShapeDtypeStruct((L, D), F32),
                   jax.ShapeDtypeStruct((1, D), F32)],
        compiler_params=_params(("arbitrary",)),
    )(x, g, target)


def _sum_leading(x, name):
    n, R, W = x.shape
    tr = _pick(R, 512, 8)

    def body(x_ref, o_ref):
        acc = x_ref[0]
        for d in range(1, n):
            acc = acc + x_ref[d]
        o_ref[...] = acc

    return _pcall(
        body, name=name, grid=(R // tr,), in_specs=[pl.BlockSpec((n, tr, W), lambda i: (0, i, 0))],
        out_specs=pl.BlockSpec((tr, W), lambda i: (i, 0)), out_shape=jax.ShapeDtypeStruct((R, W), F32),
        compiler_params=_params(("parallel",)),
    )(x)


def _adamw(w, g, m, v):
    shape = w.shape
    W = shape[-1]
    as2d = lambda t: t.reshape(-1, W)
    R = as2d(w).shape[0]
    tr = _pick(R, max(8, (2 ** 17 // W) // 8 * 8), 8)

    def body(w_ref, g_ref, m_ref, v_ref, d_ref, nm_ref, nv_ref):
        gv = g_ref[...]
        m_new = ADAM_B1 * m_ref[...] + (1.0 - ADAM_B1) * gv
        v_new = ADAM_B2 * v_ref[...] + (1.0 - ADAM_B2) * (gv * gv)
        m_hat = m_new / (1.0 - ADAM_B1 ** ADAM_STEP)
        v_hat = v_new / (1.0 - ADAM_B2 ** ADAM_STEP)
        d_ref[...] = -ADAM_LR * (m_hat / (jnp.sqrt(v_hat) + ADAM_EPS) + ADAM_WD * w_ref[...])
        nm_ref[...] = m_new
        nv_ref[...] = v_new

    spec = pl.BlockSpec((tr, W), lambda i: (i, 0))
    outs = _pcall(
        body, name="adamw", grid=(R // tr,), in_specs=[spec] * 4, out_specs=[spec] * 3,
        out_shape=[jax.ShapeDtypeStruct((R, W), F32)] * 3, compiler_params=_params(("parallel",)),
    )(as2d(w), as2d(g), as2d(m), as2d(v))
    return tuple(o.reshape(shape) for o in outs)


HBM = pl.BlockSpec(memory_space=pltpu.HBM)


def _place():
    x, y, c = lax.axis_index("x"), lax.axis_index("y"), lax.axis_index("c")
    return x, y, c, [(1 - x, y), (x, 1 - y), (1 - x, 1 - y)]


class _Gather:
    def __init__(self, x_refs, out_refs, send_sems, recv_sems, local_sems):
        self.x_refs, self.out_refs, self.n = x_refs, out_refs, len(x_refs)
        self.sems = send_sems, recv_sems, local_sems
        self.x, self.y, self.c, self.chips = _place()
        self.me, self.sibling = (self.x, self.y, self.c), (self.x, self.y, 1 - self.c)

    @staticmethod
    def scratch(n):
        return [pltpu.SemaphoreType.DMA((7 * n,)), pltpu.SemaphoreType.DMA((7 * n,)), pltpu.SemaphoreType.DMA((n,))]

    def slot(self, a, px, py, pc):
        return self.out_refs[a].at[4 * px + 2 * py + pc]

    def copy(self, a, k, blk, to, src=None):
        return pltpu.make_async_remote_copy(
            src_ref=self.slot(a, *blk) if src is None else src, dst_ref=self.slot(a, *blk),
            send_sem=self.sems[0].at[7 * a + k], recv_sem=self.sems[1].at[7 * a + k], device_id=to,
            device_id_type=MESH)

    def own(self):
        first = []
        for a in range(self.n):
            first.append(self.copy(a, 0, self.me, self.sibling, src=self.x_refs[a]))
            first += [self.copy(a, 1 + j, self.me, (*chip, self.c), src=self.x_refs[a])
                      for j, chip in enumerate(self.chips)]
        return [pltpu.make_async_copy(self.x_refs[a], self.slot(a, *self.me), self.sems[2].at[a])
                for a in range(self.n)], first

    def start(self):
        mine, first = self.own()
        for cp in mine + first:
            cp.start()

    def finish(self):
        c, chips = self.c, self.chips
        mine, first = self.own()
        passed = []
        for j, chip in enumerate(chips):
            for a in range(self.n):
                self.copy(a, 1 + j, (*chip, c), self.me).wait_recv()
                passed.append(self.copy(a, 4 + j, (*chip, c), self.sibling))
                passed[-1].start()
        for a in range(self.n):
            self.copy(a, 0, self.sibling, self.me).wait_recv()
        for j, chip in enumerate(chips):
            for a in range(self.n):
                self.copy(a, 4 + j, (*chip, 1 - c), self.me).wait_recv()
        for cp in first + passed:
            cp.wait_send()
        for cp in mine:
            cp.wait()


def _all_gather(blocks, name):
    n = len(blocks)

    def body(*refs):
        g = _Gather(refs[:n], refs[n:2 * n], *refs[2 * n:])
        g.start()
        g.finish()

    return _pcall(
        body, name=name, out_shape=[jax.ShapeDtypeStruct((N_DEV,) + b.shape, b.dtype) for b in blocks],
        in_specs=[HBM] * n, out_specs=[HBM] * n, scratch_shapes=_Gather.scratch(n),
    )(*blocks)


def _send_to_sibling(gs, name):
    n = len(gs)

    def body(*refs):
        g_refs, out_refs, (send_sems, recv_sems) = refs[:n], refs[n:2 * n], refs[2 * n:]
        x, y, c, _ = _place()
        copies = [pltpu.make_async_remote_copy(
            src_ref=g_refs[a].at[2 * q + 1 - c], dst_ref=out_refs[a].at[q], send_sem=send_sems.at[4 * a + q],
            recv_sem=recv_sems.at[4 * a + q], device_id=(x, y, 1 - c), device_id_type=MESH)
            for a in range(n) for q in range(4)]
        for cp in copies:
            cp.start()
        for cp in copies:
            cp.wait()

    return _pcall(
        body, name=name, out_shape=[jax.ShapeDtypeStruct((4,) + g.shape[1:], g.dtype) for g in gs],
        in_specs=[HBM] * n, out_specs=[HBM] * n,
        scratch_shapes=[pltpu.SemaphoreType.DMA((4 * n,)), pltpu.SemaphoreType.DMA((4 * n,))],
    )(*gs)


class _ChipExchange:
    def __init__(self, p_refs, out_refs, send_sems, recv_sems):
        self.p_refs, self.out_refs, self.sems = p_refs, out_refs, (send_sems, recv_sems)

    @staticmethod
    def scratch(n):
        return [pltpu.SemaphoreType.DMA((3 * n,)), pltpu.SemaphoreType.DMA((3 * n,))]

    def copies(self):
        x, y, c, chips = _place()
        return [pltpu.make_async_remote_copy(
            src_ref=self.p_refs[a].at[2 * cx + cy], dst_ref=self.out_refs[a].at[j], send_sem=self.sems[0].at[3 * a + j],
            recv_sem=self.sems[1].at[3 * a + j], device_id=(cx, cy, c), device_id_type=MESH)
            for a in range(len(self.p_refs)) for j, (cx, cy) in enumerate(chips)]

    def start(self):
        for cp in self.copies():
            cp.start()

    def finish(self):
        for cp in self.copies():
            cp.wait()


def _send_to_chips(ps, name):
    n = len(ps)

    def body(*refs):
        exchange = _ChipExchange(refs[:n], refs[n:2 * n], *refs[2 * n:])
        exchange.start()
        exchange.finish()

    return _pcall(
        body, name=name, out_shape=[jax.ShapeDtypeStruct((3,) + p.shape[1:], p.dtype) for p in ps],
        in_specs=[HBM] * n, out_specs=[HBM] * n, scratch_shapes=_ChipExchange.scratch(n),
    )(*ps)


def _add_rows(R, W):
    return _pick(R, max(16, 2 ** 19 // W // 16 * 16), 16)


def _add_sibling(g, recv, core):
    _, R, W = g.shape
    tr = _add_rows(R, W)

    def body(core_ref, g_ref, r_ref, o_ref):
        o_ref[...] = (g_ref[...].astype(F32) + r_ref[...].astype(F32)).astype(BF16)

    return _pcall(
        body, name="rs_add_sibling",
        grid_spec=pltpu.PrefetchScalarGridSpec(
            num_scalar_prefetch=1, grid=(4, R // tr),
            in_specs=[pl.BlockSpec((None, tr, W), lambda q, i, core_ref: (2 * q + core_ref[0], i, 0)),
                      pl.BlockSpec((None, tr, W), lambda q, i, core_ref: (q, i, 0))],
            out_specs=pl.BlockSpec((None, tr, W), lambda q, i, core_ref: (q, i, 0))),
        out_shape=jax.ShapeDtypeStruct((4, R, W), BF16), compiler_params=_params(("parallel", "parallel")),
    )(core, g, recv)


def _add_chips(p, recv, chip):
    _, R, W = p.shape
    tr = _add_rows(R, W)

    def body(chip_ref, p_ref, r_ref, o_ref):
        up = lambda t: t.astype(F32)
        o_ref[...] = ((up(p_ref[...]) + up(r_ref[0])) + up(r_ref[1])) + up(r_ref[2])

    return _pcall(
        body, name="rs_add_chips",
        grid_spec=pltpu.PrefetchScalarGridSpec(
            num_scalar_prefetch=1, grid=(R // tr,),
            in_specs=[pl.BlockSpec((None, tr, W), lambda i, chip_ref: (chip_ref[0], i, 0)),
                      pl.BlockSpec((3, tr, W), lambda i, chip_ref: (0, i, 0))],
            out_specs=pl.BlockSpec((tr, W), lambda i, chip_ref: (i, 0))),
        out_shape=jax.ShapeDtypeStruct((R, W), F32), compiler_params=_params(("parallel",)),
    )(chip, p, recv)


def _chip_sums(gs):
    core = jnp.reshape(lax.axis_index("c"), (1,)).astype(jnp.int32)
    return [_add_sibling(g, r, core) for g, r in zip(gs, _send_to_sibling(gs, "rs_sibling"))]


def _sum_chips(chip_sums, from_chips):
    chip = jnp.reshape(2 * lax.axis_index("x") + lax.axis_index("y"), (1,)).astype(jnp.int32)
    return [_add_chips(p, r, chip) for p, r in zip(chip_sums, from_chips)]


def _rope_tables(seq, n_ctx):
    rows = seq // GRID_W
    row = jnp.repeat(jnp.arange(rows, dtype=F32), GRID_W)
    col = jnp.tile(jnp.arange(GRID_W, dtype=F32), rows)
    inv = ROPE_BASE ** (-jnp.arange(ROPE_FREQS, dtype=F32) * 2.0 / (ROPE // 2))
    ang_r, ang_c = row[:, None] * inv, col[:, None] * inv
    one, zero = jnp.ones((seq, ROPE), F32), jnp.zeros((seq, ROPE), F32)
    cos = jnp.concatenate([jnp.cos(ang_r), jnp.cos(ang_r), jnp.cos(ang_c), jnp.cos(ang_c), one], axis=1)
    sin = jnp.concatenate([-jnp.sin(ang_r), jnp.sin(ang_r), -jnp.sin(ang_c), jnp.sin(ang_c), zero], axis=1)
    cos = jnp.concatenate([jnp.ones((n_ctx, LANES), F32), cos], axis=0)
    sin = jnp.concatenate([jnp.zeros((n_ctx, LANES), F32), sin], axis=0)
    return cos, sin


def _shared(v):
    return v.reshape((1, 1, -1)) if v.ndim == 1 else v.reshape((1,) + v.shape)


Z_BOUNDS = (Z_Q, Z_K, Z_V, Z_G, Z_LR, Z_A, Z_GATE, Z_BG, Z_CG, Z_H, Z_CQ, Z_CKV, Z_KR, Z_END)
PLAIN = ("w_in", "w_qn", "w_qr", "w_kn", "w_v")
COLS = ("ffn_w1", "ffn_w3")
ROWS = ("w_out", "ffn_w2")


def _layer(t, p, car, mod_l, mod_c, slots, G, nxt, n_ctx, tables):
    L, D = t.shape
    seg = lambda i: jnp.stack([mod_c[i * D:(i + 1) * D], mod_l[i * D:(i + 1) * D]]).reshape(2, 1, D)
    sh1, sc1, g1, sh2, sc2, g2 = (seg(i) for i in range(6))
    plain = _plain_weights(G, D)
    dense = lambda a, n: mm(a, plain[n], car[n], ())[0]
    take = lambda *names: tuple(nxt[n] for n in names) if nxt else ()

    (h,) = _rowop(_f_norm_mod, "norm_mod", n_ctx, out_dtypes=(BF16,))((t,), (), (_shared(p["norm1_g"]), sc1, sh1))
    z, got_in = mm(h, plain["w_in"], car["w_in"], take("w_in"))
    q, k, v, gate, lr, conf_a, conf_gate, sc_b, sc_c, sc_h, cq, ckv, kr = _split_cols(z, Z_BOUNDS)

    up = p["gla_fg_up"]
    up_f = jnp.pad(up[0], ((0, LANES - GATE_RANK), (0, 0)))
    up_b = jnp.pad(up[1], ((GATE_RANK, LANES - 2 * GATE_RANK), (0, 0)))
    logd_f, logd_b = _rowop(_f_gla_gate, "gla_gate")(
        (lr,), (), (_shared(up_f), _shared(up_b), _shared(p["gla_fg_b"][0]), _shared(p["gla_fg_b"][1])))
    o_f = _make_gla(n_ctx, False)(q, k, v, logd_f)
    o_b = _make_gla(n_ctx, True)(q, k, v, logd_b)
    (gla,) = _rowop(_f_gla_finish, "gla_finish", lane_block=HEAD_W, out_dtypes=(BF16,))(
        (o_f, o_b, gate), (), (_shared(p["gla_onorm_g"]),))

    conv = _make_conv(n_ctx)
    (u,) = _rowop(_f_glu, "glu")((conf_a, conf_gate), (), ())
    (conf,) = _rowop(_f_ln_silu, "ln_silu", out_dtypes=(BF16,))(
        (conv(u, p["conf_dw"]),), (), (_shared(p["conf_dw_b"]), _shared(p["conf_ln_g"]), _shared(p["conf_ln_b"])))

    (ch,) = _rowop(_f_mul, "mul")((sc_c, sc_h), (), ())
    (sconv,) = _rowop(_f_mul, "mul_out", out_dtypes=(BF16,))((sc_b, conv(ch, p["sc_dw"])), (), ())

    (cq,) = _rowop(_f_rms, "rms")((cq,), (), (_shared(p["mla_q_norm_g"]),))
    (ckv,) = _rowop(_f_rms, "rms")((ckv,), (), (_shared(p["mla_kv_norm_g"]),))
    rope = _rowop(_f_rope, "rope", lane_block=LANES)
    (qr,) = rope((dense(cq, "w_qr"),), tables, ())
    (kr,) = rope((kr,), tables, ())
    mla, tokens = _make_attention(n_ctx)(dense(cq, "w_qn"), qr, dense(ckv, "w_kn"), kr, dense(ckv, "w_v"), slots)

    o, got_out = _make_mm_rows(0)(jnp.concatenate([gla, conf, sconv, mla], axis=1), G["w_out"], car["w_out"],
                                  take("w_out", "mla_w_uq", "mla_w_ukv"))
    t1, h2 = _rowop(_f_resid_norm_mod, "resid_norm_mod", n_ctx, out_dtypes=(F32, BF16))(
        (t, o), (), (g1, _shared(p["norm2_g"]), sc2, sh2))
    a1, got_w1 = _make_mm_cols(0)(h2, G["ffn_w1"], car["ffn_w1"], take("ffn_w1"))
    a3, got_w3 = _make_mm_cols(0)(h2, G["ffn_w3"], car["ffn_w3"], take("ffn_w3"))
    n_ff = a1.shape[2]
    (act,) = _rowop(_f_swiglu, "swiglu", tile=1024, out_dtypes=(BF16,))(
        (a1.reshape(N_DEV * L, n_ff), a3.reshape(N_DEV * L, n_ff)), (), ())
    f, got_w2 = _make_mm_rows(0)(act.reshape(N_DEV, L, n_ff), G["ffn_w2"], car["ffn_w2"], take("ffn_w2"))
    (t2,) = _rowop(_f_resid, "resid", n_ctx)((t1, f), (), (g2,))
    got = got_in + got_out + got_w1 + got_w3 + got_w2
    gathered = dict(zip(("w_in", "w_out", "mla_w_uq", "mla_w_ukv", "ffn_w1", "ffn_w3", "ffn_w2"), got)) if nxt else None
    return (t2, tokens), gathered


def _plain_weights(G, D):
    full = lambda n: jnp.concatenate([G[n][d, 0] for d in range(N_DEV)], axis=1)
    w_in = full("w_in")
    w_in = jnp.concatenate([w_in[:, :Z_LR + 2 * GATE_RANK], jnp.zeros((D, Z_A - Z_LR - 2 * GATE_RANK), BF16),
                            w_in[:, Z_LR + 2 * GATE_RANK:], jnp.zeros((D, Z_END - Z_KR - ROPE), BF16)], axis=1)
    w_uq = full("mla_w_uq").reshape(Q_RANK, HEADS, HEAD_W + ROPE)
    w_ukv = full("mla_w_ukv").reshape(KV_RANK, HEADS, 2 * HEAD_W)
    return {"w_in": w_in,
            "w_qn": w_uq[:, :, :HEAD_W].reshape(Q_RANK, MIX_W),
            "w_qr": jnp.pad(w_uq[:, :, HEAD_W:], ((0, 0), (0, 0), (0, LANES - ROPE))).reshape(Q_RANK, HEADS * LANES),
            "w_kn": w_ukv[:, :, :HEAD_W].reshape(KV_RANK, MIX_W),
            "w_v": w_ukv[:, :, HEAD_W:].reshape(KV_RANK, MIX_W)}


def _col_slabs(full):
    n = full.shape[1] // N_DEV
    return jnp.stack([full[:, d * n:(d + 1) * n] for d in range(N_DEV)])


def _shard_grads(d_car):
    d_in = d_car["w_in"]
    d_in = jnp.concatenate([d_in[:, :Z_LR + 2 * GATE_RANK], d_in[:, Z_A:Z_KR + ROPE]], axis=1)
    by_head = lambda g: g.reshape(g.shape[0], HEADS, -1)
    d_uq = jnp.concatenate([by_head(d_car["w_qn"]), by_head(d_car["w_qr"])[:, :, :ROPE]], axis=2)
    d_ukv = jnp.concatenate([by_head(d_car["w_kn"]), by_head(d_car["w_v"])], axis=2)
    out = {"w_in": _col_slabs(d_in), "mla_w_uq": _col_slabs(d_uq.reshape(Q_RANK, -1)),
           "mla_w_ukv": _col_slabs(d_ukv.reshape(KV_RANK, -1))}
    out.update({n: d_car[n] for n in COLS + ROWS})
    return out


BIG = ("w_in", "w_out", "ffn_w1", "ffn_w3", "ffn_w2", "mla_w_uq", "mla_w_ukv")
SMALL_SHARED = ("norm1_g", "gla_onorm_g", "conf_dw_b", "conf_ln_g", "conf_ln_b", "mla_q_norm_g", "mla_kv_norm_g",
                "norm2_g")
SMALL_SHARDED = ("gla_fg_up", "gla_fg_b", "conf_dw", "sc_dw")
WEIGHTS = ("c_ctx", "norm1_g", "w_mod", "b_mod", "w_in", "gla_fg_up", "gla_fg_b", "gla_onorm_g", "conf_dw",
           "conf_dw_b", "conf_ln_g", "conf_ln_b", "sc_dw", "mla_q_norm_g", "mla_kv_norm_g", "mla_w_uq", "mla_w_ukv",
           "w_out", "norm2_g", "ffn_w1", "ffn_w3", "ffn_w2", "final_norm_g")


def _gather_last(pieces8):
    moved = jnp.moveaxis(pieces8, 0, -2)
    return moved.reshape(moved.shape[:-2] + (-1,))


def _sum_devices(x8, name):
    shape = x8.shape[1:]
    return _sum_leading(x8.reshape(N_DEV, -1, shape[-1]), name).reshape(shape)


def _step(w, m, v, x, c, ctx, loss_target):
    depth = w["norm1_g"].shape[0]
    seq, D = x.shape[1], x.shape[2]
    n_ctx = ctx.shape[1]
    me = 4 * lax.axis_index("x") + 2 * lax.axis_index("y") + lax.axis_index("c")

    shards = [{n: w[n][i:i + 1].astype(BF16) for n in BIG} for i in range(depth)]
    G = dict(zip(BIG, _all_gather([shards[0][n] for n in BIG], "gather_weights")))
    small8 = _all_gather([c] + [w[n] for n in SMALL_SHARDED], "gather_small")
    c_all = small8[0].reshape(N_DEV, D)
    small_full = {n: _gather_last(g) for n, g in zip(SMALL_SHARDED, small8[1:])}

    rows = jnp.concatenate([c_all, w["c_ctx"][None], jnp.zeros((16 - N_DEV - 1, D), F32)])
    (act,) = _rowop(_f_silu, "silu")((rows,), (), ())
    n_mod = w["w_mod"].shape[2]
    b_mine = lax.dynamic_slice_in_dim(w["b_mod"], me * n_mod, n_mod, axis=1)
    mod_cols = [_rowop(_f_add_bias, "add_bias")((_matmul(act, w["w_mod"][i], "nn", "mod_fwd"),), (),
                                                (_shared(b_mine[i]),))[0] for i in range(depth)]
    (mods8,) = _all_gather([jnp.stack(mod_cols)], "gather_mod")
    mods = jnp.moveaxis(mods8, 0, 2).reshape(depth, 16, N_DEV * n_mod)
    mods_l = [lax.dynamic_index_in_dim(mods[i], me, 0, keepdims=False) for i in range(depth)]
    mods_c = [mods[i, N_DEV] for i in range(depth)]

    smalls, cars = [], []
    plain_shapes = {"w_in": (D, Z_END), "w_qn": (Q_RANK, MIX_W), "w_qr": (Q_RANK, HEADS * LANES),
                    "w_kn": (KV_RANK, MIX_W), "w_v": (KV_RANK, MIX_W)}
    for i in range(depth):
        p = {n: w[n][i] for n in SMALL_SHARED}
        p.update({n: small_full[n][i] for n in SMALL_SHARDED})
        smalls.append(p)
        car = {n: lax.empty(s, BF16) for n, s in plain_shapes.items()}
        car.update({n: lax.empty((N_DEV,) + w[n].shape[1:], BF16) for n in COLS + ROWS})
        cars.append(car)
    t = jnp.concatenate([ctx[0], x[0]], axis=0)
    tables = _rope_tables(seq, n_ctx)
    vjps = []
    for i in range(depth):
        last = i + 1 == depth
        slots = () if last else tuple(lax.empty((3,) + w[n].shape[1:], BF16) for n in BIG)
        (t, _), vjp_i, G = jax.vjp(
            lambda t_, p_, c_, ml_, mc_, s_, G=G, nxt=None if last else shards[i + 1]: _layer(
                t_, p_, c_, ml_, mc_, s_, G, nxt, n_ctx, tables),
            t, smalls[i], cars[i], mods_l[i], mods_c[i], slots, has_aux=True)
        vjps.append(vjp_i)
    loss_lanes, d_out, d_final_g = _loss_call(t[n_ctx:], w["final_norm_g"][None], loss_target[0])
    d_t = jnp.concatenate([jnp.zeros((n_ctx, D), F32), d_out], axis=0)

    grads = {n: [None] * depth for n in BIG}
    d_smalls, d_mods_l, d_mods_c = [None] * depth, [None] * depth, [None] * depth
    pending = ()
    for i in reversed(range(depth)):
        d_t, d_smalls[i], d_car, d_mods_l[i], d_mods_c[i], arrived = vjps[i]((d_t, tuple(pending)))
        for n, g in zip(BIG, _sum_chips(pending, arrived)):
            grads[n][i + 1] = g
        sharded = _shard_grads(d_car)
        pending = _chip_sums([sharded[n] for n in BIG])
    for n, g in zip(BIG, _sum_chips(pending, _send_to_chips(pending, "rs_chips"))):
        grads[n][0] = g
    grads = {n: jnp.stack(g) for n, g in grads.items()}
    grad_x = d_t[n_ctx:][None]

    names = SMALL_SHARED + SMALL_SHARDED
    d_mod = jnp.stack([jnp.stack([d_mods_l[i], d_mods_c[i]]) for i in range(depth)])
    parts = [loss_lanes, d_final_g, d_mod] + [jnp.stack([d_smalls[i][n] for i in range(depth)]) for n in names]
    parts8 = _all_gather(parts, "gather_partials")
    summed = [_sum_devices(p8, "sum_partials") for p8 in parts8]
    (loss_row,) = _rowop(lambda a: (jnp.sum(a, axis=-1, keepdims=True) + jnp.zeros_like(a),), "loss_sum")(
        (summed[0],), (), ())
    loss = loss_row[0, 0]
    grads["final_norm_g"] = summed[1].reshape(D)
    for n, g in zip(names, summed[3:]):
        if n in SMALL_SHARDED:
            g = lax.dynamic_slice_in_dim(g, me * w[n].shape[-1], w[n].shape[-1], axis=g.ndim - 1)
        grads[n] = g

    d_mod8, d_mod_c = parts8[2], summed[2][:, 1]
    grads["b_mod"] = _rowop(lambda a, b: (a + b,), "add")((summed[2][:, 0], d_mod_c), (), ())[0]
    d_rows = jnp.concatenate([jnp.moveaxis(d_mod8[:, :, 0], 0, 1), d_mod_c[:, None],
                              jnp.zeros((depth, 16 - N_DEV - 1, 6 * D), F32)], axis=1)
    d_rows = lax.dynamic_slice_in_dim(d_rows, me * n_mod, n_mod, axis=2)
    grads["w_mod"] = jnp.stack([_matmul(act, d_rows[i], "tn", "mod_dw") for i in range(depth)])
    d_act = _matmul(d_rows.transpose(1, 0, 2).reshape(16, depth * n_mod),
                    w["w_mod"].transpose(1, 0, 2).reshape(D, depth * n_mod), "nt", "mod_dact")
    (d_act8,) = _all_gather([d_act], "gather_dact")
    (d_rows_in,) = _rowop(_f_mul_silu_grad, "silu_grad")((_sum_devices(d_act8, "sum_dact"), rows), (), ())
    grads["c_ctx"] = d_rows_in[N_DEV]

    outs = {n: _adamw(w[n], grads[n], m[n], v[n]) for n in WEIGHTS}
    return (loss, grad_x, *[grads[n] for n in WEIGHTS], *[outs[n][0] for n in WEIGHTS],
            *[outs[n][1] for n in WEIGHTS], *[outs[n][2] for n in WEIGHTS])


def kernel(x, c, ctx, c_ctx, norm1_g, w_mod, b_mod, w_in, gla_fg_up, gla_fg_b, gla_onorm_g, conf_dw, conf_dw_b, conf_ln_g, conf_ln_b, sc_dw, mla_q_norm_g, mla_kv_norm_g, mla_w_uq, mla_w_ukv, w_out, norm2_g, ffn_w1, ffn_w3, ffn_w2, final_norm_g, loss_target, m_c_ctx, m_norm1_g, m_w_mod, m_b_mod, m_w_in, m_gla_fg_up, m_gla_fg_b, m_gla_onorm_g, m_conf_dw, m_conf_dw_b, m_conf_ln_g, m_conf_ln_b, m_sc_dw, m_mla_q_norm_g, m_mla_kv_norm_g, m_mla_w_uq, m_mla_w_ukv, m_w_out, m_norm2_g, m_ffn_w1, m_ffn_w3, m_ffn_w2, m_final_norm_g, v_c_ctx, v_norm1_g, v_w_mod, v_b_mod, v_w_in, v_gla_fg_up, v_gla_fg_b, v_gla_onorm_g, v_conf_dw, v_conf_dw_b, v_conf_ln_g, v_conf_ln_b, v_sc_dw, v_mla_q_norm_g, v_mla_kv_norm_g, v_mla_w_uq, v_mla_w_ukv, v_w_out, v_norm2_g, v_ffn_w1, v_ffn_w3, v_ffn_w2, v_final_norm_g):
    w = dict(c_ctx=c_ctx, norm1_g=norm1_g, w_mod=w_mod, b_mod=b_mod, w_in=w_in, gla_fg_up=gla_fg_up, gla_fg_b=gla_fg_b, gla_onorm_g=gla_onorm_g, conf_dw=conf_dw, conf_dw_b=conf_dw_b, conf_ln_g=conf_ln_g, conf_ln_b=conf_ln_b, sc_dw=sc_dw, mla_q_norm_g=mla_q_norm_g, mla_kv_norm_g=mla_kv_norm_g, mla_w_uq=mla_w_uq, mla_w_ukv=mla_w_ukv, w_out=w_out, norm2_g=norm2_g, ffn_w1=ffn_w1, ffn_w3=ffn_w3, ffn_w2=ffn_w2, final_norm_g=final_norm_g)
    m = dict(c_ctx=m_c_ctx, norm1_g=m_norm1_g, w_mod=m_w_mod, b_mod=m_b_mod, w_in=m_w_in, gla_fg_up=m_gla_fg_up, gla_fg_b=m_gla_fg_b, gla_onorm_g=m_gla_onorm_g, conf_dw=m_conf_dw, conf_dw_b=m_conf_dw_b, conf_ln_g=m_conf_ln_g, conf_ln_b=m_conf_ln_b, sc_dw=m_sc_dw, mla_q_norm_g=m_mla_q_norm_g, mla_kv_norm_g=m_mla_kv_norm_g, mla_w_uq=m_mla_w_uq, mla_w_ukv=m_mla_w_ukv, w_out=m_w_out, norm2_g=m_norm2_g, ffn_w1=m_ffn_w1, ffn_w3=m_ffn_w3, ffn_w2=m_ffn_w2, final_norm_g=m_final_norm_g)
    v = dict(c_ctx=v_c_ctx, norm1_g=v_norm1_g, w_mod=v_w_mod, b_mod=v_b_mod, w_in=v_w_in, gla_fg_up=v_gla_fg_up, gla_fg_b=v_gla_fg_b, gla_onorm_g=v_gla_onorm_g, conf_dw=v_conf_dw, conf_dw_b=v_conf_dw_b, conf_ln_g=v_conf_ln_g, conf_ln_b=v_conf_ln_b, sc_dw=v_sc_dw, mla_q_norm_g=v_mla_q_norm_g, mla_kv_norm_g=v_mla_kv_norm_g, mla_w_uq=v_mla_w_uq, mla_w_ukv=v_mla_w_ukv, w_out=v_w_out, norm2_g=v_norm2_g, ffn_w1=v_ffn_w1, ffn_w3=v_ffn_w3, ffn_w2=v_ffn_w2, final_norm_g=v_final_norm_g)
    return _step(w, m, v, x, c, ctx, loss_target)
```
